```python
import jax
import jax.numpy as jnp
from jax import lax
import numpy as np

D_MODEL = 1024
BATCH = 16
SEQ = 4096
DEPTH = 2
DEC_BATCH = 8
DEC_SEQ = 2048
PAST_LEN = 128

BRANCH_W = 512
HG_HEADS = 4
HG_DK = 128
HG_DV = BRANCH_W // HG_HEADS
HG_WIDTH = HG_HEADS * HG_DK
HG_VWIDTH = HG_HEADS * HG_DV
HG_CHUNK = 64
LB_EPS = 1e-20
RG_WIDTH = BRANCH_W
RG_BLOCKS = 8
RG_BLOCK_W = RG_WIDTH // RG_BLOCKS
RG_CONV_W = 4
RG_C = 8.0
ATT_GROUPS = ((128, 1), (512, 4), (2048, 16))
ATT_HEADS_PER_GROUP = 4
ATT_HEADS = ATT_HEADS_PER_GROUP * len(ATT_GROUPS)
ATT_DH = BRANCH_W // ATT_HEADS_PER_GROUP
ATT_QBLK = 64
N_BRANCHES = 3
FFN_DENSE = 2816
N_EXPERTS = 8
TOP_K = 2
FFN_EXPERT = 3584
EXPERT_ROWS = 256
N_DENSE = (DEPTH + 1) // 2
N_MOE = DEPTH // 2
IN_SIZES = (HG_WIDTH,) * 3 + (HG_VWIDTH,) * 2 + (RG_WIDTH,) * 2 + (ATT_HEADS * ATT_DH,) * 3 + (D_MODEL,) * N_BRANCHES
IN_COLS = sum(IN_SIZES)
NORM_EPS = 1e-6
NEG_INF = -1e30

kernel_name = 'hybrid_bidir_encoder_block'


def rms_norm(x, g):
    xf = x.astype(jnp.float32)
    y = xf * lax.rsqrt(jnp.mean(xf * xf, axis=-1, keepdims=True) + NORM_EPS)
    return (y * g.astype(jnp.float32)).astype(x.dtype)


def hgrn2_scan(q, k, v, log_f):
    B, S, H, DK = q.shape
    DV = v.shape[-1]
    n_chunks = S // HG_CHUNK

    def to_chunks(t):
        return t.reshape(B, n_chunks, HG_CHUNK, H, t.shape[-1]).transpose(1, 0, 3, 2, 4)

    qc, kc, vc = to_chunks(q), to_chunks(k), to_chunks(v)
    ac = jnp.cumsum(to_chunks(log_f), axis=3)
    causal = jnp.tril(jnp.ones((HG_CHUNK, HG_CHUNK), dtype=bool))[:, :, None]

    def step(state, inp):
        q_n, k_n, v_n, a_n = inp
        diff = a_n[:, :, :, None, :] - a_n[:, :, None, :, :]
        decay = jnp.exp(jnp.where(causal, diff, NEG_INF))
        scores = jnp.einsum('bhtk,bhsk,bhtsk->bhts', q_n, k_n, decay)
        o = (jnp.einsum('bhts,bhsv->bhtv', scores, v_n)
             + jnp.einsum('bhtk,bhkv->bhtv', q_n * jnp.exp(a_n), state))
        a_last = a_n[:, :, -1:, :]
        k_end = k_n * jnp.exp(a_last - a_n)
        state = state * jnp.exp(a_last[:, :, 0, :, None]) + jnp.einsum('bhsk,bhsv->bhkv', k_end, v_n)
        return state, o

    state0 = jnp.zeros((B, H, DK, DV), jnp.float32)
    _, o = lax.scan(step, state0, (qc, kc, vc, ac))
    return o.transpose(1, 0, 3, 2, 4).reshape(B, S, H, DV)


def hgrn2_branch(q, f_fwd, f_bwd, v, g, lb_fwd, lb_bwd, norm_g):
    B, S, _ = q.shape

    def heads(t):
        return t.astype(jnp.float32).reshape(B, S, HG_HEADS, -1)

    qh, vh = heads(q), heads(v)

    def direction(z, lb, reverse):
        zh = heads(z)
        lb = lb.reshape(HG_HEADS, HG_DK)
        log_f = jnp.logaddexp(jnp.log(jnp.maximum(lb, LB_EPS)),
                              jnp.log1p(-lb) + jax.nn.log_sigmoid(zh))
        kh = (1.0 - lb) * jax.nn.sigmoid(-zh)
        if reverse:
            flip = lambda t: jnp.flip(t, axis=1)
            return flip(hgrn2_scan(flip(qh), flip(kh), flip(vh), flip(log_f)))
        return hgrn2_scan(qh, kh, vh, log_f)

    o = direction(f_fwd, lb_fwd, False) + direction(f_bwd, lb_bwd, True)
    o = rms_norm(o, norm_g.reshape(HG_HEADS, HG_DV)).reshape(B, S, HG_VWIDTH)
    return o * jax.nn.silu(g.astype(jnp.float32))


def centred_depthwise_conv(x, w, b):
    S = x.shape[1]
    left = RG_CONV_W // 2
    xp = jnp.pad(x, ((0, 0), (left, RG_CONV_W - 1 - left), (0, 0)))
    y = b.astype(jnp.float32)
    for j in range(RG_CONV_W):
        y = y + xp[:, j:j + S] * w[j]
    return y


def linear_scan_combine(e1, e2):
    a1, b1 = e1
    a2, b2 = e2
    return a1 * a2, a2 * b1 + b2


def rglru_branch(xr, gate, conv_w, conv_b, w_a, b_a, w_x, b_x, lam):
    B, S, _ = xr.shape
    xc = centred_depthwise_conv(xr.astype(jnp.float32), conv_w, conv_b)
    xb = xc.reshape(B, S, RG_BLOCKS, RG_BLOCK_W)

    def block_diag(w, b):
        return jnp.einsum('bsnc,ncd->bsnd', xb, w).reshape(B, S, RG_WIDTH) + b

    def direction(d, reverse):
        r = jax.nn.sigmoid(block_diag(w_a[d], b_a[d]))
        i = jax.nn.sigmoid(block_diag(w_x[d], b_x[d]))
        log_a = -RG_C * r * jax.nn.softplus(-lam[d])
        u = jnp.sqrt(-jnp.expm1(2.0 * log_a)) * (i * xc)
        _, h = lax.associative_scan(linear_scan_combine, (jnp.exp(log_a), u), reverse=reverse, axis=1)
        return h

    h = direction(0, False) + direction(1, True)
    return jax.nn.gelu(gate.astype(jnp.float32)) * h


def dilated_window_attention(q, k, v, slopes, radius, dil):
    B, S, G, DH = q.shape
    L = S // dil
    n_blk = -(-L // ATT_QBLK)
    Lp = n_blk * ATT_QBLK
    side = -(-radius // ATT_QBLK)

    def strided(t):
        return t.reshape(B, L, dil, G, DH).transpose(0, 2, 3, 1, 4)

    pad_q = ((0, 0), (0, 0), (0, 0), (0, Lp - L), (0, 0))
    pad_kv = ((0, 0), (0, 0), (0, 0), (side * ATT_QBLK, Lp - L + side * ATT_QBLK), (0, 0))
    qb = jnp.pad(strided(q), pad_q).reshape(B, dil, G, n_blk, ATT_QBLK, DH)

    def windows(t):
        tp = jnp.pad(strided(t), pad_kv).reshape(B, dil, G, n_blk + 2 * side, ATT_QBLK, DH)
        return jnp.concatenate([tp[:, :, :, j:j + n_blk] for j in range(2 * side + 1)], axis=4)

    kw, vw = windows(k), windows(v)
    qi = jnp.arange(ATT_QBLK)
    kj = jnp.arange((2 * side + 1) * ATT_QBLK) - side * ATT_QBLK
    rel = jnp.abs(kj[None, :] - qi[:, None])
    key_idx = jnp.arange(n_blk)[:, None] * ATT_QBLK + kj[None, :]
    valid = (rel <= radius)[None] & ((key_idx >= 0) & (key_idx < L))[:, None, :]
    bias = -slopes[:, None, None] * (rel * dil).astype(jnp.float32)
    s = jnp.einsum('brgnqd,brgnkd->brgnqk', qb, kw) * (DH ** -0.5) + bias[None, None, :, None]
    s = jnp.where(valid[None, None, None], s, NEG_INF)
    lse = jax.nn.logsumexp(s, axis=-1)
    o = jnp.einsum('brgnqk,brgnkd->brgnqd', jnp.exp(s - lse[..., None]), vw)
    o = o.reshape(B, dil, G, Lp, DH)[:, :, :, :L].transpose(0, 3, 1, 2, 4).reshape(B, S, G, DH)
    lse = lse.reshape(B, dil, G, Lp)[:, :, :, :L].transpose(0, 3, 1, 2).reshape(B, S, G)
    return o, lse


def dilated_attention_branch(q, k, v, q_g, k_g):
    B, S, _ = q.shape

    def heads(t):
        return t.astype(jnp.float32).reshape(B, S, ATT_HEADS, ATT_DH)

    qh, kh, vh = rms_norm(heads(q), q_g), rms_norm(heads(k), k_g), heads(v)
    slopes = 2.0 ** (-8.0 * jnp.arange(1, ATT_HEADS + 1, dtype=jnp.float32) / ATT_HEADS)
    outs, lses = [], []
    for gi, (window, dil) in enumerate(ATT_GROUPS):
        hs = slice(gi * ATT_HEADS_PER_GROUP, (gi + 1) * ATT_HEADS_PER_GROUP)
        o, lse = dilated_window_attention(qh[:, :, hs], kh[:, :, hs], vh[:, :, hs], slopes[hs],
                                          window // (2 * dil), dil)
        outs.append(o)
        lses.append(lse)
    w = jax.nn.softmax(jnp.stack(lses), axis=0)
    o = jnp.sum(w[..., None] * jnp.stack(outs), axis=0)
    return o.reshape(B, S, ATT_HEADS_PER_GROUP * ATT_DH)


def swiglu(h, w1, w3, w2):
    return (jax.nn.silu(h @ w1) * (h @ w3)) @ w2


def moe_swiglu(x2, router, w1, w3, w2):
    T, D = x2.shape
    logits = (x2 @ router).astype(jnp.float32)
    top_logits, top_idx = lax.top_k(logits, TOP_K)
    gates = jax.nn.softmax(top_logits, axis=-1)
    n_assign = T * TOP_K
    expert = top_idx.reshape(n_assign)
    token = jnp.repeat(jnp.arange(T), TOP_K)
    order = jnp.argsort(expert)
    e_sorted, tok_sorted = expert[order], token[order]
    g_sorted = gates.reshape(n_assign)[order]
    counts = jnp.bincount(expert, length=N_EXPERTS)
    starts = jnp.cumsum(counts) - counts
    padded = (counts + EXPERT_ROWS - 1) // EXPERT_ROWS * EXPERT_ROWS
    pends = jnp.cumsum(padded)
    dest = pends[e_sorted] - padded[e_sorted] + (jnp.arange(n_assign) - starts[e_sorted])
    n_blocks = -(-n_assign // EXPERT_ROWS) + N_EXPERTS
    rows = jnp.zeros((n_blocks * EXPERT_ROWS, D), x2.dtype).at[dest].set(x2[tok_sorted])
    block_expert = jnp.minimum(
        jnp.searchsorted(pends, jnp.arange(n_blocks) * EXPERT_ROWS, side='right'), N_EXPERTS - 1)

    def expert_block(args):
        xb, e = args
        return (jax.nn.silu(xb @ w1[e]) * (xb @ w3[e])) @ w2[e]

    yb = lax.map(expert_block, (rows.reshape(n_blocks, EXPERT_ROWS, D), block_expert))
    y_rows = yb.reshape(n_blocks * EXPERT_ROWS, D)[dest] * g_sorted[:, None]
    return jnp.zeros((T, D), jnp.float32).at[tok_sorted].add(y_rows)


def mixer_sublayer(x, l, p, lb):
    h = rms_norm(x, p['norm_mix_g'][l])
    proj = h @ p['w_in'][l]
    (hq, hf_fwd, hf_bwd, hv, hg, rx, rgate, aq, ak, av, ga, gb, gc) = jnp.split(
        proj, np.cumsum(IN_SIZES)[:-1].tolist(), axis=-1)
    branches = (
        hgrn2_branch(hq, hf_fwd, hf_bwd, hv, hg, lb[0], lb[1], p['hg_norm_g'][l]),
        rglru_branch(rx, rgate, p['rg_conv_w'][l], p['rg_conv_b'][l], p['rg_w_a'][l], p['rg_b_a'][l],
                     p['rg_w_x'][l], p['rg_b_x'][l], p['rg_lambda'][l]),
        dilated_attention_branch(aq, ak, av, p['attn_q_g'][l], p['attn_k_g'][l]),
    )
    merged = 0.0
    for n, (gl, o) in enumerate(zip((ga, gb, gc), branches)):
        merged = merged + jax.nn.sigmoid(gl.astype(jnp.float32)) * (o.astype(x.dtype) @ p['w_branch'][l, n])
    return x + (merged.astype(x.dtype) @ p['w_out'][l])


def channel_sublayer(x, l, p):
    h = rms_norm(x, p['norm_ffn_g'][l])
    j = l // 2
    if l % 2 == 0:
        y = swiglu(h, p['ffn_w1'][j], p['ffn_w3'][j], p['ffn_w2'][j])
    else:
        B, S, D = h.shape
        y = moe_swiglu(h.reshape(B * S, D), p['moe_router'][j], p['moe_w1'][j], p['moe_w3'][j],
                       p['moe_w2'][j]).reshape(B, S, D)
    return x + y.astype(x.dtype)


def trunk(x, p):
    sm = jax.nn.softmax(p['hg_lb_logits'].astype(jnp.float32), axis=1)
    lower_bounds = jnp.cumsum(sm, axis=1) - sm[:, :1]
    for l in range(DEPTH):
        x = mixer_sublayer(x, l, p, lower_bounds[:, l])
        x = channel_sublayer(x, l, p)
    return x


def setup_inputs(seed: int = 0) -> dict:
    key = jax.random.key(seed)
    ks = jax.random.split(key, 25)
    f32 = jnp.float32

    def normal(k, shape, scale):
        return jax.random.normal(k, shape, f32) * scale

    L = DEPTH
    u = jax.random.uniform(ks[12], (L, 2, RG_WIDTH), f32, 0.9, 0.999)
    p_base = u ** (1.0 / RG_C)
    rg_lambda = jnp.log(p_base) - jnp.log1p(-p_base)
    blk = (L, 2, RG_BLOCKS, RG_BLOCK_W, RG_BLOCK_W)
    return {
        'x_prompt': normal(ks[0], (BATCH, SEQ, D_MODEL), 1.0),
        'x_sample': normal(ks[1], (DEC_BATCH, DEC_SEQ, D_MODEL), 1.0),
        'norm_mix_g': 1.0 + normal(ks[2], (L, D_MODEL), 0.02),
        'w_in': normal(ks[3], (L, D_MODEL, IN_COLS), D_MODEL ** -0.5),
        'hg_lb_logits': 1.0 + normal(ks[4], (2, L, HG_WIDTH), 0.1),
        'hg_norm_g': 1.0 + normal(ks[5], (L, HG_VWIDTH), 0.02),
        'rg_conv_w': normal(ks[6], (L, RG_CONV_W, RG_WIDTH), RG_CONV_W ** -0.5),
        'rg_conv_b': normal(ks[7], (L, RG_WIDTH), 0.02),
        'rg_w_a': normal(ks[8], blk, RG_BLOCK_W ** -0.5),
        'rg_b_a': normal(ks[9], (L, 2, RG_WIDTH), 0.02),
        'rg_w_x': normal(ks[10], blk, RG_BLOCK_W ** -0.5),
        'rg_b_x': normal(ks[11], (L, 2, RG_WIDTH), 0.02),
        'rg_lambda': rg_lambda,
        'attn_q_g': 1.0 + normal(ks[13], (L, ATT_DH), 0.02),
        'attn_k_g': 1.0 + normal(ks[14], (L, ATT_DH), 0.02),
        'w_branch': normal(ks[15], (L, N_BRANCHES, BRANCH_W, D_MODEL), BRANCH_W ** -0.5),
        'w_out': normal(ks[16], (L, D_MODEL, D_MODEL), D_MODEL ** -0.5),
        'norm_ffn_g': 1.0 + normal(ks[17], (L, D_MODEL), 0.02),
        'ffn_w1': normal(ks[18], (N_DENSE, D_MODEL, FFN_DENSE), D_MODEL ** -0.5),
        'ffn_w3': normal(ks[19], (N_DENSE, D_MODEL, FFN_DENSE), D_MODEL ** -0.5),
        'ffn_w2': normal(ks[20], (N_DENSE, FFN_DENSE, D_MODEL), FFN_DENSE ** -0.5),
        'moe_router': normal(ks[21], (N_MOE, D_MODEL, N_EXPERTS), D_MODEL ** -0.5),
        'moe_w1': normal(ks[22], (N_MOE, N_EXPERTS, D_MODEL, FFN_EXPERT), D_MODEL ** -0.5),
        'moe_w3': normal(ks[23], (N_MOE, N_EXPERTS, D_MODEL, FFN_EXPERT), D_MODEL ** -0.5),
        'moe_w2': normal(ks[24], (N_MOE, N_EXPERTS, FFN_EXPERT, D_MODEL), FFN_EXPERT ** -0.5),
    }


def reference(x_prompt, x_sample, norm_mix_g, w_in, hg_lb_logits, hg_norm_g, rg_conv_w, rg_conv_b,
              rg_w_a, rg_b_a, rg_w_x, rg_b_x, rg_lambda, attn_q_g, attn_k_g, w_branch, w_out,
              norm_ffn_g, ffn_w1, ffn_w3, ffn_w2, moe_router, moe_w1, moe_w3, moe_w2):
    p = dict(norm_mix_g=norm_mix_g, w_in=w_in, hg_lb_logits=hg_lb_logits, hg_norm_g=hg_norm_g,
             rg_conv_w=rg_conv_w, rg_conv_b=rg_conv_b, rg_w_a=rg_w_a, rg_b_a=rg_b_a, rg_w_x=rg_w_x,
             rg_b_x=rg_b_x, rg_lambda=rg_lambda, attn_q_g=attn_q_g, attn_k_g=attn_k_g,
             w_branch=w_branch, w_out=w_out, norm_ffn_g=norm_ffn_g, ffn_w1=ffn_w1, ffn_w3=ffn_w3,
             ffn_w2=ffn_w2, moe_router=moe_router, moe_w1=moe_w1, moe_w3=moe_w3, moe_w2=moe_w2)
    y_prompt = trunk(x_prompt, p)
    y_sample = trunk(x_sample, p)
    return (y_prompt, y_sample)
```

```python
import functools
import math

import jax
import jax.numpy as jnp
from jax import lax
from jax.experimental import pallas as pl
from jax.experimental.pallas import tpu as pltpu

F32 = jnp.float32
BF16 = jnp.bfloat16

LANES = 128
SUBLANES = 8
VMEM_LIMIT_BYTES = 56 * 1024 * 1024

D_MODEL = 1024
DEPTH = 2
BRANCH_W = 512
HG_HEADS = 4
HG_DK = 128
HG_CHUNK = 64
HG_DIAG = 8
LB_EPS = 1e-20
RG_WIDTH = 512
RG_BLOCKS = 8
RG_BLOCK_W = RG_WIDTH // RG_BLOCKS
RG_CONV_W = 4
RG_C = 8.0
RG_TILE = 256
RG_HALO = 8
ATT_GROUPS = ((128, 1), (512, 4), (2048, 16))
ATT_HEADS_PER_GROUP = 4
ATT_HEADS = ATT_HEADS_PER_GROUP * len(ATT_GROUPS)
ATT_DH = 128
ATT_RADIUS = 64
ATT_QBLK = 128
FFN_DENSE = 2816
N_EXPERTS = 8
TOP_K = 2
FFN_EXPERT = 3584
NORM_EPS = 1e-6
NEG_INF = -1e30

CB_HQ, CB_HF, CB_HB, CB_HV, CB_HG = 0, 4, 8, 12, 16
CB_RX, CB_RGATE = 20, 24
CB_AQ, CB_AK, CB_AV = 28, 40, 52
CB_GATES = 64
IN_COLS = 11264
N_CB = IN_COLS // LANES

INPROJ_TM, INPROJ_TN = 1024, 1024
MERGE_TM = 256
FFN_TM, FFN_TF = 512, 1408
MOE_ROWS, MOE_TF = 512, 512
ROUTER_TM = 512
COMBINE_TM = 512


def _params(sem):
    return pltpu.CompilerParams(dimension_semantics=sem, vmem_limit_bytes=VMEM_LIMIT_BYTES)


def _dot(a, b):
    return jnp.dot(a, b, preferred_element_type=F32)


def _dot_nt(a, b):
    return lax.dot_general(a, b, (((1,), (1,)), ((), ())), preferred_element_type=F32)


def _sigmoid(x):
    return jax.nn.sigmoid(x)


def _inproj_kernel(x_ref, g_ref, w_ref, o_ref, h_ref):
    @pl.when(pl.program_id(1) == 0)
    def _():
        x = x_ref[...]
        ms = jnp.mean(x * x, axis=-1, keepdims=True)
        h_ref[...] = (x * lax.rsqrt(ms + NORM_EPS) * g_ref[...]).astype(BF16)

    acc = _dot(h_ref[...], w_ref[...])
    for c in range(INPROJ_TN // LANES):
        o_ref[c] = acc[:, c * LANES:(c + 1) * LANES].astype(BF16)


def _inproj(x, g, w):
    T = x.shape[0]
    return pl.pallas_call(
        _inproj_kernel,
        grid=(T // INPROJ_TM, IN_COLS // INPROJ_TN),
        in_specs=[
            pl.BlockSpec((INPROJ_TM, D_MODEL), lambda i, j: (i, 0)),
            pl.BlockSpec((1, D_MODEL), lambda i, j: (0, 0)),
            pl.BlockSpec((D_MODEL, INPROJ_TN), lambda i, j: (0, j)),
        ],
        out_specs=pl.BlockSpec((INPROJ_TN // LANES, INPROJ_TM, LANES), lambda i, j: (j, i, 0)),
        out_shape=jax.ShapeDtypeStruct((N_CB, T, LANES), BF16),
        scratch_shapes=[pltpu.VMEM((INPROJ_TM, D_MODEL), BF16)],
        compiler_params=_params(("parallel", "arbitrary")),
        name="inproj",
    )(x, g, w)


def _cumsum_rows(x, reverse):
    n = x.shape[0]
    row = lax.broadcasted_iota(jnp.int32, x.shape, 0)
    d = 1
    while d < n:
        if reverse:
            x = x + jnp.where(row < n - d, pltpu.roll(x, n - d, 0), 0.0)
        else:
            x = x + jnp.where(row >= d, pltpu.roll(x, d, 0), 0.0)
        d *= 2
    return x


def _hgrn_chunk(q, vb, z, c0, c1, oml, st, reverse):
    C = HG_CHUNK
    vf = vb.astype(F32)
    e = jnp.exp(-jnp.abs(z))
    log_sig = jnp.minimum(z, 0.0) - jnp.log1p(e)
    t = c1 + log_sig
    log_f = jnp.maximum(c0, t) + jnp.log1p(jnp.exp(-jnp.abs(c0 - t)))
    k = oml * _sigmoid(-z)
    a = _cumsum_rows(log_f, reverse)

    ti = lax.broadcasted_iota(jnp.int32, (C, C), 0)
    si = lax.broadcasted_iota(jnp.int32, (C, C), 1)
    scores = jnp.zeros((C, C), F32)
    h = C // 2
    while h >= HG_DIAG:
        sh = int(math.log2(h))
        pieces = []
        for b in range(C // (2 * h)):
            m = b * 2 * h + (h if reverse else h - 1)
            pieces.append(jnp.broadcast_to(a[m:m + 1, :], (2 * h, LANES)))
        ref = jnp.concatenate(pieces, axis=0) if len(pieces) > 1 else pieces[0]
        qt = (q * jnp.exp(jnp.minimum(a - ref, 0.0))).astype(BF16)
        kt = (k * jnp.exp(jnp.minimum(ref - a, 0.0))).astype(BF16)
        sc = _dot_nt(qt, kt)
        bt = lax.shift_right_logical(ti, sh)
        bs = lax.shift_right_logical(si, sh)
        if reverse:
            valid = (2 * (bs - bt) + (bt & 1)) == 2
        else:
            valid = (2 * (bt - bs) + (bs & 1)) == 2
        scores = jnp.where(valid, sc, scores)
        h //= 2

    row = lax.broadcasted_iota(jnp.int32, (C, LANES), 0)
    sub = row & (HG_DIAG - 1)
    ws, vs = [], []
    for dlt in range(HG_DIAG):
        if dlt == 0:
            ws.append((q * k).astype(BF16))
            vs.append(vf)
            continue
        shift = (C - dlt) if reverse else dlt
        a_s = pltpu.roll(a, shift, 0)
        k_s = pltpu.roll(k, shift, 0)
        v_s = pltpu.roll(vf, shift, 0)
        w = q * k_s * jnp.exp(jnp.minimum(a - a_s, 0.0))
        valid = (sub <= HG_DIAG - 1 - dlt) if reverse else (sub >= dlt)
        ws.append(jnp.where(valid, w, 0.0).astype(BF16))
        vs.append(v_s)
    r = _dot(jnp.concatenate(ws, axis=0), jnp.ones((LANES, LANES), BF16))
    o = _dot(scores.astype(BF16), vb)
    for dlt in range(HG_DIAG):
        o = o + r[dlt * C:(dlt + 1) * C, :] * vs[dlt]

    o = o + _dot_nt((q * jnp.exp(a)).astype(BF16), st.astype(BF16))
    a_last = a[0:1, :] if reverse else a[C - 1:C, :]
    k_end = (k * jnp.exp(a_last - a)).astype(BF16)
    st_new = st * jnp.exp(a_last) + _dot(vf.T.astype(BF16), k_end)
    return o, st_new


def _hgrn_kernel(q_ref, zf_ref, zb_ref, v_ref, g_ref, par_ref, o_ref, acc_ref, *, S):
    C = HG_CHUNK
    n = S // C
    par = par_ref[...]
    c0f, c1f, omlf = par[0:1], par[1:2], par[2:3]
    c0b, c1b, omlb = par[3:4], par[4:5], par[5:6]
    norm_g = par[6:7]

    def load(ci, z_ref):
        r0 = pl.multiple_of(ci * C, C)
        return (q_ref[pl.ds(r0, C), :].astype(F32), v_ref[pl.ds(r0, C), :],
                z_ref[pl.ds(r0, C), :].astype(F32), r0)

    def make_body(accumulate):
        def body(i, carry):
            stf, stb = carry
            q, vb, z, r0 = load(i, zf_ref)
            of, stf = _hgrn_chunk(q, vb, z, c0f, c1f, omlf, stf, False)
            j = n - 1 - i
            q, vb, z, r1 = load(j, zb_ref)
            ob, stb = _hgrn_chunk(q, vb, z, c0b, c1b, omlb, stb, True)
            if accumulate:
                acc_ref[pl.ds(r0, C), :] += of
                acc_ref[pl.ds(r1, C), :] += ob
            else:
                acc_ref[pl.ds(r0, C), :] = of
                acc_ref[pl.ds(r1, C), :] = ob
            return stf, stb
        return body

    zero = jnp.zeros((LANES, HG_DK), F32)
    carry = lax.fori_loop(0, n // 2, make_body(False), (zero, zero))
    lax.fori_loop(n // 2, n, make_body(True), carry)

    ft = 256

    def fin(i, _):
        r0 = pl.multiple_of(i * ft, ft)
        o = acc_ref[pl.ds(r0, ft), :]
        y = o * lax.rsqrt(jnp.mean(o * o, axis=-1, keepdims=True) + NORM_EPS) * norm_g
        g = g_ref[pl.ds(r0, ft), :].astype(F32)
        o_ref[pl.ds(r0, ft), :] = (y * (g * _sigmoid(g))).astype(BF16)
        return 0

    lax.fori_loop(0, S // ft, fin, 0)


def _hgrn(proj, par, B, S):
    def spec(cb0):
        return pl.BlockSpec((None, S, LANES), lambda b, h: (cb0 + h, b, 0))

    return pl.pallas_call(
        functools.partial(_hgrn_kernel, S=S),
        grid=(B, HG_HEADS),
        in_specs=[spec(CB_HQ), spec(CB_HF), spec(CB_HB), spec(CB_HV), spec(CB_HG),
                  pl.BlockSpec((None, SUBLANES, LANES), lambda b, h: (h, 0, 0))],
        out_specs=pl.BlockSpec((None, S, LANES), lambda b, h: (h, b, 0)),
        out_shape=jax.ShapeDtypeStruct((HG_HEADS, B * S, LANES), BF16),
        scratch_shapes=[pltpu.VMEM((S, LANES), F32)],
        compiler_params=_params(("parallel", "parallel")),
        name="hgrn2",
    )(proj, proj, proj, proj, proj, par)


def _linear_scan_rows(a, u, reverse):
    n = a.shape[0]
    row = lax.broadcasted_iota(jnp.int32, a.shape, 0)
    d = 1
    while d < n:
        if reverse:
            keep = row < n - d
            a_s = jnp.where(keep, pltpu.roll(a, n - d, 0), 1.0)
            u_s = jnp.where(keep, pltpu.roll(u, n - d, 0), 0.0)
        else:
            keep = row >= d
            a_s = jnp.where(keep, pltpu.roll(a, d, 0), 1.0)
            u_s = jnp.where(keep, pltpu.roll(u, d, 0), 0.0)
        u = a * u_s + u
        a = a * a_s
        d *= 2
    return a, u


def _rg_kernel(x_ref, gate_ref, par_ref, w_ref, o_ref, xpad_ref, h_ref, *, S):
    TS = RG_TILE
    n = S // TS
    par = par_ref[...]
    conv_b = par[4:5]

    xpad_ref[pl.ds(0, RG_HALO), :] = jnp.zeros((RG_HALO, LANES), F32)
    xpad_ref[pl.ds(S + RG_HALO, RG_HALO), :] = jnp.zeros((RG_HALO, LANES), F32)

    def copy(i, _):
        r0 = pl.multiple_of(i * TS, TS)
        xpad_ref[pl.ds(r0 + RG_HALO, TS), :] = x_ref[pl.ds(r0, TS), :].astype(F32)
        return 0

    lax.fori_loop(0, n, copy, 0)

    def tile(i, carry, d):
        r0 = pl.multiple_of(i * TS, TS)
        xe = xpad_ref[pl.ds(r0, TS + 2 * RG_HALO), :]
        xc = conv_b
        for j in range(RG_CONV_W):
            off = RG_HALO - RG_CONV_W // 2 + j
            xc = xc + par[j:j + 1] * xe[off:off + TS, :]
        gts = _dot(xc.astype(BF16), w_ref[:, d * 2 * LANES:(d + 1) * 2 * LANES])
        r = _sigmoid(gts[:, :LANES] + par[5 + 2 * d:6 + 2 * d])
        ig = _sigmoid(gts[:, LANES:] + par[6 + 2 * d:7 + 2 * d])
        lam = par[9 + d:10 + d]
        softplus_neg_lam = jnp.maximum(-lam, 0.0) + jnp.log1p(jnp.exp(-jnp.abs(lam)))
        log_a = (-RG_C) * r * softplus_neg_lam
        a = jnp.exp(log_a)
        u = jnp.sqrt(1.0 - jnp.exp(2.0 * log_a)) * (ig * xc)
        a_cum, hloc = _linear_scan_rows(a, u, reverse=(d == 1))
        hfull = hloc + a_cum * carry
        new_carry = hfull[0:1, :] if d == 1 else hfull[TS - 1:TS, :]
        return hfull, new_carry, r0

    def fwd(i, carry):
        hfull, carry, r0 = tile(i, carry, 0)
        h_ref[pl.ds(r0, TS), :] = hfull
        return carry

    def bwd(i, carry):
        j = n - 1 - i
        hfull, carry, r0 = tile(j, carry, 1)
        gate = gate_ref[pl.ds(r0, TS), :].astype(F32)
        o_ref[pl.ds(r0, TS), :] = (jax.nn.gelu(gate) * (h_ref[pl.ds(r0, TS), :] + hfull)).astype(BF16)
        return carry

    zero = jnp.zeros((1, LANES), F32)
    lax.fori_loop(0, n, fwd, zero)
    lax.fori_loop(0, n, bwd, zero)


def _rglru(proj, par, w4, B, S):
    ng = RG_WIDTH // LANES

    def spec(cb0):
        return pl.BlockSpec((None, S, LANES), lambda b, g: (cb0 + g, b, 0))

    return pl.pallas_call(
        functools.partial(_rg_kernel, S=S),
        grid=(B, ng),
        in_specs=[spec(CB_RX), spec(CB_RGATE),
                  pl.BlockSpec((None, 2 * SUBLANES, LANES), lambda b, g: (g, 0, 0)),
                  pl.BlockSpec((None, LANES, 4 * LANES), lambda b, g: (g, 0, 0))],
        out_specs=pl.BlockSpec((None, S, LANES), lambda b, g: (g, b, 0)),
        out_shape=jax.ShapeDtypeStruct((ng, B * S, LANES), BF16),
        scratch_shapes=[pltpu.VMEM((S + 2 * RG_HALO, LANES), F32), pltpu.VMEM((S, LANES), F32)],
        compiler_params=_params(("parallel", "parallel")),
        name="rglru",
    )(proj, proj, par, w4)


def _attn_kernel(q0_ref, k0_ref, v0_ref, q1_ref, k1_ref, v1_ref, q2_ref, k2_ref, v2_ref, par_ref, o_ref,
                 qf_ref, kf_ref, vf_ref, og_ref, lg_ref, *, S):
    refs = ((q0_ref, k0_ref, v0_ref), (q1_ref, k1_ref, v1_ref), (q2_ref, k2_ref, v2_ref))
    ct = 256

    for g, (_, dil) in enumerate(ATT_GROUPS):
        q_ref, k_ref, v_ref = refs[g]
        par = par_ref[g]
        qg, kg, slope = par[0:1], par[1:2], par[2:3, 0:1]
        L = S // dil
        Q = min(ATT_QBLK, L)
        KW = min(Q + 2 * ATT_RADIUS, L)
        nq = L // Q
        nq_shift = int(math.log2(nq))

        def to_f32(i, _, q_ref=q_ref, k_ref=k_ref, v_ref=v_ref, kg=kg):
            r0 = pl.multiple_of(i * ct, ct)
            qf_ref[pl.ds(r0, ct), :] = q_ref[pl.ds(r0, ct), :].astype(F32)
            kk = k_ref[pl.ds(r0, ct), :].astype(F32)
            kf_ref[pl.ds(r0, ct), :] = kk * lax.rsqrt(jnp.mean(kk * kk, axis=-1, keepdims=True) + NORM_EPS) * kg
            vf_ref[pl.ds(r0, ct), :] = v_ref[pl.ds(r0, ct), :].astype(F32)
            return 0

        lax.fori_loop(0, S // ct, to_f32, 0)

        def rows(start, size, dil=dil):
            if dil == 1:
                return pl.ds(start, size)
            return pl.ds(start, size, stride=dil)

        def qblock(it, _, g=g, dil=dil, L=L, Q=Q, KW=KW, nq=nq, nq_shift=nq_shift, qg=qg, slope=slope,
                   rows=rows):
            r = lax.shift_right_logical(it, nq_shift)
            m0 = (it & (nq - 1)) * Q
            ks = jnp.clip(m0 - ATT_RADIUS, 0, L - KW)
            qq = qf_ref[rows(m0 * dil + r, Q), :]
            qn = (qq * lax.rsqrt(jnp.mean(qq * qq, axis=-1, keepdims=True) + NORM_EPS) * qg).astype(BF16)
            kk = kf_ref[rows(ks * dil + r, KW), :].astype(BF16)
            vv = vf_ref[rows(ks * dil + r, KW), :].astype(BF16)
            s = _dot_nt(qn, kk)
            qi = m0 + lax.broadcasted_iota(jnp.int32, (Q, KW), 0)
            kj = ks + lax.broadcasted_iota(jnp.int32, (Q, KW), 1)
            rel = jnp.abs(qi - kj)
            s = s - slope * rel.astype(F32)
            s = jnp.where(rel <= ATT_RADIUS, s, NEG_INF)
            m = jnp.max(s, axis=-1, keepdims=True)
            p = jnp.exp(s - m)
            l = jnp.sum(p, axis=-1, keepdims=True)
            og_ref[g, rows(m0 * dil + r, Q), :] = _dot(p.astype(BF16), vv) / l
            lg_ref[g, rows(m0 * dil + r, Q), :] = jnp.broadcast_to(m + jnp.log(l), (Q, LANES))
            return 0

        lax.fori_loop(0, dil * nq, qblock, 0)

    def merge(i, _):
        r0 = pl.multiple_of(i * ct, ct)
        l0, l1, l2 = (lg_ref[g, pl.ds(r0, ct), :] for g in range(3))
        m = jnp.maximum(jnp.maximum(l0, l1), l2)
        e0, e1, e2 = jnp.exp(l0 - m), jnp.exp(l1 - m), jnp.exp(l2 - m)
        o = (e0 * og_ref[0, pl.ds(r0, ct), :] + e1 * og_ref[1, pl.ds(r0, ct), :]
             + e2 * og_ref[2, pl.ds(r0, ct), :]) / (e0 + e1 + e2)
        o_ref[pl.ds(r0, ct), :] = o.astype(BF16)
        return 0

    lax.fori_loop(0, S // ct, merge, 0)


def _attention(proj, par, B, S):
    H = ATT_HEADS_PER_GROUP
    ngroups = len(ATT_GROUPS)

    def spec(cb0, g):
        return pl.BlockSpec((None, S, LANES), lambda b, j: (cb0 + g * H + j, b, 0))

    in_specs = []
    for g in range(ngroups):
        in_specs += [spec(CB_AQ, g), spec(CB_AK, g), spec(CB_AV, g)]
    in_specs.append(pl.BlockSpec((None, ngroups, SUBLANES, LANES), lambda b, j: (j, 0, 0, 0)))
    return pl.pallas_call(
        functools.partial(_attn_kernel, S=S),
        grid=(B, H),
        in_specs=in_specs,
        out_specs=pl.BlockSpec((None, S, LANES), lambda b, j: (j, b, 0)),
        out_shape=jax.ShapeDtypeStruct((H, B * S, LANES), BF16),
        scratch_shapes=[pltpu.VMEM((S, LANES), F32), pltpu.VMEM((S, LANES), F32), pltpu.VMEM((S, LANES), F32),
                        pltpu.VMEM((ngroups, S, LANES), F32), pltpu.VMEM((ngroups, S, LANES), F32)],
        compiler_params=_params(("parallel", "parallel")),
        name="attention",
    )(*([proj] * (3 * ngroups)), par)


def _cat_heads(ref):
    return jnp.concatenate([ref[c] for c in range(ref.shape[0])], axis=-1)


def _merge_kernel(x_ref, hg_ref, rg_ref, att_ref, ga_ref, gb_ref, gc_ref, wb_ref, wo_ref, ng_ref,
                  x1_ref, h2_ref):
    merged = _sigmoid(_cat_heads(ga_ref).astype(F32)) * _dot(_cat_heads(hg_ref), wb_ref[0])
    merged += _sigmoid(_cat_heads(gb_ref).astype(F32)) * _dot(_cat_heads(rg_ref), wb_ref[1])
    merged += _sigmoid(_cat_heads(gc_ref).astype(F32)) * _dot(_cat_heads(att_ref), wb_ref[2])
    x1 = x_ref[...] + _dot(merged.astype(BF16), wo_ref[...])
    x1_ref[...] = x1
    ms = jnp.mean(x1 * x1, axis=-1, keepdims=True)
    h2_ref[...] = (x1 * lax.rsqrt(ms + NORM_EPS) * ng_ref[...]).astype(BF16)


def _merge(x, proj, hg, rg, att, wb, wo, ng):
    T = x.shape[0]
    tm = MERGE_TM
    nb = BRANCH_W // LANES
    ngate = D_MODEL // LANES

    branch = pl.BlockSpec((nb, tm, LANES), lambda i: (0, i, 0))

    def gate(n):
        return pl.BlockSpec((ngate, tm, LANES), lambda i: (CB_GATES // ngate + n, i, 0))

    row = pl.BlockSpec((tm, D_MODEL), lambda i: (i, 0))
    return pl.pallas_call(
        _merge_kernel,
        grid=(T // tm,),
        in_specs=[row, branch, branch, branch, gate(0), gate(1), gate(2),
                  pl.BlockSpec((3, BRANCH_W, D_MODEL), lambda i: (0, 0, 0)),
                  pl.BlockSpec((D_MODEL, D_MODEL), lambda i: (0, 0)),
                  pl.BlockSpec((1, D_MODEL), lambda i: (0, 0))],
        out_specs=[row, row],
        out_shape=[jax.ShapeDtypeStruct((T, D_MODEL), F32), jax.ShapeDtypeStruct((T, D_MODEL), BF16)],
        compiler_params=_params(("parallel",)),
        name="merge",
    )(x, hg, rg, att, proj, proj, proj, wb, wo, ng)


def _ffn_kernel(h_ref, x_ref, w1_ref, w3_ref, w2_ref, o_ref, acc_ref):
    f = pl.program_id(1)
    h = h_ref[...]
    h1 = _dot(h, w1_ref[...])
    y = _dot((h1 * _sigmoid(h1) * _dot(h, w3_ref[...])).astype(BF16), w2_ref[...])

    @pl.when(f == 0)
    def _():
        acc_ref[...] = x_ref[...] + y

    @pl.when(f > 0)
    def _():
        acc_ref[...] += y

    @pl.when(f == pl.num_programs(1) - 1)
    def _():
        o_ref[...] = acc_ref[...]


def _ffn_dense(h2, x1, w1, w3, w2):
    T = h2.shape[0]
    tm, tf = FFN_TM, FFN_TF
    return pl.pallas_call(
        _ffn_kernel,
        grid=(T // tm, FFN_DENSE // tf),
        in_specs=[pl.BlockSpec((tm, D_MODEL), lambda i, f: (i, 0)),
                  pl.BlockSpec((tm, D_MODEL), lambda i, f: (i, 0)),
                  pl.BlockSpec((D_MODEL, tf), lambda i, f: (0, f)),
                  pl.BlockSpec((D_MODEL, tf), lambda i, f: (0, f)),
                  pl.BlockSpec((tf, D_MODEL), lambda i, f: (f, 0))],
        out_specs=pl.BlockSpec((tm, D_MODEL), lambda i, f: (i, 0)),
        out_shape=jax.ShapeDtypeStruct((T, D_MODEL), F32),
        scratch_shapes=[pltpu.VMEM((tm, D_MODEL), F32)],
        compiler_params=_params(("parallel", "arbitrary")),
        name="ffn_dense",
    )(h2, x1, w1, w3, w2)


def _router_kernel(h_ref, wr_ref, o_ref, cnt_ref, base_ref):
    tm = ROUTER_TM

    @pl.when(pl.program_id(0) == 0)
    def _():
        base_ref[...] = jnp.zeros_like(base_ref)

    lane = lax.broadcasted_iota(jnp.int32, (tm, LANES), 1)
    logits = jnp.where(lane < N_EXPERTS, _dot(h_ref[...], wr_ref[...]), -jnp.inf)
    m1 = jnp.max(logits, axis=-1, keepdims=True)
    i1 = jnp.min(jnp.where(logits == m1, lane, LANES), axis=-1, keepdims=True)
    rest = jnp.where(lane == i1, -jnp.inf, logits)
    m2 = jnp.max(rest, axis=-1, keepdims=True)
    i2 = jnp.min(jnp.where(rest == m2, lane, LANES), axis=-1, keepdims=True)
    e21 = jnp.exp(m2 - m1)
    g1 = 1.0 / (1.0 + e21)
    g2 = e21 / (1.0 + e21)

    onehot = jnp.where(lane == i1, 1.0, jnp.where(lane == i2, 1.0, 0.0))
    ti = lax.broadcasted_iota(jnp.int32, (tm, tm), 0)
    si = lax.broadcasted_iota(jnp.int32, (tm, tm), 1)
    before = jnp.where(si < ti, 1.0, 0.0).astype(BF16)
    pos = _dot(before, onehot.astype(BF16)) + base_ref[0:1, :]
    r1 = jnp.sum(jnp.where(lane == i1, pos, 0.0), axis=-1, keepdims=True)
    r2 = jnp.sum(jnp.where(lane == i2, pos, 0.0), axis=-1, keepdims=True)
    total = base_ref[0:1, :] + jnp.sum(onehot, axis=0, keepdims=True)
    base_ref[...] = jnp.broadcast_to(total, base_ref.shape)
    cnt_ref[...] = jnp.broadcast_to(total, cnt_ref.shape)

    out = jnp.where(lane == 0, i1.astype(F32), 0.0)
    out = jnp.where(lane == 1, i2.astype(F32), out)
    out = jnp.where(lane == 2, g1, out)
    out = jnp.where(lane == 3, g2, out)
    out = jnp.where(lane == 4, r1, out)
    out = jnp.where(lane == 5, r2, out)
    o_ref[...] = out


def _router(h2, wr):
    T = h2.shape[0]
    tm = ROUTER_TM
    return pl.pallas_call(
        _router_kernel,
        grid=(T // tm,),
        in_specs=[pl.BlockSpec((tm, D_MODEL), lambda i: (i, 0)),
                  pl.BlockSpec((D_MODEL, LANES), lambda i: (0, 0))],
        out_specs=[pl.BlockSpec((tm, LANES), lambda i: (i, 0)),
                   pl.BlockSpec((SUBLANES, LANES), lambda i: (0, 0))],
        out_shape=[jax.ShapeDtypeStruct((T, LANES), F32), jax.ShapeDtypeStruct((SUBLANES, LANES), F32)],
        scratch_shapes=[pltpu.VMEM((SUBLANES, LANES), F32)],
        compiler_params=_params(("arbitrary",)),
        name="moe_router",
    )(h2, wr)


def _expert_kernel(be_ref, nu_ref, x_ref, w1_ref, w3_ref, w2_ref, o_ref, acc_ref):
    i = pl.program_id(0)
    f = pl.program_id(1)

    @pl.when(f == 0)
    def _():
        acc_ref[...] = jnp.zeros_like(acc_ref)

    @pl.when(i < nu_ref[0])
    def _():
        x = x_ref[...]
        h1 = _dot(x, w1_ref[...])
        acc_ref[...] += _dot((h1 * _sigmoid(h1) * _dot(x, w3_ref[...])).astype(BF16), w2_ref[...])

    @pl.when(f == pl.num_programs(1) - 1)
    def _():
        o_ref[...] = acc_ref[...].astype(BF16)


def _experts(rows, block_expert, n_used, w1, w3, w2):
    n_rows = rows.shape[0]
    R, tf = MOE_ROWS, MOE_TF
    grid_spec = pltpu.PrefetchScalarGridSpec(
        num_scalar_prefetch=2,
        grid=(n_rows // R, FFN_EXPERT // tf),
        in_specs=[pl.BlockSpec((R, D_MODEL), lambda i, f, be, nu: (i, 0)),
                  pl.BlockSpec((None, D_MODEL, tf), lambda i, f, be, nu: (be[i], 0, f)),
                  pl.BlockSpec((None, D_MODEL, tf), lambda i, f, be, nu: (be[i], 0, f)),
                  pl.BlockSpec((None, tf, D_MODEL), lambda i, f, be, nu: (be[i], f, 0))],
        out_specs=pl.BlockSpec((R, D_MODEL), lambda i, f, be, nu: (i, 0)),
        scratch_shapes=[pltpu.VMEM((R, D_MODEL), F32)],
    )
    return pl.pallas_call(
        _expert_kernel,
        grid_spec=grid_spec,
        out_shape=jax.ShapeDtypeStruct((n_rows, D_MODEL), BF16),
        compiler_params=_params(("parallel", "arbitrary")),
        name="moe_experts",
    )(block_expert, n_used, rows, w1, w3, w2)


def _combine_kernel(x_ref, y1_ref, y2_ref, info_ref, o_ref):
    info = info_ref[...]
    o_ref[...] = (x_ref[...] + info[:, 2:3] * y1_ref[...].astype(F32)
                  + info[:, 3:4] * y2_ref[...].astype(F32))


def _combine(x1, y1, y2, info):
    T = x1.shape[0]
    tm = COMBINE_TM
    row = pl.BlockSpec((tm, D_MODEL), lambda i: (i, 0))
    return pl.pallas_call(
        _combine_kernel,
        grid=(T // tm,),
        in_specs=[row, row, row, pl.BlockSpec((tm, LANES), lambda i: (i, 0))],
        out_specs=row,
        out_shape=jax.ShapeDtypeStruct((T, D_MODEL), F32),
        compiler_params=_params(("parallel",)),
        name="moe_combine",
    )(x1, y1, y2, info)


def _moe(h2, x1, wr, w1, w3, w2):
    T = h2.shape[0]
    R = MOE_ROWS
    n_blocks = -(-(T * TOP_K) // R) + N_EXPERTS
    info, cnt = _router(h2, wr)
    expert = info[:, 0:2].astype(jnp.int32)
    rank = info[:, 4:6].astype(jnp.int32)
    counts = cnt[0, :N_EXPERTS].astype(jnp.int32)
    blocks_per = (counts + R - 1) // R
    bend = jnp.cumsum(blocks_per)
    pstart = (bend - blocks_per) * R
    dest = pstart[expert] + rank
    n_used = bend[-1:]
    block_expert = jnp.minimum(
        jnp.searchsorted(bend, jnp.arange(n_blocks, dtype=jnp.int32), side='right'),
        N_EXPERTS - 1).astype(jnp.int32)
    token = jnp.broadcast_to(jnp.arange(T, dtype=jnp.int32)[:, None], (T, TOP_K))
    tok_of_row = jnp.zeros((n_blocks * R,), jnp.int32).at[dest.reshape(-1)].set(token.reshape(-1))
    rows = h2[tok_of_row]
    y = _experts(rows, block_expert, n_used, w1, w3, w2)
    return _combine(x1, y[dest[:, 0]], y[dest[:, 1]], info)


def _pad_rows(rows, n):
    rows = jnp.stack(rows, axis=-2)
    pad = [(0, 0)] * (rows.ndim - 2) + [(0, n - rows.shape[-2]), (0, 0)]
    return jnp.pad(rows, pad)


def _hgrn_params(lb, norm_g):
    lb = lb.reshape(2, HG_HEADS, HG_DK)
    rows = []
    for d in range(2):
        rows += [jnp.log(jnp.maximum(lb[d], LB_EPS)), jnp.log1p(-lb[d]), 1.0 - lb[d]]
    rows.append(norm_g.reshape(HG_HEADS, HG_DK))
    return _pad_rows(rows, SUBLANES)


def _rg_params(conv_w, conv_b, b_a, b_x, lam):
    ng = RG_WIDTH // LANES
    g = lambda t: t.reshape(ng, LANES)
    rows = [g(conv_w[j]) for j in range(RG_CONV_W)] + [g(conv_b)]
    rows += [g(b_a[0]), g(b_x[0]), g(b_a[1]), g(b_x[1]), g(lam[0]), g(lam[1])]
    return _pad_rows(rows, 2 * SUBLANES)


def _rg_gate_weights(w_a, w_x):
    ng = RG_WIDTH // LANES
    per = RG_BLOCKS // ng

    def dense(w):
        w = w.reshape(ng, per, RG_BLOCK_W, RG_BLOCK_W)
        eye = jnp.eye(per, dtype=w.dtype)
        return jnp.einsum('gpcd,pq->gpcqd', w, eye).reshape(ng, LANES, LANES)

    return jnp.concatenate([dense(w_a[0]), dense(w_x[0]), dense(w_a[1]), dense(w_x[1])], axis=-1).astype(BF16)


def _attn_params(q_g, k_g):
    H = ATT_HEADS_PER_GROUP
    groups = []
    for gi, (_, dil) in enumerate(ATT_GROUPS):
        heads = jnp.arange(gi * H + 1, (gi + 1) * H + 1, dtype=F32)
        slopes = 2.0 ** (-8.0 * heads / ATT_HEADS) * dil
        rows = [jnp.broadcast_to(q_g * (ATT_DH ** -0.5), (H, LANES)), jnp.broadcast_to(k_g, (H, LANES)),
                jnp.broadcast_to(slopes[:, None], (H, LANES))]
        groups.append(_pad_rows(rows, SUBLANES))
    return jnp.stack(groups, axis=1)


def _trunk(x3, p):
    B, S, _ = x3.shape
    x = x3.reshape(B * S, D_MODEL)
    sm = jax.nn.softmax(p['hg_lb_logits'].astype(F32), axis=1)
    lower_bounds = jnp.cumsum(sm, axis=1) - sm[:, :1]
    for l in range(DEPTH):
        proj = _inproj(x, p['norm_mix_g'][l][None], p['w_in'][l].astype(BF16))
        hg = _hgrn(proj, _hgrn_params(lower_bounds[:, l], p['hg_norm_g'][l]), B, S)
        rg = _rglru(proj,
                    _rg_params(p['rg_conv_w'][l], p['rg_conv_b'][l], p['rg_b_a'][l], p['rg_b_x'][l],
                               p['rg_lambda'][l]),
                    _rg_gate_weights(p['rg_w_a'][l], p['rg_w_x'][l]), B, S)
        att = _attention(proj, _attn_params(p['attn_q_g'][l], p['attn_k_g'][l]), B, S)
        x1, h2 = _merge(x, proj, hg, rg, att, p['w_branch'][l].astype(BF16), p['w_out'][l].astype(BF16),
                        p['norm_ffn_g'][l][None])
        j = l // 2
        if l % 2 == 0:
            x = _ffn_dense(h2, x1, p['ffn_w1'][j].astype(BF16), p['ffn_w3'][j].astype(BF16),
                           p['ffn_w2'][j].astype(BF16))
        else:
            wr = jnp.pad(p['moe_router'][j], ((0, 0), (0, LANES - N_EXPERTS))).astype(BF16)
            x = _moe(h2, x1, wr, p['moe_w1'][j].astype(BF16), p['moe_w3'][j].astype(BF16),
                     p['moe_w2'][j].astype(BF16))
    return x.reshape(B, S, D_MODEL)


def kernel(x_prompt, x_sample, norm_mix_g, w_in, hg_lb_logits, hg_norm_g, rg_conv_w, rg_conv_b, rg_w_a, rg_b_a,
           rg_w_x, rg_b_x, rg_lambda, attn_q_g, attn_k_g, w_branch, w_out, norm_ffn_g, ffn_w1, ffn_w3, ffn_w2,
           moe_router, moe_w1, moe_w3, moe_w2):
    p = dict(norm_mix_g=norm_mix_g, w_in=w_in, hg_lb_logits=hg_lb_logits, hg_norm_g=hg_norm_g,
             rg_conv_w=rg_conv_w, rg_conv_b=rg_conv_b, rg_w_a=rg_w_a, rg_b_a=rg_b_a, rg_w_x=rg_w_x,
             rg_b_x=rg_b_x, rg_lambda=rg_lambda, attn_q_g=attn_q_g, attn_k_g=attn_k_g,
             w_branch=w_branch, w_out=w_out, norm_ffn_g=norm_ffn_g, ffn_w1=ffn_w1, ffn_w3=ffn_w3,
             ffn_w2=ffn_w2, moe_router=moe_router, moe_w1=moe_w1, moe_w3=moe_w3, moe_w2=moe_w2)
    return (_trunk(x_prompt, p), _trunk(x_sample, p))
```

```python
import functools
import math

import jax
import jax.numpy as jnp
from jax import lax
from jax.experimental import pallas as pl
from jax.experimental.pallas import tpu as pltpu

F32 = jnp.float32
BF16 = jnp.bfloat16

LANES = 128
SUBLANES = 8
VMEM_LIMIT_BYTES = 56 * 1024 * 1024

D_MODEL = 1024
DEPTH = 2
BRANCH_W = 512
HG_HEADS = 4
HG_DK = 128
HG_CHUNK = 64
HG_LEVELS = 6
LB_EPS = 1e-20
RG_WIDTH = 512
RG_BLOCKS = 8
RG_BLOCK_W = RG_WIDTH // RG_BLOCKS
RG_CONV_W = 4
RG_C = 8.0
RG_TILE = 256
RG_HALO = 8
ATT_GROUPS = ((128, 1), (512, 4), (2048, 16))
ATT_HEADS_PER_GROUP = 4
ATT_HEADS = ATT_HEADS_PER_GROUP * len(ATT_GROUPS)
ATT_DH = 128
ATT_RADIUS = 64
ATT_QBLK = 128
ATT_UNROLL = 8
ATT_TILE = 256
FFN_DENSE = 2816
N_EXPERTS = 8
TOP_K = 2
FFN_EXPERT = 3584
NORM_EPS = 1e-6
NEG_INF = -1e30

CB_HQ, CB_HF, CB_HB, CB_HV, CB_HG = 0, 4, 8, 12, 16
CB_RX, CB_RGATE = 20, 24
CB_AQ, CB_AK, CB_AV = 28, 40, 52
CB_GATES = 64
IN_COLS = 11264
N_CB = IN_COLS // LANES

INPROJ_TM, INPROJ_TN = 1024, 1024
MERGE_TM = 256
FFN_TM, FFN_TF = 512, 1408
MOE_ROWS, MOE_TF = 512, 512
ROUTER_TM = 512
COMBINE_TM = 512


def _params(sem):
    return pltpu.CompilerParams(dimension_semantics=sem, vmem_limit_bytes=VMEM_LIMIT_BYTES)


def _dot(a, b):
    return jnp.dot(a, b, preferred_element_type=F32)


def _dot_nt(a, b):
    return lax.dot_general(a, b, (((1,), (1,)), ((), ())), preferred_element_type=F32)


def _sigmoid(x):
    return jax.nn.sigmoid(x)


def _lockstep(gens):
    results = [None] * len(gens)
    active = list(range(len(gens)))
    while active:
        for idx in list(active):
            try:
                next(gens[idx])
            except StopIteration as stop:
                results[idx] = stop.value
                active.remove(idx)
    return results


def _inproj_kernel(x_ref, g_ref, w_ref, o_ref, h_ref):
    @pl.when(pl.program_id(1) == 0)
    def _():
        x = x_ref[...]
        ms = jnp.mean(x * x, axis=-1, keepdims=True)
        h_ref[...] = (x * lax.rsqrt(ms + NORM_EPS) * g_ref[...]).astype(BF16)

    acc = _dot(h_ref[...], w_ref[...])
    for c in range(INPROJ_TN // LANES):
        o_ref[c] = acc[:, c * LANES:(c + 1) * LANES].astype(BF16)


def _inproj(x, g, w):
    T = x.shape[0]
    return pl.pallas_call(
        _inproj_kernel,
        grid=(T // INPROJ_TM, IN_COLS // INPROJ_TN),
        in_specs=[
            pl.BlockSpec((INPROJ_TM, D_MODEL), lambda i, j: (i, 0)),
            pl.BlockSpec((1, D_MODEL), lambda i, j: (0, 0)),
            pl.BlockSpec((D_MODEL, INPROJ_TN), lambda i, j: (0, j)),
        ],
        out_specs=pl.BlockSpec((INPROJ_TN // LANES, INPROJ_TM, LANES), lambda i, j: (j, i, 0)),
        out_shape=jax.ShapeDtypeStruct((N_CB, T, LANES), BF16),
        scratch_shapes=[pltpu.VMEM((INPROJ_TM, D_MODEL), BF16)],
        compiler_params=_params(("parallel", "arbitrary")),
        name="inproj",
    )(x, g, w)


def _hgrn_chain(q, vb, z, c0, c1, oml, code, states, reverse):
    C = HG_CHUNK
    nv = C // SUBLANES
    e = jnp.exp(-jnp.abs(z))
    log_sig = jnp.minimum(z, 0.0) - jnp.log(1.0 + e)
    t = c1 + log_sig
    log_f = jnp.maximum(c0, t) + jnp.log(1.0 + jnp.exp(-jnp.abs(c0 - t)))
    k = oml * (jnp.where(z >= 0.0, e, 1.0) / (1.0 + e))
    sc_diag = _dot_nt(q.astype(BF16), k.astype(BF16))
    vt = vb.astype(F32).T.astype(BF16)
    yield
    x3 = log_f.reshape(nv, SUBLANES, LANES)
    sub3 = lax.broadcasted_iota(jnp.int32, (nv, SUBLANES, LANES), 1)
    d = 1
    while d < SUBLANES:
        if reverse:
            x3 = x3 + jnp.where(sub3 < SUBLANES - d, pltpu.roll(x3, SUBLANES - d, 1), 0.0)
        else:
            x3 = x3 + jnp.where(sub3 >= d, pltpu.roll(x3, d, 1), 0.0)
        d *= 2
    edge = 0 if reverse else SUBLANES - 1
    tot = jnp.broadcast_to(x3[:, edge:edge + 1, :], (nv, SUBLANES, LANES))
    groups = [None] * nv
    run = None
    for g in (range(nv - 1, -1, -1) if reverse else range(nv)):
        groups[g] = x3[g] if run is None else x3[g] + run
        run = tot[g] if run is None else run + tot[g]
    a3 = jnp.stack(groups)
    a = a3.reshape(C, LANES)
    sub = lax.broadcasted_iota(jnp.int32, (C, LANES), 0) & (SUBLANES - 1)
    yield
    q_in = (q * jnp.exp(a)).astype(BF16)
    a_last = a[0:1, :] if reverse else a[C - 1:C, :]
    k_end = (k * jnp.exp(a_last - a)).astype(BF16)
    st = states[int(reverse)]
    o_inter = _dot_nt(q_in, st.astype(BF16))
    states[int(reverse)] = st * jnp.exp(a_last) + _dot(vt, k_end)
    yield

    def pick(r):
        return jnp.broadcast_to(a3[:, r:r + 1, :], (nv, SUBLANES, LANES)).reshape(C, LANES)

    level_scores = []
    for b in range(HG_LEVELS):
        h = 1 << b
        m = h if reverse else h - 1
        if h == 1:
            if reverse:
                ref = jnp.where((sub & 1) == 0, pltpu.roll(a, C - 1, 0), a)
            else:
                ref = jnp.where((sub & 1) == 1, pltpu.roll(a, 1, 0), a)
        elif 2 * h < SUBLANES:
            ref = jnp.where(sub < 2 * h, pick(m), pick(m + 2 * h))
        elif 2 * h == SUBLANES:
            ref = pick(m)
        else:
            pieces = [jnp.broadcast_to(a[blk * 2 * h + m:blk * 2 * h + m + 1, :], (2 * h, LANES))
                      for blk in range(C // (2 * h))]
            ref = jnp.concatenate(pieces, axis=0) if len(pieces) > 1 else pieces[0]
        eb = jnp.exp(-jnp.abs(a - ref))
        level_scores.append(_dot_nt((q * eb).astype(BF16), (k * eb).astype(BF16)))
        yield
    scores = jnp.where(code == HG_LEVELS, sc_diag, 0.0)
    for b in range(HG_LEVELS):
        scores = jnp.where(code == b, level_scores[b], scores)
    o = o_inter + _dot(scores.astype(BF16), vb)
    yield
    return o


def _hgrn_kernel(q_ref, zf_ref, zb_ref, v_ref, g_ref, par_ref, o_ref, acc_ref, code_ref, *, S):
    C = HG_CHUNK
    n = S // C
    par = par_ref[...]
    c0f, c1f, omlf = par[0:1], par[1:2], par[2:3]
    c0b, c1b, omlb = par[3:4], par[4:5], par[5:6]
    norm_g = par[6:7]

    ti = lax.broadcasted_iota(jnp.int32, (C, C), 0)
    si = lax.broadcasted_iota(jnp.int32, (C, C), 1)
    x = ti ^ si
    lvl = jnp.zeros((C, C), jnp.int32)
    for b in range(1, HG_LEVELS):
        lvl = lvl + jnp.where(x >= (1 << b), 1, 0)
    diag = jnp.where(ti == si, HG_LEVELS, -1)
    code_ref[0] = jnp.where(ti > si, lvl, diag)
    code_ref[1] = jnp.where(ti < si, lvl, diag)

    def chain(ci, z_ref, c0, c1, oml, d, states):
        r0 = pl.multiple_of(ci * C, C)
        return _hgrn_chain(q_ref[pl.ds(r0, C), :].astype(F32), v_ref[pl.ds(r0, C), :],
                           z_ref[pl.ds(r0, C), :].astype(F32), c0, c1, oml, code_ref[d], states, d == 1)

    def make_body(accumulate):
        def body(i, carry):
            chunks = (2 * i, 2 * i + 1, n - 1 - 2 * i, n - 2 - 2 * i)
            states = list(carry)
            outs = _lockstep([chain(chunks[0], zf_ref, c0f, c1f, omlf, 0, states),
                              chain(chunks[1], zf_ref, c0f, c1f, omlf, 0, states),
                              chain(chunks[2], zb_ref, c0b, c1b, omlb, 1, states),
                              chain(chunks[3], zb_ref, c0b, c1b, omlb, 1, states)])
            for ci, o in zip(chunks, outs):
                r0 = pl.multiple_of(ci * C, C)
                if accumulate:
                    acc_ref[pl.ds(r0, C), :] += o
                else:
                    acc_ref[pl.ds(r0, C), :] = o
            return tuple(states)
        return body

    zero = jnp.zeros((LANES, HG_DK), F32)
    carry = lax.fori_loop(0, n // 4, make_body(False), (zero, zero))
    lax.fori_loop(n // 4, n // 2, make_body(True), carry)

    ft = 256

    def fin(i, _):
        r0 = pl.multiple_of(i * ft, ft)
        o = acc_ref[pl.ds(r0, ft), :]
        y = o * lax.rsqrt(jnp.mean(o * o, axis=-1, keepdims=True) + NORM_EPS) * norm_g
        g = g_ref[pl.ds(r0, ft), :].astype(F32)
        o_ref[pl.ds(r0, ft), :] = (y * (g * _sigmoid(g))).astype(BF16)
        return 0

    lax.fori_loop(0, S // ft, fin, 0)


def _hgrn(proj, par, B, S):
    def spec(cb0):
        return pl.BlockSpec((None, S, LANES), lambda b, h: (cb0 + h, b, 0))

    return pl.pallas_call(
        functools.partial(_hgrn_kernel, S=S),
        grid=(B, HG_HEADS),
        in_specs=[spec(CB_HQ), spec(CB_HF), spec(CB_HB), spec(CB_HV), spec(CB_HG),
                  pl.BlockSpec((None, SUBLANES, LANES), lambda b, h: (h, 0, 0))],
        out_specs=pl.BlockSpec((None, S, LANES), lambda b, h: (h, b, 0)),
        out_shape=jax.ShapeDtypeStruct((HG_HEADS, B * S, LANES), BF16),
        scratch_shapes=[pltpu.VMEM((S, LANES), F32), pltpu.VMEM((2, HG_CHUNK, HG_CHUNK), jnp.int32)],
        compiler_params=_params(("parallel", "parallel")),
        name="hgrn2",
    )(proj, proj, proj, proj, proj, par)


def _linear_scan_tile(a, u, carry, reverse):
    n = a.shape[0]
    nv = n // SUBLANES
    a3 = a.reshape(nv, SUBLANES, LANES)
    u3 = u.reshape(nv, SUBLANES, LANES)
    sub = lax.broadcasted_iota(jnp.int32, (nv, SUBLANES, LANES), 1)
    d = 1
    while d < SUBLANES:
        if reverse:
            keep = sub < SUBLANES - d
            shift = SUBLANES - d
        else:
            keep = sub >= d
            shift = d
        a_s = jnp.where(keep, pltpu.roll(a3, shift, 1), 1.0)
        u_s = jnp.where(keep, pltpu.roll(u3, shift, 1), 0.0)
        u3 = a3 * u_s + u3
        a3 = a3 * a_s
        d *= 2
        yield
    edge = 0 if reverse else SUBLANES - 1
    a_tot = jnp.broadcast_to(a3[:, edge:edge + 1, :], a3.shape)
    u_tot = jnp.broadcast_to(u3[:, edge:edge + 1, :], u3.shape)
    groups = [None] * nv
    for g in (range(nv - 1, -1, -1) if reverse else range(nv)):
        groups[g] = u3[g] + a3[g] * carry
        carry = u_tot[g] + a_tot[g] * carry
        if g % 4 == 0:
            yield
    return jnp.stack(groups).reshape(n, LANES), carry


def _rg_kernel(x_ref, gate_ref, par_ref, w_ref, o_ref, xpad_ref, xc_ref, h_ref, *, S):
    TS = RG_TILE
    n = S // TS
    par = par_ref[...]
    conv_b = par[4:5]

    xpad_ref[pl.ds(0, RG_HALO), :] = jnp.zeros((RG_HALO, LANES), F32)
    xpad_ref[pl.ds(S + RG_HALO, RG_HALO), :] = jnp.zeros((RG_HALO, LANES), F32)

    def copy(i, _):
        r0 = pl.multiple_of(i * TS, TS)
        xpad_ref[pl.ds(r0 + RG_HALO, TS), :] = x_ref[pl.ds(r0, TS), :].astype(F32)
        return 0

    lax.fori_loop(0, n, copy, 0)

    def conv(i, _):
        r0 = pl.multiple_of(i * TS, TS)
        xc = conv_b
        for j in range(RG_CONV_W):
            off = RG_HALO - RG_CONV_W // 2 + j
            xc = xc + par[j:j + 1] * xpad_ref[pl.ds(r0 + off, TS), :]
        xc_ref[pl.ds(r0, TS), :] = xc
        return 0

    lax.fori_loop(0, n, conv, 0)

    def tile(i, carries, d, final):
        r0 = pl.multiple_of(i * TS, TS)
        xc = xc_ref[pl.ds(r0, TS), :]
        gts = _dot(xc.astype(BF16), w_ref[:, d * 2 * LANES:(d + 1) * 2 * LANES])
        yield
        r = _sigmoid(gts[:, :LANES] + par[5 + 2 * d:6 + 2 * d])
        ig = _sigmoid(gts[:, LANES:] + par[6 + 2 * d:7 + 2 * d])
        lam = par[9 + d:10 + d]
        softplus_neg_lam = jnp.maximum(-lam, 0.0) + jnp.log1p(jnp.exp(-jnp.abs(lam)))
        log_a = (-RG_C) * r * softplus_neg_lam
        a = jnp.exp(log_a)
        y = 1.0 - a * a
        u = jnp.where(y > 0.0, y * lax.rsqrt(y), 0.0) * (ig * xc)
        yield
        h, carries[d] = yield from _linear_scan_tile(a, u, carries[d], reverse=(d == 1))
        yield
        if final:
            gate = gate_ref[pl.ds(r0, TS), :].astype(F32)
            o_ref[pl.ds(r0, TS), :] = (jax.nn.gelu(gate) * (h_ref[pl.ds(r0, TS), :] + h)).astype(BF16)
        else:
            h_ref[pl.ds(r0, TS), :] = h

    def make_body(final):
        def body(i, carry):
            carries = list(carry)
            _lockstep([tile(i, carries, 0, final), tile(n - 1 - i, carries, 1, final)])
            return tuple(carries)
        return body

    zero = jnp.zeros((SUBLANES, LANES), F32)
    carry = lax.fori_loop(0, n // 2, make_body(False), (zero, zero))
    lax.fori_loop(n // 2, n, make_body(True), carry)


def _rglru(proj, par, w4, B, S):
    ng = RG_WIDTH // LANES

    def spec(cb0):
        return pl.BlockSpec((None, S, LANES), lambda b, g: (cb0 + g, b, 0))

    return pl.pallas_call(
        functools.partial(_rg_kernel, S=S),
        grid=(B, ng),
        in_specs=[spec(CB_RX), spec(CB_RGATE),
                  pl.BlockSpec((None, 2 * SUBLANES, LANES), lambda b, g: (g, 0, 0)),
                  pl.BlockSpec((None, LANES, 4 * LANES), lambda b, g: (g, 0, 0))],
        out_specs=pl.BlockSpec((None, S, LANES), lambda b, g: (g, b, 0)),
        out_shape=jax.ShapeDtypeStruct((ng, B * S, LANES), BF16),
        scratch_shapes=[pltpu.VMEM((S + 2 * RG_HALO, LANES), F32), pltpu.VMEM((S, LANES), F32),
                        pltpu.VMEM((S, LANES), F32)],
        compiler_params=_params(("parallel", "parallel")),
        name="rglru",
    )(proj, proj, par, w4)


def _attn_kernel(q0_ref, k0_ref, v0_ref, q1_ref, k1_ref, v1_ref, q2_ref, k2_ref, v2_ref, par_ref, o_ref,
                 perm_ref, qd_ref, kd_ref, vd_ref, bias_ref, og_ref, lg_ref, *, S):
    refs = ((q0_ref, k0_ref, v0_ref), (q1_ref, k1_ref, v1_ref), (q2_ref, k2_ref, v2_ref))
    ct = ATT_TILE

    for g, (_, dil) in enumerate(ATT_GROUPS):
        q_ref, k_ref, v_ref = refs[g]
        par = par_ref[g]
        qg, kg, slope = par[0:1], par[1:2], par[2:3, 0:1]
        L = S // dil
        Q = min(ATT_QBLK, L)
        KW = min(Q + 2 * ATT_RADIUS, L)
        nq = L // Q
        nq_shift = int(math.log2(nq))

        for var in range(3):
            qi = var * ATT_RADIUS + lax.broadcasted_iota(jnp.int32, (Q, KW), 0)
            rel = jnp.abs(qi - lax.broadcasted_iota(jnp.int32, (Q, KW), 1))
            bias_ref[var, :Q, :KW] = jnp.where(rel <= ATT_RADIUS, -slope * rel.astype(F32), NEG_INF)

        per = ct // dil
        if dil > 1:
            pi = lax.broadcasted_iota(jnp.int32, (ct, ct), 0)
            pj = lax.broadcasted_iota(jnp.int32, (ct, ct), 1)
            src_tok = (pi & (per - 1)) * dil + lax.shift_right_logical(pi, int(math.log2(per)))
            perm_ref[...] = jnp.where(pj == src_tok, 1.0, 0.0).astype(BF16)

        def prep_tile(i, q_ref=q_ref, k_ref=k_ref, v_ref=v_ref, qg=qg, kg=kg, dil=dil, per=per, L=L):
            r0 = pl.multiple_of(i * ct, ct)
            xs = (q_ref[pl.ds(r0, ct), :].astype(F32), k_ref[pl.ds(r0, ct), :].astype(F32))
            sums = []
            for x in xs:
                x2 = x * x
                hi = x2.astype(BF16)
                lo = (x2 - hi.astype(F32)).astype(BF16)
                sums.append(_dot(jnp.concatenate([hi, lo], axis=1), jnp.ones((2 * LANES, LANES), BF16)))
            yield
            tiles = [(x * lax.rsqrt(ss * (1.0 / LANES) + NORM_EPS) * gain).astype(BF16)
                     for x, ss, gain in zip(xs, sums, (qg, kg))]
            tiles.append(v_ref[pl.ds(r0, ct), :])
            if dil > 1:
                tiles = [_dot(perm_ref[...], x) for x in tiles]
                yield
            for dst_ref, y in zip((qd_ref, kd_ref, vd_ref), tiles):
                if dil == 1:
                    dst_ref[pl.ds(r0, ct), :] = y
                else:
                    y = y.astype(BF16)
                    for r in range(dil):
                        dst_ref[pl.ds(pl.multiple_of(r * L + i * per, 16), per), :] = y[r * per:(r + 1) * per, :]

        prep_unroll = min(4, S // ct)

        def prep(i, _, prep_tile=prep_tile, prep_unroll=prep_unroll):
            _lockstep([prep_tile(i * prep_unroll + u) for u in range(prep_unroll)])
            return 0

        lax.fori_loop(0, S // ct // prep_unroll, prep, 0)

        def rows(start, size, dil=dil):
            if dil == 1:
                return pl.ds(start, size)
            return pl.ds(start, size, stride=dil)

        unroll = min(ATT_UNROLL, dil * nq)

        def qblock(it, g=g, dil=dil, L=L, Q=Q, KW=KW, nq=nq, nq_shift=nq_shift, rows=rows):
            r = lax.shift_right_logical(it, nq_shift)
            m0 = (it & (nq - 1)) * Q
            ks = jnp.clip(m0 - ATT_RADIUS, 0, L - KW)
            var = lax.shift_right_logical(m0 - ks, int(math.log2(ATT_RADIUS)))
            base = r * L
            qn = qd_ref[pl.ds(pl.multiple_of(base + m0, 16), Q), :]
            kk = kd_ref[pl.ds(pl.multiple_of(base + ks, 16), KW), :]
            s = _dot_nt(qn, kk)
            yield
            s = s + bias_ref[var, :Q, :KW]
            m = jnp.max(s, axis=-1, keepdims=True)
            yield
            pb = jnp.exp(s - m).astype(BF16)
            vv = vd_ref[pl.ds(pl.multiple_of(base + ks, 16), KW), :]
            l = _dot(pb, jnp.ones((KW, LANES), BF16))
            o = _dot(pb, vv)
            yield
            og_ref[g, rows(m0 * dil + r, Q), :] = o / l
            lg_ref[g, rows(m0 * dil + r, Q), :] = m + jnp.log(l)

        def qblocks(i, _, unroll=unroll, qblock=qblock):
            _lockstep([qblock(i * unroll + u) for u in range(unroll)])
            return 0

        lax.fori_loop(0, (dil * nq) // unroll, qblocks, 0)

    def merge(i, _):
        r0 = pl.multiple_of(i * ct, ct)
        l0, l1, l2 = (lg_ref[g, pl.ds(r0, ct), :] for g in range(3))
        m = jnp.maximum(jnp.maximum(l0, l1), l2)
        e0, e1, e2 = jnp.exp(l0 - m), jnp.exp(l1 - m), jnp.exp(l2 - m)
        o = (e0 * og_ref[0, pl.ds(r0, ct), :] + e1 * og_ref[1, pl.ds(r0, ct), :]
             + e2 * og_ref[2, pl.ds(r0, ct), :]) / (e0 + e1 + e2)
        o_ref[pl.ds(r0, ct), :] = o.astype(BF16)
        return 0

    lax.fori_loop(0, S // ct, merge, 0)


def _attention(proj, par, B, S):
    H = ATT_HEADS_PER_GROUP
    ngroups = len(ATT_GROUPS)

    def spec(cb0, g):
        return pl.BlockSpec((None, S, LANES), lambda b, j: (cb0 + g * H + j, b, 0))

    in_specs = []
    for g in range(ngroups):
        in_specs += [spec(CB_AQ, g), spec(CB_AK, g), spec(CB_AV, g)]
    in_specs.append(pl.BlockSpec((None, ngroups, SUBLANES, LANES), lambda b, j: (j, 0, 0, 0)))
    return pl.pallas_call(
        functools.partial(_attn_kernel, S=S),
        grid=(B, H),
        in_specs=in_specs,
        out_specs=pl.BlockSpec((None, S, LANES), lambda b, j: (j, b, 0)),
        out_shape=jax.ShapeDtypeStruct((H, B * S, LANES), BF16),
        scratch_shapes=[pltpu.VMEM((ATT_TILE, ATT_TILE), BF16),
                        pltpu.VMEM((S, LANES), BF16), pltpu.VMEM((S, LANES), BF16), pltpu.VMEM((S, LANES), BF16),
                        pltpu.VMEM((3, ATT_QBLK, ATT_QBLK + 2 * ATT_RADIUS), F32),
                        pltpu.VMEM((ngroups, S, LANES), F32), pltpu.VMEM((ngroups, S, LANES), F32)],
        compiler_params=_params(("parallel", "parallel")),
        name="attention",
    )(*([proj] * (3 * ngroups)), par)


def _cat_heads(ref):
    return jnp.concatenate([ref[c] for c in range(ref.shape[0])], axis=-1)


def _merge_kernel(x_ref, hg_ref, rg_ref, att_ref, ga_ref, gb_ref, gc_ref, wb_ref, wo_ref, ng_ref,
                  x1_ref, h2_ref):
    merged = _sigmoid(_cat_heads(ga_ref).astype(F32)) * _dot(_cat_heads(hg_ref), wb_ref[0])
    merged += _sigmoid(_cat_heads(gb_ref).astype(F32)) * _dot(_cat_heads(rg_ref), wb_ref[1])
    merged += _sigmoid(_cat_heads(gc_ref).astype(F32)) * _dot(_cat_heads(att_ref), wb_ref[2])
    x1 = x_ref[...] + _dot(merged.astype(BF16), wo_ref[...])
    x1_ref[...] = x1
    ms = jnp.mean(x1 * x1, axis=-1, keepdims=True)
    h2_ref[...] = (x1 * lax.rsqrt(ms + NORM_EPS) * ng_ref[...]).astype(BF16)


def _merge(x, proj, hg, rg, att, wb, wo, ng):
    T = x.shape[0]
    tm = MERGE_TM
    nb = BRANCH_W // LANES
    ngate = D_MODEL // LANES

    branch = pl.BlockSpec((nb, tm, LANES), lambda i: (0, i, 0))

    def gate(n):
        return pl.BlockSpec((ngate, tm, LANES), lambda i: (CB_GATES // ngate + n, i, 0))

    row = pl.BlockSpec((tm, D_MODEL), lambda i: (i, 0))
    return pl.pallas_call(
        _merge_kernel,
        grid=(T // tm,),
        in_specs=[row, branch, branch, branch, gate(0), gate(1), gate(2),
                  pl.BlockSpec((3, BRANCH_W, D_MODEL), lambda i: (0, 0, 0)),
                  pl.BlockSpec((D_MODEL, D_MODEL), lambda i: (0, 0)),
                  pl.BlockSpec((1, D_MODEL), lambda i: (0, 0))],
        out_specs=[row, row],
        out_shape=[jax.ShapeDtypeStruct((T, D_MODEL), F32), jax.ShapeDtypeStruct((T, D_MODEL), BF16)],
        compiler_params=_params(("parallel",)),
        name="merge",
    )(x, hg, rg, att, proj, proj, proj, wb, wo, ng)


def _ffn_kernel(h_ref, x_ref, w1_ref, w3_ref, w2_ref, o_ref, acc_ref):
    f = pl.program_id(1)
    h = h_ref[...]
    h1 = _dot(h, w1_ref[...])
    y = _dot((h1 * _sigmoid(h1) * _dot(h, w3_ref[...])).astype(BF16), w2_ref[...])

    @pl.when(f == 0)
    def _():
        acc_ref[...] = x_ref[...] + y

    @pl.when(f > 0)
    def _():
        acc_ref[...] += y

    @pl.when(f == pl.num_programs(1) - 1)
    def _():
        o_ref[...] = acc_ref[...]


def _ffn_dense(h2, x1, w1, w3, w2):
    T = h2.shape[0]
    tm, tf = FFN_TM, FFN_TF
    return pl.pallas_call(
        _ffn_kernel,
        grid=(T // tm, FFN_DENSE // tf),
        in_specs=[pl.BlockSpec((tm, D_MODEL), lambda i, f: (i, 0)),
                  pl.BlockSpec((tm, D_MODEL), lambda i, f: (i, 0)),
                  pl.BlockSpec((D_MODEL, tf), lambda i, f: (0, f)),
                  pl.BlockSpec((D_MODEL, tf), lambda i, f: (0, f)),
                  pl.BlockSpec((tf, D_MODEL), lambda i, f: (f, 0))],
        out_specs=pl.BlockSpec((tm, D_MODEL), lambda i, f: (i, 0)),
        out_shape=jax.ShapeDtypeStruct((T, D_MODEL), F32),
        scratch_shapes=[pltpu.VMEM((tm, D_MODEL), F32)],
        compiler_params=_params(("parallel", "arbitrary")),
        name="ffn_dense",
    )(h2, x1, w1, w3, w2)


def _router_kernel(h_ref, wr_ref, o_ref, cnt_ref, base_ref):
    tm = ROUTER_TM

    @pl.when(pl.program_id(0) == 0)
    def _():
        base_ref[...] = jnp.zeros_like(base_ref)

    lane = lax.broadcasted_iota(jnp.int32, (tm, LANES), 1)
    logits = jnp.where(lane < N_EXPERTS, _dot(h_ref[...], wr_ref[...]), -jnp.inf)
    m1 = jnp.max(logits, axis=-1, keepdims=True)
    i1 = jnp.min(jnp.where(logits == m1, lane, LANES), axis=-1, keepdims=True)
    rest = jnp.where(lane == i1, -jnp.inf, logits)
    m2 = jnp.max(rest, axis=-1, keepdims=True)
    i2 = jnp.min(jnp.where(rest == m2, lane, LANES), axis=-1, keepdims=True)
    e21 = jnp.exp(m2 - m1)
    g1 = 1.0 / (1.0 + e21)
    g2 = e21 / (1.0 + e21)

    onehot = jnp.where(lane == i1, 1.0, jnp.where(lane == i2, 1.0, 0.0))
    ti = lax.broadcasted_iota(jnp.int32, (tm, tm), 0)
    si = lax.broadcasted_iota(jnp.int32, (tm, tm), 1)
    before = jnp.where(si < ti, 1.0, 0.0).astype(BF16)
    pos = _dot(before, onehot.astype(BF16)) + base_ref[0:1, :]
    r1 = jnp.sum(jnp.where(lane == i1, pos, 0.0), axis=-1, keepdims=True)
    r2 = jnp.sum(jnp.where(lane == i2, pos, 0.0), axis=-1, keepdims=True)
    total = base_ref[0:1, :] + jnp.sum(onehot, axis=0, keepdims=True)
    base_ref[...] = jnp.broadcast_to(total, base_ref.shape)
    cnt_ref[...] = jnp.broadcast_to(total, cnt_ref.shape)

    out = jnp.where(lane == 0, i1.astype(F32), 0.0)
    out = jnp.where(lane == 1, i2.astype(F32), out)
    out = jnp.where(lane == 2, g1, out)
    out = jnp.where(lane == 3, g2, out)
    out = jnp.where(lane == 4, r1, out)
    out = jnp.where(lane == 5, r2, out)
    o_ref[...] = out


def _router(h2, wr):
    T = h2.shape[0]
    tm = ROUTER_TM
    return pl.pallas_call(
        _router_kernel,
        grid=(T // tm,),
        in_specs=[pl.BlockSpec((tm, D_MODEL), lambda i: (i, 0)),
                  pl.BlockSpec((D_MODEL, LANES), lambda i: (0, 0))],
        out_specs=[pl.BlockSpec((tm, LANES), lambda i: (i, 0)),
                   pl.BlockSpec((SUBLANES, LANES), lambda i: (0, 0))],
        out_shape=[jax.ShapeDtypeStruct((T, LANES), F32), jax.ShapeDtypeStruct((SUBLANES, LANES), F32)],
        scratch_shapes=[pltpu.VMEM((SUBLANES, LANES), F32)],
        compiler_params=_params(("arbitrary",)),
        name="moe_router",
    )(h2, wr)


def _expert_kernel(be_ref, nu_ref, x_ref, w1_ref, w3_ref, w2_ref, o_ref, acc_ref):
    i = pl.program_id(0)
    f = pl.program_id(1)

    @pl.when(f == 0)
    def _():
        acc_ref[...] = jnp.zeros_like(acc_ref)

    @pl.when(i < nu_ref[0])
    def _():
        x = x_ref[...]
        h1 = _dot(x, w1_ref[...])
        acc_ref[...] += _dot((h1 * _sigmoid(h1) * _dot(x, w3_ref[...])).astype(BF16), w2_ref[...])

    @pl.when(f == pl.num_programs(1) - 1)
    def _():
        o_ref[...] = acc_ref[...].astype(BF16)


def _experts(rows, block_expert, n_used, w1, w3, w2):
    n_rows = rows.shape[0]
    R, tf = MOE_ROWS, MOE_TF
    grid_spec = pltpu.PrefetchScalarGridSpec(
        num_scalar_prefetch=2,
        grid=(n_rows // R, FFN_EXPERT // tf),
        in_specs=[pl.BlockSpec((R, D_MODEL), lambda i, f, be, nu: (i, 0)),
                  pl.BlockSpec((None, D_MODEL, tf), lambda i, f, be, nu: (be[i], 0, f)),
                  pl.BlockSpec((None, D_MODEL, tf), lambda i, f, be, nu: (be[i], 0, f)),
                  pl.BlockSpec((None, tf, D_MODEL), lambda i, f, be, nu: (be[i], f, 0))],
        out_specs=pl.BlockSpec((R, D_MODEL), lambda i, f, be, nu: (i, 0)),
        scratch_shapes=[pltpu.VMEM((R, D_MODEL), F32)],
    )
    return pl.pallas_call(
        _expert_kernel,
        grid_spec=grid_spec,
        out_shape=jax.ShapeDtypeStruct((n_rows, D_MODEL), BF16),
        compiler_params=_params(("parallel", "arbitrary")),
        name="moe_experts",
    )(block_expert, n_used, rows, w1, w3, w2)


def _combine_kernel(x_ref, y1_ref, y2_ref, info_ref, o_ref):
    info = info_ref[...]
    o_ref[...] = (x_ref[...] + info[:, 2:3] * y1_ref[...].astype(F32)
                  + info[:, 3:4] * y2_ref[...].astype(F32))


def _combine(x1, y1, y2, info):
    T = x1.shape[0]
    tm = COMBINE_TM
    row = pl.BlockSpec((tm, D_MODEL), lambda i: (i, 0))
    return pl.pallas_call(
        _combine_kernel,
        grid=(T // tm,),
        in_specs=[row, row, row, pl.BlockSpec((tm, LANES), lambda i: (i, 0))],
        out_specs=row,
        out_shape=jax.ShapeDtypeStruct((T, D_MODEL), F32),
        compiler_params=_params(("parallel",)),
        name="moe_combine",
    )(x1, y1, y2, info)


def _moe(h2, x1, wr, w1, w3, w2):
    T = h2.shape[0]
    R = MOE_ROWS
    n_blocks = -(-(T * TOP_K) // R) + N_EXPERTS
    info, cnt = _router(h2, wr)
    expert = info[:, 0:2].astype(jnp.int32)
    rank = info[:, 4:6].astype(jnp.int32)
    counts = cnt[0, :N_EXPERTS].astype(jnp.int32)
    blocks_per = (counts + R - 1) // R
    bend = jnp.cumsum(blocks_per)
    pstart = (bend - blocks_per) * R
    dest = pstart[expert] + rank
    n_used = bend[-1:]
    block_expert = jnp.minimum(
        jnp.searchsorted(bend, jnp.arange(n_blocks, dtype=jnp.int32), side='right'),
        N_EXPERTS - 1).astype(jnp.int32)
    token = jnp.broadcast_to(jnp.arange(T, dtype=jnp.int32)[:, None], (T, TOP_K))
    tok_of_row = jnp.zeros((n_blocks * R,), jnp.int32).at[dest.reshape(-1)].set(token.reshape(-1))
    rows = h2[tok_of_row]
    y = _experts(rows, block_expert, n_used, w1, w3, w2)
    return _combine(x1, y[dest[:, 0]], y[dest[:, 1]], info)


def _pad_rows(rows, n):
    rows = jnp.stack(rows, axis=-2)
    pad = [(0, 0)] * (rows.ndim - 2) + [(0, n - rows.shape[-2]), (0, 0)]
    return jnp.pad(rows, pad)


def _hgrn_params(lb, norm_g):
    lb = lb.reshape(2, HG_HEADS, HG_DK)
    rows = []
    for d in range(2):
        rows += [jnp.log(jnp.maximum(lb[d], LB_EPS)), jnp.log1p(-lb[d]), 1.0 - lb[d]]
    rows.append(norm_g.reshape(HG_HEADS, HG_DK))
    return _pad_rows(rows, SUBLANES)


def _rg_params(conv_w, conv_b, b_a, b_x, lam):
    ng = RG_WIDTH // LANES
    g = lambda t: t.reshape(ng, LANES)
    rows = [g(conv_w[j]) for j in range(RG_CONV_W)] + [g(conv_b)]
    rows += [g(b_a[0]), g(b_x[0]), g(b_a[1]), g(b_x[1]), g(lam[0]), g(lam[1])]
    return _pad_rows(rows, 2 * SUBLANES)


def _rg_gate_weights(w_a, w_x):
    ng = RG_WIDTH // LANES
    per = RG_BLOCKS // ng

    def dense(w):
        w = w.reshape(ng, per, RG_BLOCK_W, RG_BLOCK_W)
        eye = jnp.eye(per, dtype=w.dtype)
        return jnp.einsum('gpcd,pq->gpcqd', w, eye).reshape(ng, LANES, LANES)

    return jnp.concatenate([dense(w_a[0]), dense(w_x[0]), dense(w_a[1]), dense(w_x[1])], axis=-1).astype(BF16)


def _attn_params(q_g, k_g):
    H = ATT_HEADS_PER_GROUP
    groups = []
    for gi, (_, dil) in enumerate(ATT_GROUPS):
        heads = jnp.arange(gi * H + 1, (gi + 1) * H + 1, dtype=F32)
        slopes = 2.0 ** (-8.0 * heads / ATT_HEADS) * dil
        rows = [jnp.broadcast_to(q_g * (ATT_DH ** -0.5), (H, LANES)), jnp.broadcast_to(k_g, (H, LANES)),
                jnp.broadcast_to(slopes[:, None], (H, LANES))]
        groups.append(_pad_rows(rows, SUBLANES))
    return jnp.stack(groups, axis=1)


def _trunk(x3, p):
    B, S, _ = x3.shape
    x = x3.reshape(B * S, D_MODEL)
    sm = jax.nn.softmax(p['hg_lb_logits'].astype(F32), axis=1)
    lower_bounds = jnp.cumsum(sm, axis=1) - sm[:, :1]
    for l in range(DEPTH):
        proj = _inproj(x, p['norm_mix_g'][l][None], p['w_in'][l].astype(BF16))
        hg = _hgrn(proj, _hgrn_params(lower_bounds[:, l], p['hg_norm_g'][l]), B, S)
        rg = _rglru(proj,
                    _rg_params(p['rg_conv_w'][l], p['rg_conv_b'][l], p['rg_b_a'][l], p['rg_b_x'][l],
                               p['rg_lambda'][l]),
                    _rg_gate_weights(p['rg_w_a'][l], p['rg_w_x'][l]), B, S)
        att = _attention(proj, _attn_params(p['attn_q_g'][l], p['attn_k_g'][l]), B, S)
        x1, h2 = _merge(x, proj, hg, rg, att, p['w_branch'][l].astype(BF16), p['w_out'][l].astype(BF16),
                        p['norm_ffn_g'][l][None])
        j = l // 2
        if l % 2 == 0:
            x = _ffn_dense(h2, x1, p['ffn_w1'][j].astype(BF16), p['ffn_w3'][j].astype(BF16),
                           p['ffn_w2'][j].astype(BF16))
        else:
            wr = jnp.pad(p['moe_router'][j], ((0, 0), (0, LANES - N_EXPERTS))).astype(BF16)
            x = _moe(h2, x1, wr, p['moe_w1'][j].astype(BF16), p['moe_w3'][j].astype(BF16),
                     p['moe_w2'][j].astype(BF16))
    return x.reshape(B, S, D_MODEL)


def kernel(x_prompt, x_sample, norm_mix_g, w_in, hg_lb_logits, hg_norm_g, rg_conv_w, rg_conv_b, rg_w_a, rg_b_a,
           rg_w_x, rg_b_x, rg_lambda, attn_q_g, attn_k_g, w_branch, w_out, norm_ffn_g, ffn_w1, ffn_w3, ffn_w2,
           moe_router, moe_w1, moe_w3, moe_w2):
    p = dict(norm_mix_g=norm_mix_g, w_in=w_in, hg_lb_logits=hg_lb_logits, hg_norm_g=hg_norm_g,
             rg_conv_w=rg_conv_w, rg_conv_b=rg_conv_b, rg_w_a=rg_w_a, rg_b_a=rg_b_a, rg_w_x=rg_w_x,
             rg_b_x=rg_b_x, rg_lambda=rg_lambda, attn_q_g=attn_q_g, attn_k_g=attn_k_g,
             w_branch=w_branch, w_out=w_out, norm_ffn_g=norm_ffn_g, ffn_w1=ffn_w1, ffn_w3=ffn_w3,
             ffn_w2=ffn_w2, moe_router=moe_router, moe_w1=moe_w1, moe_w3=moe_w3, moe_w2=moe_w2)
    return (_trunk(x_prompt, p), _trunk(x_sample, p))
```

```python
import functools
import math

import jax
import jax.numpy as jnp
from jax import lax
from jax.experimental import pallas as pl
from jax.experimental.pallas import tpu as pltpu

F32 = jnp.float32
BF16 = jnp.bfloat16

LANES = 128
SUBLANES = 8
VMEM_LIMIT_BYTES = 56 * 1024 * 1024

D_MODEL = 1024
DEPTH = 2
BRANCH_W = 512
HG_HEADS = 4
HG_DK = 128
HG_CHUNK = 64
HG_HEADS_PER_STEP = 2
HG_LEVELS = 6
LB_EPS = 1e-20
RG_WIDTH = 512
RG_BLOCKS = 8
RG_BLOCK_W = RG_WIDTH // RG_BLOCKS
RG_CONV_W = 4
RG_C = 8.0
RG_TILE = 256
RG_HALO = 8
ATT_GROUPS = ((128, 1), (512, 4), (2048, 16))
ATT_HEADS_PER_GROUP = 4
ATT_HEADS = ATT_HEADS_PER_GROUP * len(ATT_GROUPS)
ATT_DH = 128
ATT_RADIUS = 64
ATT_QBLK = 128
ATT_UNROLL = 8
ATT_TILE = 256
FFN_DENSE = 2816
N_EXPERTS = 8
TOP_K = 2
FFN_EXPERT = 3584
NORM_EPS = 1e-6
NEG_INF = -1e30

CB_HQ, CB_HF, CB_HB, CB_HV, CB_HG = 0, 4, 8, 12, 16
CB_RX, CB_RGATE = 20, 24
CB_AQ, CB_AK, CB_AV = 28, 40, 52
CB_GATES = 64
IN_COLS = 11264
N_CB = IN_COLS // LANES

INPROJ_TM, INPROJ_TN = 1024, 1024
MERGE_TM = 512
FFN_TM, FFN_TF = 512, 1408
MOE_ROWS, MOE_TF = 1024, 896
ROUTER_TM = 512
COMBINE_TM = 512


def _params(sem):
    return pltpu.CompilerParams(dimension_semantics=sem, vmem_limit_bytes=VMEM_LIMIT_BYTES)


def _dot(a, b):
    return jnp.dot(a, b, preferred_element_type=F32)


def _dot_nt(a, b):
    return lax.dot_general(a, b, (((1,), (1,)), ((), ())), preferred_element_type=F32)


def _sigmoid(x):
    return jax.nn.sigmoid(x)


def _lockstep(gens):
    results = [None] * len(gens)
    active = list(range(len(gens)))
    while active:
        for idx in list(active):
            try:
                next(gens[idx])
            except StopIteration as stop:
                results[idx] = stop.value
                active.remove(idx)
    return results


def _inproj_kernel(x_ref, g_ref, w_ref, o_ref, h_ref):
    @pl.when(pl.program_id(1) == 0)
    def _():
        x = x_ref[...]
        ms = jnp.mean(x * x, axis=-1, keepdims=True)
        h_ref[...] = (x * lax.rsqrt(ms + NORM_EPS) * g_ref[...]).astype(BF16)

    acc = _dot(h_ref[...], w_ref[...])
    for c in range(INPROJ_TN // LANES):
        o_ref[c] = acc[:, c * LANES:(c + 1) * LANES].astype(BF16)


def _inproj(x, g, w):
    T = x.shape[0]
    return pl.pallas_call(
        _inproj_kernel,
        grid=(T // INPROJ_TM, IN_COLS // INPROJ_TN),
        in_specs=[
            pl.BlockSpec((INPROJ_TM, D_MODEL), lambda i, j: (i, 0)),
            pl.BlockSpec((1, D_MODEL), lambda i, j: (0, 0)),
            pl.BlockSpec((D_MODEL, INPROJ_TN), lambda i, j: (0, j)),
        ],
        out_specs=pl.BlockSpec((INPROJ_TN // LANES, INPROJ_TM, LANES), lambda i, j: (j, i, 0)),
        out_shape=jax.ShapeDtypeStruct((N_CB, T, LANES), BF16),
        scratch_shapes=[pltpu.VMEM((INPROJ_TM, D_MODEL), BF16)],
        compiler_params=_params(("parallel", "arbitrary")),
        name="inproj",
    )(x, g, w)


def _hgrn_chain(q, vb, z, c0, c1, oml, code, states, slot, reverse):
    C = HG_CHUNK
    nv = C // SUBLANES
    e = jnp.exp(-jnp.abs(z))
    log_sig = jnp.minimum(z, 0.0) - jnp.log(1.0 + e)
    t = c1 + log_sig
    log_f = jnp.maximum(c0, t) + jnp.log(1.0 + jnp.exp(-jnp.abs(c0 - t)))
    k = oml * (jnp.where(z >= 0.0, e, 1.0) / (1.0 + e))
    sc_diag = _dot_nt(q.astype(BF16), k.astype(BF16))
    vt = vb.astype(F32).T.astype(BF16)
    yield
    x3 = log_f.reshape(nv, SUBLANES, LANES)
    sub3 = lax.broadcasted_iota(jnp.int32, (nv, SUBLANES, LANES), 1)
    d = 1
    while d < SUBLANES:
        if reverse:
            x3 = x3 + jnp.where(sub3 < SUBLANES - d, pltpu.roll(x3, SUBLANES - d, 1), 0.0)
        else:
            x3 = x3 + jnp.where(sub3 >= d, pltpu.roll(x3, d, 1), 0.0)
        d *= 2
    edge = 0 if reverse else SUBLANES - 1
    tot = jnp.broadcast_to(x3[:, edge:edge + 1, :], (nv, SUBLANES, LANES))
    groups = [None] * nv
    run = None
    for g in (range(nv - 1, -1, -1) if reverse else range(nv)):
        groups[g] = x3[g] if run is None else x3[g] + run
        run = tot[g] if run is None else run + tot[g]
    a3 = jnp.stack(groups)
    a = a3.reshape(C, LANES)
    sub = lax.broadcasted_iota(jnp.int32, (C, LANES), 0) & (SUBLANES - 1)
    yield
    q_in = (q * jnp.exp(a)).astype(BF16)
    a_last = a[0:1, :] if reverse else a[C - 1:C, :]
    k_end = (k * jnp.exp(a_last - a)).astype(BF16)
    st = states[slot]
    o_inter = _dot_nt(q_in, st.astype(BF16))
    states[slot] = st * jnp.exp(a_last) + _dot(vt, k_end)
    yield

    def pick(r):
        return jnp.broadcast_to(a3[:, r:r + 1, :], (nv, SUBLANES, LANES)).reshape(C, LANES)

    level_scores = []
    for b in range(HG_LEVELS):
        h = 1 << b
        m = h if reverse else h - 1
        if h == 1:
            if reverse:
                ref = jnp.where((sub & 1) == 0, pltpu.roll(a, C - 1, 0), a)
            else:
                ref = jnp.where((sub & 1) == 1, pltpu.roll(a, 1, 0), a)
        elif 2 * h < SUBLANES:
            ref = jnp.where(sub < 2 * h, pick(m), pick(m + 2 * h))
        elif 2 * h == SUBLANES:
            ref = pick(m)
        else:
            pieces = [jnp.broadcast_to(a[blk * 2 * h + m:blk * 2 * h + m + 1, :], (2 * h, LANES))
                      for blk in range(C // (2 * h))]
            ref = jnp.concatenate(pieces, axis=0) if len(pieces) > 1 else pieces[0]
        eb = jnp.exp(-jnp.abs(a - ref))
        level_scores.append(_dot_nt((q * eb).astype(BF16), (k * eb).astype(BF16)))
        yield
    scores = jnp.where(code == HG_LEVELS, sc_diag, 0.0)
    for b in range(HG_LEVELS):
        scores = jnp.where(code == b, level_scores[b], scores)
    o = o_inter + _dot(scores.astype(BF16), vb)
    yield
    return o


def _hgrn_kernel(*refs, S):
    C = HG_CHUNK
    n = S // C
    nh = HG_HEADS_PER_STEP
    head_refs = [refs[5 * hh:5 * hh + 5] for hh in range(nh)]
    par_ref, o_ref, acc_ref, code_ref = refs[5 * nh:]

    ti = lax.broadcasted_iota(jnp.int32, (C, C), 0)
    si = lax.broadcasted_iota(jnp.int32, (C, C), 1)
    x = ti ^ si
    lvl = jnp.zeros((C, C), jnp.int32)
    for b in range(1, HG_LEVELS):
        lvl = lvl + jnp.where(x >= (1 << b), 1, 0)
    diag = jnp.where(ti == si, HG_LEVELS, -1)
    code_ref[0] = jnp.where(ti > si, lvl, diag)
    code_ref[1] = jnp.where(ti < si, lvl, diag)

    def chain(ci, hh, d, states):
        q_ref, zf_ref, zb_ref, v_ref, _ = head_refs[hh]
        z_ref = zb_ref if d else zf_ref
        par = par_ref[hh]
        r0 = pl.multiple_of(ci * C, C)
        return _hgrn_chain(q_ref[pl.ds(r0, C), :].astype(F32), v_ref[pl.ds(r0, C), :],
                           z_ref[pl.ds(r0, C), :].astype(F32), par[3 * d:3 * d + 1], par[3 * d + 1:3 * d + 2],
                           par[3 * d + 2:3 * d + 3], code_ref[d], states, 2 * hh + d, d == 1)

    def make_body(accumulate):
        def body(i, carry):
            states = list(carry)
            jobs = []
            for hh in range(nh):
                jobs += [(2 * i, hh, 0), (2 * i + 1, hh, 0), (n - 1 - 2 * i, hh, 1), (n - 2 - 2 * i, hh, 1)]
            outs = _lockstep([chain(ci, hh, d, states) for ci, hh, d in jobs])
            for (ci, hh, _), o in zip(jobs, outs):
                r0 = pl.multiple_of(ci * C, C)
                if accumulate:
                    acc_ref[hh, pl.ds(r0, C), :] += o
                else:
                    acc_ref[hh, pl.ds(r0, C), :] = o
            return tuple(states)
        return body

    zero = jnp.zeros((LANES, HG_DK), F32)
    carry = lax.fori_loop(0, n // 4, make_body(False), (zero,) * (2 * nh))
    lax.fori_loop(n // 4, n // 2, make_body(True), carry)

    ft = 256

    def fin(i, _):
        r0 = pl.multiple_of(i * ft, ft)
        for hh in range(nh):
            o = acc_ref[hh, pl.ds(r0, ft), :]
            y = o * lax.rsqrt(jnp.mean(o * o, axis=-1, keepdims=True) + NORM_EPS) * par_ref[hh][6:7]
            g = head_refs[hh][4][pl.ds(r0, ft), :].astype(F32)
            o_ref[hh, pl.ds(r0, ft), :] = (y * (g * _sigmoid(g))).astype(BF16)
        return 0

    lax.fori_loop(0, S // ft, fin, 0)


def _hgrn(proj, par, B, S):
    nh = HG_HEADS_PER_STEP

    def spec(cb0, hh):
        return pl.BlockSpec((None, S, LANES), lambda b, h: (cb0 + nh * h + hh, b, 0))

    in_specs = []
    for hh in range(nh):
        in_specs += [spec(CB_HQ, hh), spec(CB_HF, hh), spec(CB_HB, hh), spec(CB_HV, hh), spec(CB_HG, hh)]
    in_specs.append(pl.BlockSpec((nh, SUBLANES, LANES), lambda b, h: (h, 0, 0)))
    return pl.pallas_call(
        functools.partial(_hgrn_kernel, S=S),
        grid=(B, HG_HEADS // nh),
        in_specs=in_specs,
        out_specs=pl.BlockSpec((nh, S, LANES), lambda b, h: (h, b, 0)),
        out_shape=jax.ShapeDtypeStruct((HG_HEADS, B * S, LANES), BF16),
        scratch_shapes=[pltpu.VMEM((nh, S, LANES), F32), pltpu.VMEM((2, HG_CHUNK, HG_CHUNK), jnp.int32)],
        compiler_params=_params(("parallel", "parallel")),
        name="hgrn2",
    )(*([proj] * (5 * nh)), par)


def _linear_scan_tile(a, u, carry, reverse):
    n = a.shape[0]
    nv = n // SUBLANES
    a3 = a.reshape(nv, SUBLANES, LANES)
    u3 = u.reshape(nv, SUBLANES, LANES)
    sub = lax.broadcasted_iota(jnp.int32, (nv, SUBLANES, LANES), 1)
    d = 1
    while d < SUBLANES:
        if reverse:
            keep = sub < SUBLANES - d
            shift = SUBLANES - d
        else:
            keep = sub >= d
            shift = d
        a_s = jnp.where(keep, pltpu.roll(a3, shift, 1), 1.0)
        u_s = jnp.where(keep, pltpu.roll(u3, shift, 1), 0.0)
        u3 = a3 * u_s + u3
        a3 = a3 * a_s
        d *= 2
        yield
    edge = 0 if reverse else SUBLANES - 1
    a_tot = jnp.broadcast_to(a3[:, edge:edge + 1, :], a3.shape)
    u_tot = jnp.broadcast_to(u3[:, edge:edge + 1, :], u3.shape)
    groups = [None] * nv
    for g in (range(nv - 1, -1, -1) if reverse else range(nv)):
        groups[g] = u3[g] + a3[g] * carry
        carry = u_tot[g] + a_tot[g] * carry
        if g % 4 == 0:
            yield
    return jnp.stack(groups).reshape(n, LANES), carry


def _rg_kernel(x_ref, gate_ref, par_ref, w_ref, o_ref, xpad_ref, xc_ref, h_ref, *, S):
    TS = RG_TILE
    n = S // TS
    par = par_ref[...]
    conv_b = par[4:5]

    xpad_ref[pl.ds(0, RG_HALO), :] = jnp.zeros((RG_HALO, LANES), F32)
    xpad_ref[pl.ds(S + RG_HALO, RG_HALO), :] = jnp.zeros((RG_HALO, LANES), F32)

    def copy(i, _):
        r0 = pl.multiple_of(i * TS, TS)
        xpad_ref[pl.ds(r0 + RG_HALO, TS), :] = x_ref[pl.ds(r0, TS), :].astype(F32)
        return 0

    lax.fori_loop(0, n, copy, 0)

    def conv(i, _):
        r0 = pl.multiple_of(i * TS, TS)
        xc = conv_b
        for j in range(RG_CONV_W):
            off = RG_HALO - RG_CONV_W // 2 + j
            xc = xc + par[j:j + 1] * xpad_ref[pl.ds(r0 + off, TS), :]
        xc_ref[pl.ds(r0, TS), :] = xc
        return 0

    lax.fori_loop(0, n, conv, 0)

    def tile(i, carries, d, final):
        r0 = pl.multiple_of(i * TS, TS)
        xc = xc_ref[pl.ds(r0, TS), :]
        gts = _dot(xc.astype(BF16), w_ref[:, d * 2 * LANES:(d + 1) * 2 * LANES])
        yield
        r = _sigmoid(gts[:, :LANES] + par[5 + 2 * d:6 + 2 * d])
        ig = _sigmoid(gts[:, LANES:] + par[6 + 2 * d:7 + 2 * d])
        lam = par[9 + d:10 + d]
        softplus_neg_lam = jnp.maximum(-lam, 0.0) + jnp.log1p(jnp.exp(-jnp.abs(lam)))
        log_a = (-RG_C) * r * softplus_neg_lam
        a = jnp.exp(log_a)
        y = 1.0 - a * a
        u = jnp.where(y > 0.0, y * lax.rsqrt(y), 0.0) * (ig * xc)
        yield
        h, carries[d] = yield from _linear_scan_tile(a, u, carries[d], reverse=(d == 1))
        yield
        if final:
            gate = gate_ref[pl.ds(r0, TS), :].astype(F32)
            o_ref[pl.ds(r0, TS), :] = (jax.nn.gelu(gate) * (h_ref[pl.ds(r0, TS), :] + h)).astype(BF16)
        else:
            h_ref[pl.ds(r0, TS), :] = h

    def make_body(final):
        def body(i, carry):
            carries = list(carry)
            _lockstep([tile(i, carries, 0, final), tile(n - 1 - i, carries, 1, final)])
            return tuple(carries)
        return body

    zero = jnp.zeros((SUBLANES, LANES), F32)
    carry = lax.fori_loop(0, n // 2, make_body(False), (zero, zero))
    lax.fori_loop(n // 2, n, make_body(True), carry)


def _rglru(proj, par, w4, B, S):
    ng = RG_WIDTH // LANES

    def spec(cb0):
        return pl.BlockSpec((None, S, LANES), lambda b, g: (cb0 + g, b, 0))

    return pl.pallas_call(
        functools.partial(_rg_kernel, S=S),
        grid=(B, ng),
        in_specs=[spec(CB_RX), spec(CB_RGATE),
                  pl.BlockSpec((None, 2 * SUBLANES, LANES), lambda b, g: (g, 0, 0)),
                  pl.BlockSpec((None, LANES, 4 * LANES), lambda b, g: (g, 0, 0))],
        out_specs=pl.BlockSpec((None, S, LANES), lambda b, g: (g, b, 0)),
        out_shape=jax.ShapeDtypeStruct((ng, B * S, LANES), BF16),
        scratch_shapes=[pltpu.VMEM((S + 2 * RG_HALO, LANES), F32), pltpu.VMEM((S, LANES), F32),
                        pltpu.VMEM((S, LANES), F32)],
        compiler_params=_params(("parallel", "parallel")),
        name="rglru",
    )(proj, proj, par, w4)


def _attn_kernel(q0_ref, k0_ref, v0_ref, q1_ref, k1_ref, v1_ref, q2_ref, k2_ref, v2_ref, par_ref, o_ref,
                 perm_ref, qd_ref, kd_ref, vd_ref, bias_ref, og_ref, lg_ref, *, S):
    refs = ((q0_ref, k0_ref, v0_ref), (q1_ref, k1_ref, v1_ref), (q2_ref, k2_ref, v2_ref))
    ct = ATT_TILE

    for g, (_, dil) in enumerate(ATT_GROUPS):
        q_ref, k_ref, v_ref = refs[g]
        par = par_ref[g]
        qg, kg, slope = par[0:1], par[1:2], par[2:3, 0:1]
        L = S // dil
        Q = min(ATT_QBLK, L)
        KW = min(Q + 2 * ATT_RADIUS, L)
        nq = L // Q
        nq_shift = int(math.log2(nq))

        for var in range(3):
            qi = var * ATT_RADIUS + lax.broadcasted_iota(jnp.int32, (Q, KW), 0)
            rel = jnp.abs(qi - lax.broadcasted_iota(jnp.int32, (Q, KW), 1))
            bias_ref[var, :Q, :KW] = jnp.where(rel <= ATT_RADIUS, -slope * rel.astype(F32), NEG_INF)

        per = ct // dil
        if dil > 1:
            pi = lax.broadcasted_iota(jnp.int32, (ct, ct), 0)
            pj = lax.broadcasted_iota(jnp.int32, (ct, ct), 1)
            src_tok = (pi & (per - 1)) * dil + lax.shift_right_logical(pi, int(math.log2(per)))
            perm_ref[...] = jnp.where(pj == src_tok, 1.0, 0.0).astype(BF16)

        def prep_tile(i, q_ref=q_ref, k_ref=k_ref, v_ref=v_ref, qg=qg, kg=kg, dil=dil, per=per, L=L):
            r0 = pl.multiple_of(i * ct, ct)
            xs = (q_ref[pl.ds(r0, ct), :].astype(F32), k_ref[pl.ds(r0, ct), :].astype(F32))
            sums = [jnp.sum(x * x, axis=-1, keepdims=True) for x in xs]
            yield
            tiles = [(x * lax.rsqrt(ss * (1.0 / LANES) + NORM_EPS) * gain).astype(BF16)
                     for x, ss, gain in zip(xs, sums, (qg, kg))]
            tiles.append(v_ref[pl.ds(r0, ct), :])
            if dil > 1:
                tiles = [_dot(perm_ref[...], x) for x in tiles]
                yield
            for dst_ref, y in zip((qd_ref, kd_ref, vd_ref), tiles):
                if dil == 1:
                    dst_ref[pl.ds(r0, ct), :] = y
                else:
                    y = y.astype(BF16)
                    for r in range(dil):
                        dst_ref[pl.ds(pl.multiple_of(r * L + i * per, 16), per), :] = y[r * per:(r + 1) * per, :]

        prep_unroll = min(4, S // ct)

        def prep(i, _, prep_tile=prep_tile, prep_unroll=prep_unroll):
            _lockstep([prep_tile(i * prep_unroll + u) for u in range(prep_unroll)])
            return 0

        lax.fori_loop(0, S // ct // prep_unroll, prep, 0)

        def rows(start, size, dil=dil):
            if dil == 1:
                return pl.ds(start, size)
            return pl.ds(start, size, stride=dil)

        unroll = min(ATT_UNROLL, dil * nq)

        def qblock(it, g=g, dil=dil, L=L, Q=Q, KW=KW, nq=nq, nq_shift=nq_shift, rows=rows):
            r = lax.shift_right_logical(it, nq_shift)
            m0 = (it & (nq - 1)) * Q
            ks = jnp.clip(m0 - ATT_RADIUS, 0, L - KW)
            var = lax.shift_right_logical(m0 - ks, int(math.log2(ATT_RADIUS)))
            base = r * L
            qn = qd_ref[pl.ds(pl.multiple_of(base + m0, 16), Q), :]
            kk = kd_ref[pl.ds(pl.multiple_of(base + ks, 16), KW), :]
            s = _dot_nt(qn, kk)
            yield
            s = s + bias_ref[var, :Q, :KW]
            m = jnp.max(s, axis=-1, keepdims=True)
            yield
            pb = jnp.exp(s - m).astype(BF16)
            vv = vd_ref[pl.ds(pl.multiple_of(base + ks, 16), KW), :]
            ol = _dot(pb, jnp.concatenate([vv, jnp.ones((KW, LANES), BF16)], axis=1))
            yield
            l = ol[:, LANES:]
            og_ref[g, rows(m0 * dil + r, Q), :] = ol[:, :LANES] / l
            lg_ref[g, rows(m0 * dil + r, Q), :] = m + jnp.log(l)

        def qblocks(i, _, unroll=unroll, qblock=qblock):
            _lockstep([qblock(i * unroll + u) for u in range(unroll)])
            return 0

        lax.fori_loop(0, (dil * nq) // unroll, qblocks, 0)

    def merge(i, _):
        r0 = pl.multiple_of(i * ct, ct)
        l0, l1, l2 = (lg_ref[g, pl.ds(r0, ct), :] for g in range(3))
        m = jnp.maximum(jnp.maximum(l0, l1), l2)
        e0, e1, e2 = jnp.exp(l0 - m), jnp.exp(l1 - m), jnp.exp(l2 - m)
        o = (e0 * og_ref[0, pl.ds(r0, ct), :] + e1 * og_ref[1, pl.ds(r0, ct), :]
             + e2 * og_ref[2, pl.ds(r0, ct), :]) / (e0 + e1 + e2)
        o_ref[pl.ds(r0, ct), :] = o.astype(BF16)
        return 0

    lax.fori_loop(0, S // ct, merge, 0)


def _attention(proj, par, B, S):
    H = ATT_HEADS_PER_GROUP
    ngroups = len(ATT_GROUPS)

    def spec(cb0, g):
        return pl.BlockSpec((None, S, LANES), lambda b, j: (cb0 + g * H + j, b, 0))

    in_specs = []
    for g in range(ngroups):
        in_specs += [spec(CB_AQ, g), spec(CB_AK, g), spec(CB_AV, g)]
    in_specs.append(pl.BlockSpec((None, ngroups, SUBLANES, LANES), lambda b, j: (j, 0, 0, 0)))
    return pl.pallas_call(
        functools.partial(_attn_kernel, S=S),
        grid=(B, H),
        in_specs=in_specs,
        out_specs=pl.BlockSpec((None, S, LANES), lambda b, j: (j, b, 0)),
        out_shape=jax.ShapeDtypeStruct((H, B * S, LANES), BF16),
        scratch_shapes=[pltpu.VMEM((ATT_TILE, ATT_TILE), BF16),
                        pltpu.VMEM((S, LANES), BF16), pltpu.VMEM((S, LANES), BF16), pltpu.VMEM((S, LANES), BF16),
                        pltpu.VMEM((3, ATT_QBLK, ATT_QBLK + 2 * ATT_RADIUS), F32),
                        pltpu.VMEM((ngroups, S, LANES), F32), pltpu.VMEM((ngroups, S, LANES), F32)],
        compiler_params=_params(("parallel", "parallel")),
        name="attention",
    )(*([proj] * (3 * ngroups)), par)


def _cat_heads(ref):
    return jnp.concatenate([ref[c] for c in range(ref.shape[0])], axis=-1)


def _merge_kernel(x_ref, hg_ref, rg_ref, att_ref, ga_ref, gb_ref, gc_ref, wb_ref, wo_ref, ng_ref,
                  x1_ref, h2_ref):
    merged = _sigmoid(_cat_heads(ga_ref).astype(F32)) * _dot(_cat_heads(hg_ref), wb_ref[0])
    merged += _sigmoid(_cat_heads(gb_ref).astype(F32)) * _dot(_cat_heads(rg_ref), wb_ref[1])
    merged += _sigmoid(_cat_heads(gc_ref).astype(F32)) * _dot(_cat_heads(att_ref), wb_ref[2])
    x1 = x_ref[...] + _dot(merged.astype(BF16), wo_ref[...])
    x1_ref[...] = x1
    ms = jnp.mean(x1 * x1, axis=-1, keepdims=True)
    h2_ref[...] = (x1 * lax.rsqrt(ms + NORM_EPS) * ng_ref[...]).astype(BF16)


def _merge(x, proj, hg, rg, att, wb, wo, ng):
    T = x.shape[0]
    tm = MERGE_TM
    nb = BRANCH_W // LANES
    ngate = D_MODEL // LANES

    branch = pl.BlockSpec((nb, tm, LANES), lambda i: (0, i, 0))

    def gate(n):
        return pl.BlockSpec((ngate, tm, LANES), lambda i: (CB_GATES // ngate + n, i, 0))

    row = pl.BlockSpec((tm, D_MODEL), lambda i: (i, 0))
    return pl.pallas_call(
        _merge_kernel,
        grid=(T // tm,),
        in_specs=[row, branch, branch, branch, gate(0), gate(1), gate(2),
                  pl.BlockSpec((3, BRANCH_W, D_MODEL), lambda i: (0, 0, 0)),
                  pl.BlockSpec((D_MODEL, D_MODEL), lambda i: (0, 0)),
                  pl.BlockSpec((1, D_MODEL), lambda i: (0, 0))],
        out_specs=[row, row],
        out_shape=[jax.ShapeDtypeStruct((T, D_MODEL), F32), jax.ShapeDtypeStruct((T, D_MODEL), BF16)],
        compiler_params=_params(("parallel",)),
        name="merge",
    )(x, hg, rg, att, proj, proj, proj, wb, wo, ng)


def _ffn_kernel(h_ref, x_ref, w1_ref, w3_ref, w2_ref, o_ref, acc_ref):
    f = pl.program_id(1)
    h = h_ref[...]
    h1 = _dot(h, w1_ref[...])
    y = _dot((h1 * _sigmoid(h1) * _dot(h, w3_ref[...])).astype(BF16), w2_ref[...])

    @pl.when(f == 0)
    def _():
        acc_ref[...] = x_ref[...] + y

    @pl.when(f > 0)
    def _():
        acc_ref[...] += y

    @pl.when(f == pl.num_programs(1) - 1)
    def _():
        o_ref[...] = acc_ref[...]


def _ffn_dense(h2, x1, w1, w3, w2):
    T = h2.shape[0]
    tm, tf = FFN_TM, FFN_TF
    return pl.pallas_call(
        _ffn_kernel,
        grid=(T // tm, FFN_DENSE // tf),
        in_specs=[pl.BlockSpec((tm, D_MODEL), lambda i, f: (i, 0)),
                  pl.BlockSpec((tm, D_MODEL), lambda i, f: (i, 0)),
                  pl.BlockSpec((D_MODEL, tf), lambda i, f: (0, f)),
                  pl.BlockSpec((D_MODEL, tf), lambda i, f: (0, f)),
                  pl.BlockSpec((tf, D_MODEL), lambda i, f: (f, 0))],
        out_specs=pl.BlockSpec((tm, D_MODEL), lambda i, f: (i, 0)),
        out_shape=jax.ShapeDtypeStruct((T, D_MODEL), F32),
        scratch_shapes=[pltpu.VMEM((tm, D_MODEL), F32)],
        compiler_params=_params(("parallel", "arbitrary")),
        name="ffn_dense",
    )(h2, x1, w1, w3, w2)


def _router_kernel(h_ref, wr_ref, o_ref, cnt_ref, base_ref):
    tm = ROUTER_TM

    @pl.when(pl.program_id(0) == 0)
    def _():
        base_ref[...] = jnp.zeros_like(base_ref)

    lane = lax.broadcasted_iota(jnp.int32, (tm, LANES), 1)
    logits = jnp.where(lane < N_EXPERTS, _dot(h_ref[...], wr_ref[...]), -jnp.inf)
    m1 = jnp.max(logits, axis=-1, keepdims=True)
    i1 = jnp.min(jnp.where(logits == m1, lane, LANES), axis=-1, keepdims=True)
    rest = jnp.where(lane == i1, -jnp.inf, logits)
    m2 = jnp.max(rest, axis=-1, keepdims=True)
    i2 = jnp.min(jnp.where(rest == m2, lane, LANES), axis=-1, keepdims=True)
    e21 = jnp.exp(m2 - m1)
    g1 = 1.0 / (1.0 + e21)
    g2 = e21 / (1.0 + e21)

    onehot = jnp.where(lane == i1, 1.0, jnp.where(lane == i2, 1.0, 0.0))
    ti = lax.broadcasted_iota(jnp.int32, (tm, tm), 0)
    si = lax.broadcasted_iota(jnp.int32, (tm, tm), 1)
    before = jnp.where(si < ti, 1.0, 0.0).astype(BF16)
    pos = _dot(before, onehot.astype(BF16)) + base_ref[0:1, :]
    r1 = jnp.sum(jnp.where(lane == i1, pos, 0.0), axis=-1, keepdims=True)
    r2 = jnp.sum(jnp.where(lane == i2, pos, 0.0), axis=-1, keepdims=True)
    total = base_ref[0:1, :] + jnp.sum(onehot, axis=0, keepdims=True)
    base_ref[...] = jnp.broadcast_to(total, base_ref.shape)
    cnt_ref[...] = jnp.broadcast_to(total, cnt_ref.shape)

    out = jnp.where(lane == 0, i1.astype(F32), 0.0)
    out = jnp.where(lane == 1, i2.astype(F32), out)
    out = jnp.where(lane == 2, g1, out)
    out = jnp.where(lane == 3, g2, out)
    out = jnp.where(lane == 4, r1, out)
    out = jnp.where(lane == 5, r2, out)
    o_ref[...] = out


def _router(h2, wr):
    T = h2.shape[0]
    tm = ROUTER_TM
    return pl.pallas_call(
        _router_kernel,
        grid=(T // tm,),
        in_specs=[pl.BlockSpec((tm, D_MODEL), lambda i: (i, 0)),
                  pl.BlockSpec((D_MODEL, LANES), lambda i: (0, 0))],
        out_specs=[pl.BlockSpec((tm, LANES), lambda i: (i, 0)),
                   pl.BlockSpec((SUBLANES, LANES), lambda i: (0, 0))],
        out_shape=[jax.ShapeDtypeStruct((T, LANES), F32), jax.ShapeDtypeStruct((SUBLANES, LANES), F32)],
        scratch_shapes=[pltpu.VMEM((SUBLANES, LANES), F32)],
        compiler_params=_params(("arbitrary",)),
        name="moe_router",
    )(h2, wr)


def _expert_kernel(be_ref, nu_ref, x_ref, w1_ref, w3_ref, w2_ref, o_ref, acc_ref):
    i = pl.program_id(0)
    f = pl.program_id(1)

    @pl.when(f == 0)
    def _():
        acc_ref[...] = jnp.zeros_like(acc_ref)

    @pl.when(i < nu_ref[0])
    def _():
        x = x_ref[...]
        h1 = _dot(x, w1_ref[...])
        acc_ref[...] += _dot((h1 * _sigmoid(h1) * _dot(x, w3_ref[...])).astype(BF16), w2_ref[...])

    @pl.when(f == pl.num_programs(1) - 1)
    def _():
        o_ref[...] = acc_ref[...].astype(BF16)


def _experts(rows, block_expert, n_used, w1, w3, w2):
    n_rows = rows.shape[0]
    R, tf = MOE_ROWS, MOE_TF
    grid_spec = pltpu.PrefetchScalarGridSpec(
        num_scalar_prefetch=2,
        grid=(n_rows // R, FFN_EXPERT // tf),
        in_specs=[pl.BlockSpec((R, D_MODEL), lambda i, f, be, nu: (i, 0)),
                  pl.BlockSpec((None, D_MODEL, tf), lambda i, f, be, nu: (be[i], 0, f)),
                  pl.BlockSpec((None, D_MODEL, tf), lambda i, f, be, nu: (be[i], 0, f)),
                  pl.BlockSpec((None, tf, D_MODEL), lambda i, f, be, nu: (be[i], f, 0))],
        out_specs=pl.BlockSpec((R, D_MODEL), lambda i, f, be, nu: (i, 0)),
        scratch_shapes=[pltpu.VMEM((R, D_MODEL), F32)],
    )
    return pl.pallas_call(
        _expert_kernel,
        grid_spec=grid_spec,
        out_shape=jax.ShapeDtypeStruct((n_rows, D_MODEL), BF16),
        compiler_params=_params(("parallel", "arbitrary")),
        name="moe_experts",
    )(block_expert, n_used, rows, w1, w3, w2)


def _combine_kernel(x_ref, y1_ref, y2_ref, info_ref, o_ref):
    info = info_ref[...]
    o_ref[...] = (x_ref[...] + info[:, 2:3] * y1_ref[...].astype(F32)
                  + info[:, 3:4] * y2_ref[...].astype(F32))


def _combine(x1, y1, y2, info):
    T = x1.shape[0]
    tm = COMBINE_TM
    row = pl.BlockSpec((tm, D_MODEL), lambda i: (i, 0))
    return pl.pallas_call(
        _combine_kernel,
        grid=(T // tm,),
        in_specs=[row, row, row, pl.BlockSpec((tm, LANES), lambda i: (i, 0))],
        out_specs=row,
        out_shape=jax.ShapeDtypeStruct((T, D_MODEL), F32),
        compiler_params=_params(("parallel",)),
        name="moe_combine",
    )(x1, y1, y2, info)


def _moe(h2, x1, wr, w1, w3, w2):
    T = h2.shape[0]
    R = MOE_ROWS
    n_blocks = -(-(T * TOP_K) // R) + N_EXPERTS
    info, cnt = _router(h2, wr)
    expert = info[:, 0:2].astype(jnp.int32)
    rank = info[:, 4:6].astype(jnp.int32)
    counts = cnt[0, :N_EXPERTS].astype(jnp.int32)
    blocks_per = (counts + R - 1) // R
    bend = jnp.cumsum(blocks_per)
    pstart = (bend - blocks_per) * R
    dest = pstart[expert] + rank
    n_used = bend[-1:]
    block_expert = jnp.minimum(
        jnp.searchsorted(bend, jnp.arange(n_blocks, dtype=jnp.int32), side='right'),
        N_EXPERTS - 1).astype(jnp.int32)
    token = jnp.broadcast_to(jnp.arange(T, dtype=jnp.int32)[:, None], (T, TOP_K))
    tok_of_row = jnp.zeros((n_blocks * R,), jnp.int32).at[dest.reshape(-1)].set(token.reshape(-1))
    rows = h2[tok_of_row]
    y = _experts(rows, block_expert, n_used, w1, w3, w2)
    return _combine(x1, y[dest[:, 0]], y[dest[:, 1]], info)


def _pad_rows(rows, n):
    rows = jnp.stack(rows, axis=-2)
    pad = [(0, 0)] * (rows.ndim - 2) + [(0, n - rows.shape[-2]), (0, 0)]
    return jnp.pad(rows, pad)


def _hgrn_params(lb, norm_g):
    lb = lb.reshape(2, HG_HEADS, HG_DK)
    rows = []
    for d in range(2):
        rows += [jnp.log(jnp.maximum(lb[d], LB_EPS)), jnp.log1p(-lb[d]), 1.0 - lb[d]]
    rows.append(norm_g.reshape(HG_HEADS, HG_DK))
    return _pad_rows(rows, SUBLANES)


def _rg_params(conv_w, conv_b, b_a, b_x, lam):
    ng = RG_WIDTH // LANES
    g = lambda t: t.reshape(ng, LANES)
    rows = [g(conv_w[j]) for j in range(RG_CONV_W)] + [g(conv_b)]
    rows += [g(b_a[0]), g(b_x[0]), g(b_a[1]), g(b_x[1]), g(lam[0]), g(lam[1])]
    return _pad_rows(rows, 2 * SUBLANES)


def _rg_gate_weights(w_a, w_x):
    ng = RG_WIDTH // LANES
    per = RG_BLOCKS // ng

    def dense(w):
        w = w.reshape(ng, per, RG_BLOCK_W, RG_BLOCK_W)
        eye = jnp.eye(per, dtype=w.dtype)
        return jnp.einsum('gpcd,pq->gpcqd', w, eye).reshape(ng, LANES, LANES)

    return jnp.concatenate([dense(w_a[0]), dense(w_x[0]), dense(w_a[1]), dense(w_x[1])], axis=-1).astype(BF16)


def _attn_params(q_g, k_g):
    H = ATT_HEADS_PER_GROUP
    groups = []
    for gi, (_, dil) in enumerate(ATT_GROUPS):
        heads = jnp.arange(gi * H + 1, (gi + 1) * H + 1, dtype=F32)
        slopes = 2.0 ** (-8.0 * heads / ATT_HEADS) * dil
        rows = [jnp.broadcast_to(q_g * (ATT_DH ** -0.5), (H, LANES)), jnp.broadcast_to(k_g, (H, LANES)),
                jnp.broadcast_to(slopes[:, None], (H, LANES))]
        groups.append(_pad_rows(rows, SUBLANES))
    return jnp.stack(groups, axis=1)


def _trunk(x3, p):
    B, S, _ = x3.shape
    x = x3.reshape(B * S, D_MODEL)
    sm = jax.nn.softmax(p['hg_lb_logits'].astype(F32), axis=1)
    lower_bounds = jnp.cumsum(sm, axis=1) - sm[:, :1]
    for l in range(DEPTH):
        proj = _inproj(x, p['norm_mix_g'][l][None], p['w_in'][l].astype(BF16))
        hg = _hgrn(proj, _hgrn_params(lower_bounds[:, l], p['hg_norm_g'][l]), B, S)
        rg = _rglru(proj,
                    _rg_params(p['rg_conv_w'][l], p['rg_conv_b'][l], p['rg_b_a'][l], p['rg_b_x'][l],
                               p['rg_lambda'][l]),
                    _rg_gate_weights(p['rg_w_a'][l], p['rg_w_x'][l]), B, S)
        att = _attention(proj, _attn_params(p['attn_q_g'][l], p['attn_k_g'][l]), B, S)
        x1, h2 = _merge(x, proj, hg, rg, att, p['w_branch'][l].astype(BF16), p['w_out'][l].astype(BF16),
                        p['norm_ffn_g'][l][None])
        j = l // 2
        if l % 2 == 0:
            x = _ffn_dense(h2, x1, p['ffn_w1'][j].astype(BF16), p['ffn_w3'][j].astype(BF16),
                           p['ffn_w2'][j].astype(BF16))
        else:
            wr = jnp.pad(p['moe_router'][j], ((0, 0), (0, LANES - N_EXPERTS))).astype(BF16)
            x = _moe(h2, x1, wr, p['moe_w1'][j].astype(BF16), p['moe_w3'][j].astype(BF16),
                     p['moe_w2'][j].astype(BF16))
    return x.reshape(B, S, D_MODEL)


def kernel(x_prompt, x_sample, norm_mix_g, w_in, hg_lb_logits, hg_norm_g, rg_conv_w, rg_conv_b, rg_w_a, rg_b_a,
           rg_w_x, rg_b_x, rg_lambda, attn_q_g, attn_k_g, w_branch, w_out, norm_ffn_g, ffn_w1, ffn_w3, ffn_w2,
           moe_router, moe_w1, moe_w3, moe_w2):
    p = dict(norm_mix_g=norm_mix_g, w_in=w_in, hg_lb_logits=hg_lb_logits, hg_norm_g=hg_norm_g,
             rg_conv_w=rg_conv_w, rg_conv_b=rg_conv_b, rg_w_a=rg_w_a, rg_b_a=rg_b_a, rg_w_x=rg_w_x,
             rg_b_x=rg_b_x, rg_lambda=rg_lambda, attn_q_g=attn_q_g, attn_k_g=attn_k_g,
             w_branch=w_branch, w_out=w_out, norm_ffn_g=norm_ffn_g, ffn_w1=ffn_w1, ffn_w3=ffn_w3,
             ffn_w2=ffn_w2, moe_router=moe_router, moe_w1=moe_w1, moe_w3=moe_w3, moe_w2=moe_w2)
    return (_trunk(x_prompt, p), _trunk(x_sample, p))
```

```python
import functools
import math

import jax
import jax.numpy as jnp
from jax import lax
from jax.experimental import pallas as pl
from jax.experimental.pallas import tpu as pltpu

F32 = jnp.float32
BF16 = jnp.bfloat16

LANES = 128
SUBLANES = 8
VMEM_LIMIT_BYTES = 56 * 1024 * 1024

D_MODEL = 1024
DEPTH = 2
BRANCH_W = 512
HG_HEADS = 4
HG_DK = 128
HG_CHUNK = 64
HG_HEADS_PER_STEP = 2
HG_LEVELS = 6
LB_EPS = 1e-20
RG_WIDTH = 512
RG_BLOCKS = 8
RG_BLOCK_W = RG_WIDTH // RG_BLOCKS
RG_CONV_W = 4
RG_C = 8.0
RG_TILE = 256
RG_HALO = 8
ATT_GROUPS = ((128, 1), (512, 4), (2048, 16))
ATT_HEADS_PER_GROUP = 4
ATT_HEADS = ATT_HEADS_PER_GROUP * len(ATT_GROUPS)
ATT_DH = 128
ATT_RADIUS = 64
ATT_QBLK = 128
ATT_UNROLL = 8
ATT_TILE = 256
FFN_DENSE = 2816
N_EXPERTS = 8
TOP_K = 2
FFN_EXPERT = 3584
NORM_EPS = 1e-6
NEG_INF = -1e30

CB_HQ, CB_HF, CB_HB, CB_HV, CB_HG = 0, 4, 8, 12, 16
CB_RX, CB_RGATE = 20, 24
CB_AQ, CB_AK, CB_AV = 28, 40, 52
CB_GATES = 64
IN_COLS = 11264
N_CB = IN_COLS // LANES

INPROJ_TM, INPROJ_TN = 1024, 1408
MERGE_TM = 512
FFN_TM, FFN_TF = 512, 1408
MOE_ROWS, MOE_TF = 1024, 896
ROUTER_TM = 512
COMBINE_TM = 512


def _params(sem):
    return pltpu.CompilerParams(dimension_semantics=sem, vmem_limit_bytes=VMEM_LIMIT_BYTES)


def _dot(a, b):
    return jnp.dot(a, b, preferred_element_type=F32)


def _dot_nt(a, b):
    return lax.dot_general(a, b, (((1,), (1,)), ((), ())), preferred_element_type=F32)


def _sigmoid(x):
    return jax.nn.sigmoid(x)


def _lockstep(gens):
    results = [None] * len(gens)
    active = list(range(len(gens)))
    while active:
        for idx in list(active):
            try:
                next(gens[idx])
            except StopIteration as stop:
                results[idx] = stop.value
                active.remove(idx)
    return results


def _inproj_kernel(x_ref, g_ref, w_ref, o_ref, h_ref):
    @pl.when(pl.program_id(1) == 0)
    def _():
        x = x_ref[...]
        ms = jnp.mean(x * x, axis=-1, keepdims=True)
        h_ref[...] = (x * lax.rsqrt(ms + NORM_EPS) * g_ref[...]).astype(BF16)

    acc = _dot(h_ref[...], w_ref[...])
    for c in range(INPROJ_TN // LANES):
        o_ref[c] = acc[:, c * LANES:(c + 1) * LANES].astype(BF16)


def _inproj(x, g, w):
    T = x.shape[0]
    return pl.pallas_call(
        _inproj_kernel,
        grid=(T // INPROJ_TM, IN_COLS // INPROJ_TN),
        in_specs=[
            pl.BlockSpec((INPROJ_TM, D_MODEL), lambda i, j: (i, 0)),
            pl.BlockSpec((1, D_MODEL), lambda i, j: (0, 0)),
            pl.BlockSpec((D_MODEL, INPROJ_TN), lambda i, j: (0, j)),
        ],
        out_specs=pl.BlockSpec((INPROJ_TN // LANES, INPROJ_TM, LANES), lambda i, j: (j, i, 0)),
        out_shape=jax.ShapeDtypeStruct((N_CB, T, LANES), BF16),
        scratch_shapes=[pltpu.VMEM((INPROJ_TM, D_MODEL), BF16)],
        compiler_params=_params(("parallel", "arbitrary")),
        name="inproj",
    )(x, g, w)


def _hgrn_chain(q, vb, z, c0, c1, oml, code, states, slot, reverse):
    C = HG_CHUNK
    nv = C // SUBLANES
    e = jnp.exp(-jnp.abs(z))
    log_sig = jnp.minimum(z, 0.0) - jnp.log(1.0 + e)
    t = c1 + log_sig
    log_f = jnp.maximum(c0, t) + jnp.log(1.0 + jnp.exp(-jnp.abs(c0 - t)))
    k = oml * (jnp.where(z >= 0.0, e, 1.0) / (1.0 + e))
    sc_diag = _dot_nt(q.astype(BF16), k.astype(BF16))
    vt = vb.astype(F32).T.astype(BF16)
    yield
    x3 = log_f.reshape(nv, SUBLANES, LANES)
    sub3 = lax.broadcasted_iota(jnp.int32, (nv, SUBLANES, LANES), 1)
    d = 1
    while d < SUBLANES:
        if reverse:
            x3 = x3 + jnp.where(sub3 < SUBLANES - d, pltpu.roll(x3, SUBLANES - d, 1), 0.0)
        else:
            x3 = x3 + jnp.where(sub3 >= d, pltpu.roll(x3, d, 1), 0.0)
        d *= 2
    edge = 0 if reverse else SUBLANES - 1
    tot = jnp.broadcast_to(x3[:, edge:edge + 1, :], (nv, SUBLANES, LANES))
    groups = [None] * nv
    run = None
    for g in (range(nv - 1, -1, -1) if reverse else range(nv)):
        groups[g] = x3[g] if run is None else x3[g] + run
        run = tot[g] if run is None else run + tot[g]
    a3 = jnp.stack(groups)
    a = a3.reshape(C, LANES)
    sub = lax.broadcasted_iota(jnp.int32, (C, LANES), 0) & (SUBLANES - 1)
    yield
    q_in = (q * jnp.exp(a)).astype(BF16)
    a_last = a[0:1, :] if reverse else a[C - 1:C, :]
    k_end = (k * jnp.exp(a_last - a)).astype(BF16)
    st = states[slot]
    o_inter = _dot_nt(q_in, st.astype(BF16))
    states[slot] = st * jnp.exp(a_last) + _dot(vt, k_end)
    yield

    def pick(r):
        return jnp.broadcast_to(a3[:, r:r + 1, :], (nv, SUBLANES, LANES)).reshape(C, LANES)

    level_scores = []
    for b in range(HG_LEVELS):
        h = 1 << b
        m = h if reverse else h - 1
        if h == 1:
            if reverse:
                ref = jnp.where((sub & 1) == 0, pltpu.roll(a, C - 1, 0), a)
            else:
                ref = jnp.where((sub & 1) == 1, pltpu.roll(a, 1, 0), a)
        elif 2 * h < SUBLANES:
            ref = jnp.where(sub < 2 * h, pick(m), pick(m + 2 * h))
        elif 2 * h == SUBLANES:
            ref = pick(m)
        else:
            pieces = [jnp.broadcast_to(a[blk * 2 * h + m:blk * 2 * h + m + 1, :], (2 * h, LANES))
                      for blk in range(C // (2 * h))]
            ref = jnp.concatenate(pieces, axis=0) if len(pieces) > 1 else pieces[0]
        eb = jnp.exp(-jnp.abs(a - ref))
        level_scores.append(_dot_nt((q * eb).astype(BF16), (k * eb).astype(BF16)))
        yield
    scores = jnp.where(code == HG_LEVELS, sc_diag, 0.0)
    for b in range(HG_LEVELS):
        scores = jnp.where(code == b, level_scores[b], scores)
    o = o_inter + _dot(scores.astype(BF16), vb)
    yield
    return o


def _hgrn_kernel(*refs, S):
    C = HG_CHUNK
    n = S // C
    nh = HG_HEADS_PER_STEP
    head_refs = [refs[5 * hh:5 * hh + 5] for hh in range(nh)]
    par_ref, o_ref, acc_ref, code_ref = refs[5 * nh:]

    ti = lax.broadcasted_iota(jnp.int32, (C, C), 0)
    si = lax.broadcasted_iota(jnp.int32, (C, C), 1)
    x = ti ^ si
    lvl = jnp.zeros((C, C), jnp.int32)
    for b in range(1, HG_LEVELS):
        lvl = lvl + jnp.where(x >= (1 << b), 1, 0)
    diag = jnp.where(ti == si, HG_LEVELS, -1)
    code_ref[0] = jnp.where(ti > si, lvl, diag)
    code_ref[1] = jnp.where(ti < si, lvl, diag)

    def chain(ci, hh, d, states):
        q_ref, zf_ref, zb_ref, v_ref, _ = head_refs[hh]
        z_ref = zb_ref if d else zf_ref
        par = par_ref[hh]
        r0 = pl.multiple_of(ci * C, C)
        return _hgrn_chain(q_ref[pl.ds(r0, C), :].astype(F32), v_ref[pl.ds(r0, C), :],
                           z_ref[pl.ds(r0, C), :].astype(F32), par[3 * d:3 * d + 1], par[3 * d + 1:3 * d + 2],
                           par[3 * d + 2:3 * d + 3], code_ref[d], states, 2 * hh + d, d == 1)

    def make_body(accumulate):
        def body(i, carry):
            states = list(carry)
            jobs = []
            for hh in range(nh):
                jobs += [(2 * i, hh, 0), (2 * i + 1, hh, 0), (n - 1 - 2 * i, hh, 1), (n - 2 - 2 * i, hh, 1)]
            outs = _lockstep([chain(ci, hh, d, states) for ci, hh, d in jobs])
            for (ci, hh, _), o in zip(jobs, outs):
                r0 = pl.multiple_of(ci * C, C)
                if accumulate:
                    acc_ref[hh, pl.ds(r0, C), :] += o
                else:
                    acc_ref[hh, pl.ds(r0, C), :] = o
            return tuple(states)
        return body

    zero = jnp.zeros((LANES, HG_DK), F32)
    carry = lax.fori_loop(0, n // 4, make_body(False), (zero,) * (2 * nh))
    lax.fori_loop(n // 4, n // 2, make_body(True), carry)

    ft = 256

    def fin(i, _):
        r0 = pl.multiple_of(i * ft, ft)
        for hh in range(nh):
            o = acc_ref[hh, pl.ds(r0, ft), :]
            y = o * lax.rsqrt(jnp.mean(o * o, axis=-1, keepdims=True) + NORM_EPS) * par_ref[hh][6:7]
            g = head_refs[hh][4][pl.ds(r0, ft), :].astype(F32)
            o_ref[hh, pl.ds(r0, ft), :] = (y * (g * _sigmoid(g))).astype(BF16)
        return 0

    lax.fori_loop(0, S // ft, fin, 0)


def _hgrn(proj, par, B, S):
    nh = HG_HEADS_PER_STEP

    def spec(cb0, hh):
        return pl.BlockSpec((None, S, LANES), lambda b, h: (cb0 + nh * h + hh, b, 0))

    in_specs = []
    for hh in range(nh):
        in_specs += [spec(CB_HQ, hh), spec(CB_HF, hh), spec(CB_HB, hh), spec(CB_HV, hh), spec(CB_HG, hh)]
    in_specs.append(pl.BlockSpec((nh, SUBLANES, LANES), lambda b, h: (h, 0, 0)))
    return pl.pallas_call(
        functools.partial(_hgrn_kernel, S=S),
        grid=(B, HG_HEADS // nh),
        in_specs=in_specs,
        out_specs=pl.BlockSpec((nh, S, LANES), lambda b, h: (h, b, 0)),
        out_shape=jax.ShapeDtypeStruct((HG_HEADS, B * S, LANES), BF16),
        scratch_shapes=[pltpu.VMEM((nh, S, LANES), F32), pltpu.VMEM((2, HG_CHUNK, HG_CHUNK), jnp.int32)],
        compiler_params=_params(("parallel", "parallel")),
        name="hgrn2",
    )(*([proj] * (5 * nh)), par)


def _linear_scan_tile(a, u, carry, reverse):
    n = a.shape[0]
    nv = n // SUBLANES
    a3 = a.reshape(nv, SUBLANES, LANES)
    u3 = u.reshape(nv, SUBLANES, LANES)
    sub = lax.broadcasted_iota(jnp.int32, (nv, SUBLANES, LANES), 1)
    d = 1
    while d < SUBLANES:
        if reverse:
            keep = sub < SUBLANES - d
            shift = SUBLANES - d
        else:
            keep = sub >= d
            shift = d
        a_s = jnp.where(keep, pltpu.roll(a3, shift, 1), 1.0)
        u_s = jnp.where(keep, pltpu.roll(u3, shift, 1), 0.0)
        u3 = a3 * u_s + u3
        a3 = a3 * a_s
        d *= 2
        yield
    edge = 0 if reverse else SUBLANES - 1
    a_tot = jnp.broadcast_to(a3[:, edge:edge + 1, :], a3.shape)
    u_tot = jnp.broadcast_to(u3[:, edge:edge + 1, :], u3.shape)
    groups = [None] * nv
    for g in (range(nv - 1, -1, -1) if reverse else range(nv)):
        groups[g] = u3[g] + a3[g] * carry
        carry = u_tot[g] + a_tot[g] * carry
        if g % 4 == 0:
            yield
    return jnp.stack(groups).reshape(n, LANES), carry


def _rg_kernel(x_ref, gate_ref, par_ref, w_ref, o_ref, xpad_ref, xc_ref, h_ref, *, S):
    TS = RG_TILE
    n = S // TS
    par = par_ref[...]
    conv_b = par[4:5]

    xpad_ref[pl.ds(0, RG_HALO), :] = jnp.zeros((RG_HALO, LANES), F32)
    xpad_ref[pl.ds(S + RG_HALO, RG_HALO), :] = jnp.zeros((RG_HALO, LANES), F32)

    def copy(i, _):
        r0 = pl.multiple_of(i * TS, TS)
        xpad_ref[pl.ds(r0 + RG_HALO, TS), :] = x_ref[pl.ds(r0, TS), :].astype(F32)
        return 0

    lax.fori_loop(0, n, copy, 0)

    def conv(i, _):
        r0 = pl.multiple_of(i * TS, TS)
        xc = conv_b
        for j in range(RG_CONV_W):
            off = RG_HALO - RG_CONV_W // 2 + j
            xc = xc + par[j:j + 1] * xpad_ref[pl.ds(r0 + off, TS), :]
        xc_ref[pl.ds(r0, TS), :] = xc
        return 0

    lax.fori_loop(0, n, conv, 0)

    def tile(i, carries, d, final):
        r0 = pl.multiple_of(i * TS, TS)
        xc = xc_ref[pl.ds(r0, TS), :]
        gts = _dot(xc.astype(BF16), w_ref[:, d * 2 * LANES:(d + 1) * 2 * LANES])
        yield
        r = _sigmoid(gts[:, :LANES] + par[5 + 2 * d:6 + 2 * d])
        ig = _sigmoid(gts[:, LANES:] + par[6 + 2 * d:7 + 2 * d])
        lam = par[9 + d:10 + d]
        softplus_neg_lam = jnp.maximum(-lam, 0.0) + jnp.log1p(jnp.exp(-jnp.abs(lam)))
        log_a = (-RG_C) * r * softplus_neg_lam
        a = jnp.exp(log_a)
        y = 1.0 - a * a
        u = jnp.where(y > 0.0, y * lax.rsqrt(y), 0.0) * (ig * xc)
        yield
        h, carries[d] = yield from _linear_scan_tile(a, u, carries[d], reverse=(d == 1))
        yield
        if final:
            gate = gate_ref[pl.ds(r0, TS), :].astype(F32)
            o_ref[pl.ds(r0, TS), :] = (jax.nn.gelu(gate) * (h_ref[pl.ds(r0, TS), :] + h)).astype(BF16)
        else:
            h_ref[pl.ds(r0, TS), :] = h

    def make_body(final):
        def body(i, carry):
            carries = list(carry)
            _lockstep([tile(i, carries, 0, final), tile(n - 1 - i, carries, 1, final)])
            return tuple(carries)
        return body

    zero = jnp.zeros((SUBLANES, LANES), F32)
    carry = lax.fori_loop(0, n // 2, make_body(False), (zero, zero))
    lax.fori_loop(n // 2, n, make_body(True), carry)


def _rglru(proj, par, w4, B, S):
    ng = RG_WIDTH // LANES

    def spec(cb0):
        return pl.BlockSpec((None, S, LANES), lambda b, g: (cb0 + g, b, 0))

    return pl.pallas_call(
        functools.partial(_rg_kernel, S=S),
        grid=(B, ng),
        in_specs=[spec(CB_RX), spec(CB_RGATE),
                  pl.BlockSpec((None, 2 * SUBLANES, LANES), lambda b, g: (g, 0, 0)),
                  pl.BlockSpec((None, LANES, 4 * LANES), lambda b, g: (g, 0, 0))],
        out_specs=pl.BlockSpec((None, S, LANES), lambda b, g: (g, b, 0)),
        out_shape=jax.ShapeDtypeStruct((ng, B * S, LANES), BF16),
        scratch_shapes=[pltpu.VMEM((S + 2 * RG_HALO, LANES), F32), pltpu.VMEM((S, LANES), F32),
                        pltpu.VMEM((S, LANES), F32)],
        compiler_params=_params(("parallel", "parallel")),
        name="rglru",
    )(proj, proj, par, w4)


def _attn_kernel(q0_ref, k0_ref, v0_ref, q1_ref, k1_ref, v1_ref, q2_ref, k2_ref, v2_ref, par_ref, o_ref,
                 perm_ref, qd_ref, kd_ref, vd_ref, bias_ref, og_ref, lg_ref, *, S):
    refs = ((q0_ref, k0_ref, v0_ref), (q1_ref, k1_ref, v1_ref), (q2_ref, k2_ref, v2_ref))
    ct = ATT_TILE

    for g, (_, dil) in enumerate(ATT_GROUPS):
        q_ref, k_ref, v_ref = refs[g]
        par = par_ref[g]
        qg, kg, slope = par[0:1], par[1:2], par[2:3, 0:1]
        L = S // dil
        Q = min(ATT_QBLK, L)
        KW = min(Q + 2 * ATT_RADIUS, L)
        nq = L // Q
        nq_shift = int(math.log2(nq))

        for var in range(3):
            qi = var * ATT_RADIUS + lax.broadcasted_iota(jnp.int32, (Q, KW), 0)
            rel = jnp.abs(qi - lax.broadcasted_iota(jnp.int32, (Q, KW), 1))
            bias_ref[var, :Q, :KW] = jnp.where(rel <= ATT_RADIUS, -slope * rel.astype(F32), NEG_INF)

        per = ct // dil
        if dil > 1:
            pi = lax.broadcasted_iota(jnp.int32, (ct, ct), 0)
            pj = lax.broadcasted_iota(jnp.int32, (ct, ct), 1)
            src_tok = (pi & (per - 1)) * dil + lax.shift_right_logical(pi, int(math.log2(per)))
            perm_ref[...] = jnp.where(pj == src_tok, 1.0, 0.0).astype(BF16)

        def prep_tile(i, q_ref=q_ref, k_ref=k_ref, v_ref=v_ref, qg=qg, kg=kg, dil=dil, per=per, L=L):
            r0 = pl.multiple_of(i * ct, ct)
            xs = (q_ref[pl.ds(r0, ct), :].astype(F32), k_ref[pl.ds(r0, ct), :].astype(F32))
            sums = [jnp.sum(x * x, axis=-1, keepdims=True) for x in xs]
            yield
            tiles = [(x * lax.rsqrt(ss * (1.0 / LANES) + NORM_EPS) * gain).astype(BF16)
                     for x, ss, gain in zip(xs, sums, (qg, kg))]
            tiles.append(v_ref[pl.ds(r0, ct), :])
            if dil > 1:
                tiles = [_dot(perm_ref[...], x) for x in tiles]
                yield
            for dst_ref, y in zip((qd_ref, kd_ref, vd_ref), tiles):
                if dil == 1:
                    dst_ref[pl.ds(r0, ct), :] = y
                else:
                    y = y.astype(BF16)
                    for r in range(dil):
                        dst_ref[pl.ds(pl.multiple_of(r * L + i * per, 16), per), :] = y[r * per:(r + 1) * per, :]

        prep_unroll = min(4, S // ct)

        def prep(i, _, prep_tile=prep_tile, prep_unroll=prep_unroll):
            _lockstep([prep_tile(i * prep_unroll + u) for u in range(prep_unroll)])
            return 0

        lax.fori_loop(0, S // ct // prep_unroll, prep, 0)

        def rows(start, size, dil=dil):
            if dil == 1:
                return pl.ds(start, size)
            return pl.ds(start, size, stride=dil)

        unroll = min(ATT_UNROLL, dil * nq)

        def qblock(it, g=g, dil=dil, L=L, Q=Q, KW=KW, nq=nq, nq_shift=nq_shift, rows=rows):
            r = lax.shift_right_logical(it, nq_shift)
            m0 = (it & (nq - 1)) * Q
            ks = jnp.clip(m0 - ATT_RADIUS, 0, L - KW)
            var = lax.shift_right_logical(m0 - ks, int(math.log2(ATT_RADIUS)))
            base = r * L
            qn = qd_ref[pl.ds(pl.multiple_of(base + m0, 16), Q), :]
            kk = kd_ref[pl.ds(pl.multiple_of(base + ks, 16), KW), :]
            s = _dot_nt(qn, kk)
            yield
            s = s + bias_ref[var, :Q, :KW]
            m = jnp.max(s, axis=-1, keepdims=True)
            yield
            pb = jnp.exp(s - m).astype(BF16)
            vv = vd_ref[pl.ds(pl.multiple_of(base + ks, 16), KW), :]
            ol = _dot(pb, jnp.concatenate([vv, jnp.ones((KW, LANES), BF16)], axis=1))
            yield
            l = ol[:, LANES:]
            og_ref[g, rows(m0 * dil + r, Q), :] = ol[:, :LANES] / l
            lg_ref[g, rows(m0 * dil + r, Q), :] = m + jnp.log(l)

        def qblocks(i, _, unroll=unroll, qblock=qblock):
            _lockstep([qblock(i * unroll + u) for u in range(unroll)])
            return 0

        lax.fori_loop(0, (dil * nq) // unroll, qblocks, 0)

    def merge(i, _):
        r0 = pl.multiple_of(i * ct, ct)
        l0, l1, l2 = (lg_ref[g, pl.ds(r0, ct), :] for g in range(3))
        m = jnp.maximum(jnp.maximum(l0, l1), l2)
        e0, e1, e2 = jnp.exp(l0 - m), jnp.exp(l1 - m), jnp.exp(l2 - m)
        o = (e0 * og_ref[0, pl.ds(r0, ct), :] + e1 * og_ref[1, pl.ds(r0, ct), :]
             + e2 * og_ref[2, pl.ds(r0, ct), :]) / (e0 + e1 + e2)
        o_ref[pl.ds(r0, ct), :] = o.astype(BF16)
        return 0

    lax.fori_loop(0, S // ct, merge, 0)


def _attention(proj, par, B, S):
    H = ATT_HEADS_PER_GROUP
    ngroups = len(ATT_GROUPS)

    def spec(cb0, g):
        return pl.BlockSpec((None, S, LANES), lambda b, j: (cb0 + g * H + j, b, 0))

    in_specs = []
    for g in range(ngroups):
        in_specs += [spec(CB_AQ, g), spec(CB_AK, g), spec(CB_AV, g)]
    in_specs.append(pl.BlockSpec((None, ngroups, SUBLANES, LANES), lambda b, j: (j, 0, 0, 0)))
    return pl.pallas_call(
        functools.partial(_attn_kernel, S=S),
        grid=(B, H),
        in_specs=in_specs,
        out_specs=pl.BlockSpec((None, S, LANES), lambda b, j: (j, b, 0)),
        out_shape=jax.ShapeDtypeStruct((H, B * S, LANES), BF16),
        scratch_shapes=[pltpu.VMEM((ATT_TILE, ATT_TILE), BF16),
                        pltpu.VMEM((S, LANES), BF16), pltpu.VMEM((S, LANES), BF16), pltpu.VMEM((S, LANES), BF16),
                        pltpu.VMEM((3, ATT_QBLK, ATT_QBLK + 2 * ATT_RADIUS), F32),
                        pltpu.VMEM((ngroups, S, LANES), F32), pltpu.VMEM((ngroups, S, LANES), F32)],
        compiler_params=_params(("parallel", "parallel")),
        name="attention",
    )(*([proj] * (3 * ngroups)), par)


def _cat_heads(ref):
    return jnp.concatenate([ref[c] for c in range(ref.shape[0])], axis=-1)


def _merge_kernel(x_ref, hg_ref, rg_ref, att_ref, ga_ref, gb_ref, gc_ref, wb_ref, wo_ref, ng_ref,
                  x1_ref, h2_ref):
    merged = _sigmoid(_cat_heads(ga_ref).astype(F32)) * _dot(_cat_heads(hg_ref), wb_ref[0])
    merged += _sigmoid(_cat_heads(gb_ref).astype(F32)) * _dot(_cat_heads(rg_ref), wb_ref[1])
    merged += _sigmoid(_cat_heads(gc_ref).astype(F32)) * _dot(_cat_heads(att_ref), wb_ref[2])
    x1 = x_ref[...] + _dot(merged.astype(BF16), wo_ref[...])
    x1_ref[...] = x1
    ms = jnp.mean(x1 * x1, axis=-1, keepdims=True)
    h2_ref[...] = (x1 * lax.rsqrt(ms + NORM_EPS) * ng_ref[...]).astype(BF16)


def _merge(x, proj, hg, rg, att, wb, wo, ng):
    T = x.shape[0]
    tm = MERGE_TM
    nb = BRANCH_W // LANES
    ngate = D_MODEL // LANES

    branch = pl.BlockSpec((nb, tm, LANES), lambda i: (0, i, 0))

    def gate(n):
        return pl.BlockSpec((ngate, tm, LANES), lambda i: (CB_GATES // ngate + n, i, 0))

    row = pl.BlockSpec((tm, D_MODEL), lambda i: (i, 0))
    return pl.pallas_call(
        _merge_kernel,
        grid=(T // tm,),
        in_specs=[row, branch, branch, branch, gate(0), gate(1), gate(2),
                  pl.BlockSpec((3, BRANCH_W, D_MODEL), lambda i: (0, 0, 0)),
                  pl.BlockSpec((D_MODEL, D_MODEL), lambda i: (0, 0)),
                  pl.BlockSpec((1, D_MODEL), lambda i: (0, 0))],
        out_specs=[row, row],
        out_shape=[jax.ShapeDtypeStruct((T, D_MODEL), F32), jax.ShapeDtypeStruct((T, D_MODEL), BF16)],
        compiler_params=_params(("parallel",)),
        name="merge",
    )(x, hg, rg, att, proj, proj, proj, wb, wo, ng)


def _ffn_kernel(h_ref, x_ref, w1_ref, w3_ref, w2_ref, o_ref, acc_ref):
    f = pl.program_id(1)
    h = h_ref[...]
    h1 = _dot(h, w1_ref[...])
    y = _dot((h1 * _sigmoid(h1) * _dot(h, w3_ref[...])).astype(BF16), w2_ref[...])

    @pl.when(f == 0)
    def _():
        acc_ref[...] = x_ref[...] + y

    @pl.when(f > 0)
    def _():
        acc_ref[...] += y

    @pl.when(f == pl.num_programs(1) - 1)
    def _():
        o_ref[...] = acc_ref[...]


def _ffn_dense(h2, x1, w1, w3, w2):
    T = h2.shape[0]
    tm, tf = FFN_TM, FFN_TF
    return pl.pallas_call(
        _ffn_kernel,
        grid=(T // tm, FFN_DENSE // tf),
        in_specs=[pl.BlockSpec((tm, D_MODEL), lambda i, f: (i, 0)),
                  pl.BlockSpec((tm, D_MODEL), lambda i, f: (i, 0)),
                  pl.BlockSpec((D_MODEL, tf), lambda i, f: (0, f)),
                  pl.BlockSpec((D_MODEL, tf), lambda i, f: (0, f)),
                  pl.BlockSpec((tf, D_MODEL), lambda i, f: (f, 0))],
        out_specs=pl.BlockSpec((tm, D_MODEL), lambda i, f: (i, 0)),
        out_shape=jax.ShapeDtypeStruct((T, D_MODEL), F32),
        scratch_shapes=[pltpu.VMEM((tm, D_MODEL), F32)],
        compiler_params=_params(("parallel", "arbitrary")),
        name="ffn_dense",
    )(h2, x1, w1, w3, w2)


def _router_kernel(h_ref, wr_ref, o_ref, cnt_ref, base_ref):
    tm = ROUTER_TM

    @pl.when(pl.program_id(0) == 0)
    def _():
        base_ref[...] = jnp.zeros_like(base_ref)

    lane = lax.broadcasted_iota(jnp.int32, (tm, LANES), 1)
    logits = jnp.where(lane < N_EXPERTS, _dot(h_ref[...], wr_ref[...]), -jnp.inf)
    m1 = jnp.max(logits, axis=-1, keepdims=True)
    i1 = jnp.min(jnp.where(logits == m1, lane, LANES), axis=-1, keepdims=True)
    rest = jnp.where(lane == i1, -jnp.inf, logits)
    m2 = jnp.max(rest, axis=-1, keepdims=True)
    i2 = jnp.min(jnp.where(rest == m2, lane, LANES), axis=-1, keepdims=True)
    e21 = jnp.exp(m2 - m1)
    g1 = 1.0 / (1.0 + e21)
    g2 = e21 / (1.0 + e21)

    onehot = jnp.where(lane == i1, 1.0, jnp.where(lane == i2, 1.0, 0.0))
    ti = lax.broadcasted_iota(jnp.int32, (tm, tm), 0)
    si = lax.broadcasted_iota(jnp.int32, (tm, tm), 1)
    before = jnp.where(si < ti, 1.0, 0.0).astype(BF16)
    pos = _dot(before, onehot.astype(BF16)) + base_ref[0:1, :]
    r1 = jnp.sum(jnp.where(lane == i1, pos, 0.0), axis=-1, keepdims=True)
    r2 = jnp.sum(jnp.where(lane == i2, pos, 0.0), axis=-1, keepdims=True)
    total = base_ref[0:1, :] + jnp.sum(onehot, axis=0, keepdims=True)
    base_ref[...] = jnp.broadcast_to(total, base_ref.shape)
    cnt_ref[...] = jnp.broadcast_to(total, cnt_ref.shape)

    out = jnp.where(lane == 0, i1.astype(F32), 0.0)
    out = jnp.where(lane == 1, i2.astype(F32), out)
    out = jnp.where(lane == 2, g1, out)
    out = jnp.where(lane == 3, g2, out)
    out = jnp.where(lane == 4, r1, out)
    out = jnp.where(lane == 5, r2, out)
    o_ref[...] = out


def _router(h2, wr):
    T = h2.shape[0]
    tm = ROUTER_TM
    return pl.pallas_call(
        _router_kernel,
        grid=(T // tm,),
        in_specs=[pl.BlockSpec((tm, D_MODEL), lambda i: (i, 0)),
                  pl.BlockSpec((D_MODEL, LANES), lambda i: (0, 0))],
        out_specs=[pl.BlockSpec((tm, LANES), lambda i: (i, 0)),
                   pl.BlockSpec((SUBLANES, LANES), lambda i: (0, 0))],
        out_shape=[jax.ShapeDtypeStruct((T, LANES), F32), jax.ShapeDtypeStruct((SUBLANES, LANES), F32)],
        scratch_shapes=[pltpu.VMEM((SUBLANES, LANES), F32)],
        compiler_params=_params(("arbitrary",)),
        name="moe_router",
    )(h2, wr)


def _expert_kernel(be_ref, nu_ref, x_ref, w1_ref, w3_ref, w2_ref, o_ref, acc_ref):
    i = pl.program_id(0)
    f = pl.program_id(1)

    @pl.when(f == 0)
    def _():
        acc_ref[...] = jnp.zeros_like(acc_ref)

    @pl.when(i < nu_ref[0])
    def _():
        x = x_ref[...]
        h1 = _dot(x, w1_ref[...])
        acc_ref[...] += _dot((h1 * _sigmoid(h1) * _dot(x, w3_ref[...])).astype(BF16), w2_ref[...])

    @pl.when(f == pl.num_programs(1) - 1)
    def _():
        o_ref[...] = acc_ref[...].astype(BF16)


def _experts(rows, block_expert, n_used, w1, w3, w2):
    n_rows = rows.shape[0]
    R, tf = MOE_ROWS, MOE_TF
    grid_spec = pltpu.PrefetchScalarGridSpec(
        num_scalar_prefetch=2,
        grid=(n_rows // R, FFN_EXPERT // tf),
        in_specs=[pl.BlockSpec((R, D_MODEL), lambda i, f, be, nu: (i, 0)),
                  pl.BlockSpec((None, D_MODEL, tf), lambda i, f, be, nu: (be[i], 0, f)),
                  pl.BlockSpec((None, D_MODEL, tf), lambda i, f, be, nu: (be[i], 0, f)),
                  pl.BlockSpec((None, tf, D_MODEL), lambda i, f, be, nu: (be[i], f, 0))],
        out_specs=pl.BlockSpec((R, D_MODEL), lambda i, f, be, nu: (i, 0)),
        scratch_shapes=[pltpu.VMEM((R, D_MODEL), F32)],
    )
    return pl.pallas_call(
        _expert_kernel,
        grid_spec=grid_spec,
        out_shape=jax.ShapeDtypeStruct((n_rows, D_MODEL), BF16),
        compiler_params=_params(("parallel", "arbitrary")),
        name="moe_experts",
    )(block_expert, n_used, rows, w1, w3, w2)


def _combine_kernel(x_ref, y_ref, info_ref, o_ref):
    info = info_ref[...]
    o_ref[...] = (x_ref[...] + info[:, 2:3] * y_ref[:, :D_MODEL].astype(F32)
                  + info[:, 3:4] * y_ref[:, D_MODEL:].astype(F32))


def _combine(x1, y12, info):
    T = x1.shape[0]
    tm = COMBINE_TM
    row = pl.BlockSpec((tm, D_MODEL), lambda i: (i, 0))
    return pl.pallas_call(
        _combine_kernel,
        grid=(T // tm,),
        in_specs=[row, pl.BlockSpec((tm, TOP_K * D_MODEL), lambda i: (i, 0)),
                  pl.BlockSpec((tm, LANES), lambda i: (i, 0))],
        out_specs=row,
        out_shape=jax.ShapeDtypeStruct((T, D_MODEL), F32),
        compiler_params=_params(("parallel",)),
        name="moe_combine",
    )(x1, y12, info)


def _moe(h2, x1, wr, w1, w3, w2):
    T = h2.shape[0]
    R = MOE_ROWS
    n_blocks = -(-(T * TOP_K) // R) + N_EXPERTS
    info, cnt = _router(h2, wr)
    expert = info[:, 0:2].astype(jnp.int32)
    rank = info[:, 4:6].astype(jnp.int32)
    counts = cnt[0, :N_EXPERTS].astype(jnp.int32)
    blocks_per = (counts + R - 1) // R
    bend = jnp.cumsum(blocks_per)
    pstart = (bend - blocks_per) * R
    dest = pstart[expert] + rank
    n_used = bend[-1:]
    block_expert = jnp.minimum(
        jnp.searchsorted(bend, jnp.arange(n_blocks, dtype=jnp.int32), side='right'),
        N_EXPERTS - 1).astype(jnp.int32)
    token = jnp.broadcast_to(jnp.arange(T, dtype=jnp.int32)[:, None], (T, TOP_K))
    tok_of_row = jnp.zeros((n_blocks * R,), jnp.int32).at[dest.reshape(-1)].set(token.reshape(-1))
    rows = h2[tok_of_row]
    y = _experts(rows, block_expert, n_used, w1, w3, w2)
    y12 = y[dest.reshape(-1)].reshape(T, TOP_K * D_MODEL)
    return _combine(x1, y12, info)


def _pad_rows(rows, n):
    rows = jnp.stack(rows, axis=-2)
    pad = [(0, 0)] * (rows.ndim - 2) + [(0, n - rows.shape[-2]), (0, 0)]
    return jnp.pad(rows, pad)


def _hgrn_params(lb, norm_g):
    lb = lb.reshape(2, HG_HEADS, HG_DK)
    rows = []
    for d in range(2):
        rows += [jnp.log(jnp.maximum(lb[d], LB_EPS)), jnp.log1p(-lb[d]), 1.0 - lb[d]]
    rows.append(norm_g.reshape(HG_HEADS, HG_DK))
    return _pad_rows(rows, SUBLANES)


def _rg_params(conv_w, conv_b, b_a, b_x, lam):
    ng = RG_WIDTH // LANES
    g = lambda t: t.reshape(ng, LANES)
    rows = [g(conv_w[j]) for j in range(RG_CONV_W)] + [g(conv_b)]
    rows += [g(b_a[0]), g(b_x[0]), g(b_a[1]), g(b_x[1]), g(lam[0]), g(lam[1])]
    return _pad_rows(rows, 2 * SUBLANES)


def _rg_gate_weights(w_a, w_x):
    ng = RG_WIDTH // LANES
    per = RG_BLOCKS // ng

    def dense(w):
        w = w.reshape(ng, per, RG_BLOCK_W, RG_BLOCK_W)
        eye = jnp.eye(per, dtype=w.dtype)
        return jnp.einsum('gpcd,pq->gpcqd', w, eye).reshape(ng, LANES, LANES)

    return jnp.concatenate([dense(w_a[0]), dense(w_x[0]), dense(w_a[1]), dense(w_x[1])], axis=-1).astype(BF16)


def _attn_params(q_g, k_g):
    H = ATT_HEADS_PER_GROUP
    groups = []
    for gi, (_, dil) in enumerate(ATT_GROUPS):
        heads = jnp.arange(gi * H + 1, (gi + 1) * H + 1, dtype=F32)
        slopes = 2.0 ** (-8.0 * heads / ATT_HEADS) * dil
        rows = [jnp.broadcast_to(q_g * (ATT_DH ** -0.5), (H, LANES)), jnp.broadcast_to(k_g, (H, LANES)),
                jnp.broadcast_to(slopes[:, None], (H, LANES))]
        groups.append(_pad_rows(rows, SUBLANES))
    return jnp.stack(groups, axis=1)


def _trunk(x3, p):
    B, S, _ = x3.shape
    x = x3.reshape(B * S, D_MODEL)
    sm = jax.nn.softmax(p['hg_lb_logits'].astype(F32), axis=1)
    lower_bounds = jnp.cumsum(sm, axis=1) - sm[:, :1]
    for l in range(DEPTH):
        proj = _inproj(x, p['norm_mix_g'][l][None], p['w_in'][l].astype(BF16))
        hg = _hgrn(proj, _hgrn_params(lower_bounds[:, l], p['hg_norm_g'][l]), B, S)
        rg = _rglru(proj,
                    _rg_params(p['rg_conv_w'][l], p['rg_conv_b'][l], p['rg_b_a'][l], p['rg_b_x'][l],
                               p['rg_lambda'][l]),
                    _rg_gate_weights(p['rg_w_a'][l], p['rg_w_x'][l]), B, S)
        att = _attention(proj, _attn_params(p['attn_q_g'][l], p['attn_k_g'][l]), B, S)
        x1, h2 = _merge(x, proj, hg, rg, att, p['w_branch'][l].astype(BF16), p['w_out'][l].astype(BF16),
                        p['norm_ffn_g'][l][None])
        j = l // 2
        if l % 2 == 0:
            x = _ffn_dense(h2, x1, p['ffn_w1'][j].astype(BF16), p['ffn_w3'][j].astype(BF16),
                           p['ffn_w2'][j].astype(BF16))
        else:
            wr = jnp.pad(p['moe_router'][j], ((0, 0), (0, LANES - N_EXPERTS))).astype(BF16)
            x = _moe(h2, x1, wr, p['moe_w1'][j].astype(BF16), p['moe_w3'][j].astype(BF16),
                     p['moe_w2'][j].astype(BF16))
    return x.reshape(B, S, D_MODEL)


def kernel(x_prompt, x_sample, norm_mix_g, w_in, hg_lb_logits, hg_norm_g, rg_conv_w, rg_conv_b, rg_w_a, rg_b_a,
           rg_w_x, rg_b_x, rg_lambda, attn_q_g, attn_k_g, w_branch, w_out, norm_ffn_g, ffn_w1, ffn_w3, ffn_w2,
           moe_router, moe_w1, moe_w3, moe_w2):
    p = dict(norm_mix_g=norm_mix_g, w_in=w_in, hg_lb_logits=hg_lb_logits, hg_norm_g=hg_norm_g,
             rg_conv_w=rg_conv_w, rg_conv_b=rg_conv_b, rg_w_a=rg_w_a, rg_b_a=rg_b_a, rg_w_x=rg_w_x,
             rg_b_x=rg_b_x, rg_lambda=rg_lambda, attn_q_g=attn_q_g, attn_k_g=attn_k_g,
             w_branch=w_branch, w_out=w_out, norm_ffn_g=norm_ffn_g, ffn_w1=ffn_w1, ffn_w3=ffn_w3,
             ffn_w2=ffn_w2, moe_router=moe_router, moe_w1=moe_w1, moe_w3=moe_w3, moe_w2=moe_w2)
    return (_trunk(x_prompt, p), _trunk(x_sample, p))
```

```python
import functools
import math

import jax
import jax.numpy as jnp
from jax import lax
from jax.experimental import pallas as pl
from jax.experimental.pallas import tpu as pltpu

F32 = jnp.float32
BF16 = jnp.bfloat16

LANES = 128
SUBLANES = 8
VMEM_LIMIT_BYTES = 56 * 1024 * 1024

D_MODEL = 1024
DEPTH = 2
BRANCH_W = 512
HG_HEADS = 4
HG_DK = 128
HG_CHUNK = 64
HG_HEADS_PER_STEP = 2
HG_LEVELS = 6
LB_EPS = 1e-20
RG_WIDTH = 512
RG_BLOCKS = 8
RG_BLOCK_W = RG_WIDTH // RG_BLOCKS
RG_CONV_W = 4
RG_C = 8.0
RG_TILE = 256
RG_HALO = 8
ATT_GROUPS = ((128, 1), (512, 4), (2048, 16))
ATT_HEADS_PER_GROUP = 4
ATT_HEADS = ATT_HEADS_PER_GROUP * len(ATT_GROUPS)
ATT_DH = 128
ATT_RADIUS = 64
ATT_QBLK = 128
ATT_UNROLL = 8
ATT_TILE = 256
FFN_DENSE = 2816
N_EXPERTS = 8
TOP_K = 2
FFN_EXPERT = 3584
NORM_EPS = 1e-6
NEG_INF = -1e30
LOG2_E = 1.4426950408889634

CB_HQ, CB_HF, CB_HB, CB_HV, CB_HG = 0, 4, 8, 12, 16
CB_RX, CB_RGATE = 20, 24
CB_AQ, CB_AK, CB_AV = 28, 40, 52
CB_GATES = 64
IN_COLS = 11264
N_CB = IN_COLS // LANES

INPROJ_TM, INPROJ_TN = 1024, 1024
MERGE_TM = 512
FFN_TM, FFN_TF = 512, 1408
MOE_ROWS, MOE_TF = 1024, 896
ROUTER_TM = 512
COMBINE_TM = 512


def _params(sem):
    return pltpu.CompilerParams(dimension_semantics=sem, vmem_limit_bytes=VMEM_LIMIT_BYTES)


def _dot(a, b):
    return jnp.dot(a, b, preferred_element_type=F32)


def _dot_nt(a, b):
    return lax.dot_general(a, b, (((1,), (1,)), ((), ())), preferred_element_type=F32)


def _sigmoid(x):
    return jax.nn.sigmoid(x)


def _lockstep(gens):
    results = [None] * len(gens)
    active = list(range(len(gens)))
    while active:
        for idx in list(active):
            try:
                next(gens[idx])
            except StopIteration as stop:
                results[idx] = stop.value
                active.remove(idx)
    return results


def _inproj_kernel(x_ref, g_ref, w_ref, o_ref, h_ref):
    @pl.when(pl.program_id(1) == 0)
    def _():
        x = x_ref[...]
        ms = jnp.mean(x * x, axis=-1, keepdims=True)
        h_ref[...] = (x * lax.rsqrt(ms + NORM_EPS) * g_ref[...]).astype(BF16)

    acc = _dot(h_ref[...], w_ref[...])
    for c in range(INPROJ_TN // LANES):
        o_ref[c] = acc[:, c * LANES:(c + 1) * LANES].astype(BF16)


def _inproj(x, g, w):
    T = x.shape[0]
    return pl.pallas_call(
        _inproj_kernel,
        grid=(T // INPROJ_TM, IN_COLS // INPROJ_TN),
        in_specs=[
            pl.BlockSpec((INPROJ_TM, D_MODEL), lambda i, j: (i, 0)),
            pl.BlockSpec((1, D_MODEL), lambda i, j: (0, 0)),
            pl.BlockSpec((D_MODEL, INPROJ_TN), lambda i, j: (0, j)),
        ],
        out_specs=pl.BlockSpec((INPROJ_TN // LANES, INPROJ_TM, LANES), lambda i, j: (j, i, 0)),
        out_shape=jax.ShapeDtypeStruct((N_CB, T, LANES), BF16),
        scratch_shapes=[pltpu.VMEM((INPROJ_TM, D_MODEL), BF16)],
        compiler_params=_params(("parallel", "arbitrary")),
        name="inproj",
    )(x, g, w)


def _hgrn_chain(q, vb, z, c0, c1, oml, code, sign_ref, states, slot, reverse):
    C = HG_CHUNK
    nv = C // SUBLANES
    e = jnp.exp(-jnp.abs(z))
    log_sig = jnp.minimum(z, 0.0) - jnp.log(1.0 + e)
    t = c1 + log_sig
    log_f = jnp.maximum(c0, t) + jnp.log(1.0 + jnp.exp(-jnp.abs(c0 - t)))
    k = oml * (jnp.where(z >= 0.0, e, 1.0) / (1.0 + e))
    sc_diag = _dot_nt(q.astype(BF16), k.astype(BF16))
    vt = vb.astype(F32).T.astype(BF16)
    yield
    x3 = log_f.reshape(nv, SUBLANES, LANES)
    sub3 = lax.broadcasted_iota(jnp.int32, (nv, SUBLANES, LANES), 1)
    d = 1
    while d < SUBLANES:
        if reverse:
            x3 = x3 + jnp.where(sub3 < SUBLANES - d, pltpu.roll(x3, SUBLANES - d, 1), 0.0)
        else:
            x3 = x3 + jnp.where(sub3 >= d, pltpu.roll(x3, d, 1), 0.0)
        d *= 2
    edge = 0 if reverse else SUBLANES - 1
    tot = jnp.broadcast_to(x3[:, edge:edge + 1, :], (nv, SUBLANES, LANES))
    groups = [None] * nv
    run = None
    for g in (range(nv - 1, -1, -1) if reverse else range(nv)):
        groups[g] = x3[g] if run is None else x3[g] + run
        run = tot[g] if run is None else run + tot[g]
    a3 = jnp.stack(groups) * LOG2_E
    a = a3.reshape(C, LANES)
    sub = lax.broadcasted_iota(jnp.int32, (C, LANES), 0) & (SUBLANES - 1)
    yield
    q_in = (q * jnp.exp2(a)).astype(BF16)
    a_last = a[0:1, :] if reverse else a[C - 1:C, :]
    k_end = (k * jnp.exp2(a_last - a)).astype(BF16)
    st = states[slot]
    o_inter = _dot_nt(q_in, st.astype(BF16))
    states[slot] = st * jnp.exp2(a_last) + _dot(vt, k_end)
    yield

    def pick(r):
        return jnp.broadcast_to(a3[:, r:r + 1, :], (nv, SUBLANES, LANES)).reshape(C, LANES)

    level_scores = []
    for b in range(HG_LEVELS):
        h = 1 << b
        m = h if reverse else h - 1
        if h == 1:
            if reverse:
                ref = jnp.where((sub & 1) == 0, pltpu.roll(a, C - 1, 0), a)
            else:
                ref = jnp.where((sub & 1) == 1, pltpu.roll(a, 1, 0), a)
        elif 2 * h < SUBLANES:
            ref = jnp.where(sub < 2 * h, pick(m), pick(m + 2 * h))
        elif 2 * h == SUBLANES:
            ref = pick(m)
        else:
            pieces = [jnp.broadcast_to(a[blk * 2 * h + m:blk * 2 * h + m + 1, :], (2 * h, LANES))
                      for blk in range(C // (2 * h))]
            ref = jnp.concatenate(pieces, axis=0) if len(pieces) > 1 else pieces[0]
        eb = jnp.exp2(((ref - a) if reverse else (a - ref)) * sign_ref[b])
        level_scores.append(_dot_nt((q * eb).astype(BF16), (k * eb).astype(BF16)))
        yield
    scores = jnp.where(code == HG_LEVELS, sc_diag, 0.0)
    for b in range(HG_LEVELS):
        scores = jnp.where(code == b, level_scores[b], scores)
    o = o_inter + _dot(scores.astype(BF16), vb)
    yield
    return o


def _hgrn_kernel(*refs, S):
    C = HG_CHUNK
    n = S // C
    nh = HG_HEADS_PER_STEP
    head_refs = [refs[5 * hh:5 * hh + 5] for hh in range(nh)]
    par_ref, o_ref, acc_ref, code_ref, sign_ref = refs[5 * nh:]

    ti = lax.broadcasted_iota(jnp.int32, (C, C), 0)
    si = lax.broadcasted_iota(jnp.int32, (C, C), 1)
    x = ti ^ si
    lvl = jnp.zeros((C, C), jnp.int32)
    for b in range(1, HG_LEVELS):
        lvl = lvl + jnp.where(x >= (1 << b), 1, 0)
    diag = jnp.where(ti == si, HG_LEVELS, -1)
    code_ref[0] = jnp.where(ti > si, lvl, diag)
    code_ref[1] = jnp.where(ti < si, lvl, diag)
    row = lax.broadcasted_iota(jnp.int32, (C, LANES), 0)
    for b in range(HG_LEVELS):
        sign_ref[b] = jnp.where((row & (1 << b)) != 0, 1.0, -1.0)

    def chain(ci, hh, d, states):
        q_ref, zf_ref, zb_ref, v_ref, _ = head_refs[hh]
        z_ref = zb_ref if d else zf_ref
        par = par_ref[hh]
        r0 = pl.multiple_of(ci * C, C)
        return _hgrn_chain(q_ref[pl.ds(r0, C), :].astype(F32), v_ref[pl.ds(r0, C), :],
                           z_ref[pl.ds(r0, C), :].astype(F32), par[3 * d:3 * d + 1], par[3 * d + 1:3 * d + 2],
                           par[3 * d + 2:3 * d + 3], code_ref[d], sign_ref, states, 2 * hh + d, d == 1)

    def make_body(accumulate):
        def body(i, carry):
            states = list(carry)
            jobs = []
            for hh in range(nh):
                jobs += [(2 * i, hh, 0), (2 * i + 1, hh, 0), (n - 1 - 2 * i, hh, 1), (n - 2 - 2 * i, hh, 1)]
            outs = _lockstep([chain(ci, hh, d, states) for ci, hh, d in jobs])
            for (ci, hh, _), o in zip(jobs, outs):
                r0 = pl.multiple_of(ci * C, C)
                if accumulate:
                    acc_ref[hh, pl.ds(r0, C), :] += o
                else:
                    acc_ref[hh, pl.ds(r0, C), :] = o
            return tuple(states)
        return body

    zero = jnp.zeros((LANES, HG_DK), F32)
    carry = lax.fori_loop(0, n // 4, make_body(False), (zero,) * (2 * nh))
    lax.fori_loop(n // 4, n // 2, make_body(True), carry)

    ft = 256

    def fin(i, _):
        r0 = pl.multiple_of(i * ft, ft)
        for hh in range(nh):
            o = acc_ref[hh, pl.ds(r0, ft), :]
            y = o * lax.rsqrt(jnp.mean(o * o, axis=-1, keepdims=True) + NORM_EPS) * par_ref[hh][6:7]
            g = head_refs[hh][4][pl.ds(r0, ft), :].astype(F32)
            o_ref[hh, pl.ds(r0, ft), :] = (y * (g * _sigmoid(g))).astype(BF16)
        return 0

    lax.fori_loop(0, S // ft, fin, 0)


def _hgrn(proj, par, B, S):
    nh = HG_HEADS_PER_STEP

    def spec(cb0, hh):
        return pl.BlockSpec((None, S, LANES), lambda b, h: (cb0 + nh * h + hh, b, 0))

    in_specs = []
    for hh in range(nh):
        in_specs += [spec(CB_HQ, hh), spec(CB_HF, hh), spec(CB_HB, hh), spec(CB_HV, hh), spec(CB_HG, hh)]
    in_specs.append(pl.BlockSpec((nh, SUBLANES, LANES), lambda b, h: (h, 0, 0)))
    return pl.pallas_call(
        functools.partial(_hgrn_kernel, S=S),
        grid=(B, HG_HEADS // nh),
        in_specs=in_specs,
        out_specs=pl.BlockSpec((nh, S, LANES), lambda b, h: (h, b, 0)),
        out_shape=jax.ShapeDtypeStruct((HG_HEADS, B * S, LANES), BF16),
        scratch_shapes=[pltpu.VMEM((nh, S, LANES), F32), pltpu.VMEM((2, HG_CHUNK, HG_CHUNK), jnp.int32),
                        pltpu.VMEM((HG_LEVELS, HG_CHUNK, LANES), F32)],
        compiler_params=_params(("parallel", "parallel")),
        name="hgrn2",
    )(*([proj] * (5 * nh)), par)


def _linear_scan_tile(a, u, carry, reverse):
    n = a.shape[0]
    nv = n // SUBLANES
    a3 = a.reshape(nv, SUBLANES, LANES)
    u3 = u.reshape(nv, SUBLANES, LANES)
    sub = lax.broadcasted_iota(jnp.int32, (nv, SUBLANES, LANES), 1)
    d = 1
    while d < SUBLANES:
        if reverse:
            keep = sub < SUBLANES - d
            shift = SUBLANES - d
        else:
            keep = sub >= d
            shift = d
        a_s = jnp.where(keep, pltpu.roll(a3, shift, 1), 1.0)
        u_s = jnp.where(keep, pltpu.roll(u3, shift, 1), 0.0)
        u3 = a3 * u_s + u3
        a3 = a3 * a_s
        d *= 2
        yield
    edge = 0 if reverse else SUBLANES - 1
    a_tot = jnp.broadcast_to(a3[:, edge:edge + 1, :], a3.shape)
    u_tot = jnp.broadcast_to(u3[:, edge:edge + 1, :], u3.shape)
    groups = [None] * nv
    for g in (range(nv - 1, -1, -1) if reverse else range(nv)):
        groups[g] = u3[g] + a3[g] * carry
        carry = u_tot[g] + a_tot[g] * carry
        if g % 4 == 0:
            yield
    return jnp.stack(groups).reshape(n, LANES), carry


def _rg_kernel(x_ref, gate_ref, par_ref, w_ref, o_ref, xpad_ref, xc_ref, h_ref, *, S):
    TS = RG_TILE
    n = S // TS
    par = par_ref[...]
    conv_b = par[4:5]

    xpad_ref[pl.ds(0, RG_HALO), :] = jnp.zeros((RG_HALO, LANES), F32)
    xpad_ref[pl.ds(S + RG_HALO, RG_HALO), :] = jnp.zeros((RG_HALO, LANES), F32)

    def copy(i, _):
        r0 = pl.multiple_of(i * TS, TS)
        xpad_ref[pl.ds(r0 + RG_HALO, TS), :] = x_ref[pl.ds(r0, TS), :].astype(F32)
        return 0

    lax.fori_loop(0, n, copy, 0)

    def conv(i, _):
        r0 = pl.multiple_of(i * TS, TS)
        xc = conv_b
        for j in range(RG_CONV_W):
            off = RG_HALO - RG_CONV_W // 2 + j
            xc = xc + par[j:j + 1] * xpad_ref[pl.ds(r0 + off, TS), :]
        xc_ref[pl.ds(r0, TS), :] = xc
        return 0

    lax.fori_loop(0, n, conv, 0)

    def tile(i, carries, d, final):
        r0 = pl.multiple_of(i * TS, TS)
        xc = xc_ref[pl.ds(r0, TS), :]
        gts = _dot(xc.astype(BF16), w_ref[:, d * 2 * LANES:(d + 1) * 2 * LANES])
        yield
        r = _sigmoid(gts[:, :LANES] + par[5 + 2 * d:6 + 2 * d])
        ig = _sigmoid(gts[:, LANES:] + par[6 + 2 * d:7 + 2 * d])
        lam = par[9 + d:10 + d]
        softplus_neg_lam = jnp.maximum(-lam, 0.0) + jnp.log1p(jnp.exp(-jnp.abs(lam)))
        log_a = (-RG_C) * r * softplus_neg_lam
        a = jnp.exp(log_a)
        y = 1.0 - a * a
        u = jnp.where(y > 0.0, y * lax.rsqrt(y), 0.0) * (ig * xc)
        yield
        h, carries[d] = yield from _linear_scan_tile(a, u, carries[d], reverse=(d == 1))
        yield
        if final:
            gate = gate_ref[pl.ds(r0, TS), :].astype(F32)
            o_ref[pl.ds(r0, TS), :] = (jax.nn.gelu(gate) * (h_ref[pl.ds(r0, TS), :] + h)).astype(BF16)
        else:
            h_ref[pl.ds(r0, TS), :] = h

    def make_body(final):
        def body(i, carry):
            carries = list(carry)
            _lockstep([tile(i, carries, 0, final), tile(n - 1 - i, carries, 1, final)])
            return tuple(carries)
        return body

    zero = jnp.zeros((SUBLANES, LANES), F32)
    carry = lax.fori_loop(0, n // 2, make_body(False), (zero, zero))
    lax.fori_loop(n // 2, n, make_body(True), carry)


def _rglru(proj, par, w4, B, S):
    ng = RG_WIDTH // LANES

    def spec(cb0):
        return pl.BlockSpec((None, S, LANES), lambda b, g: (cb0 + g, b, 0))

    return pl.pallas_call(
        functools.partial(_rg_kernel, S=S),
        grid=(B, ng),
        in_specs=[spec(CB_RX), spec(CB_RGATE),
                  pl.BlockSpec((None, 2 * SUBLANES, LANES), lambda b, g: (g, 0, 0)),
                  pl.BlockSpec((None, LANES, 4 * LANES), lambda b, g: (g, 0, 0))],
        out_specs=pl.BlockSpec((None, S, LANES), lambda b, g: (g, b, 0)),
        out_shape=jax.ShapeDtypeStruct((ng, B * S, LANES), BF16),
        scratch_shapes=[pltpu.VMEM((S + 2 * RG_HALO, LANES), F32), pltpu.VMEM((S, LANES), F32),
                        pltpu.VMEM((S, LANES), F32)],
        compiler_params=_params(("parallel", "parallel")),
        name="rglru",
    )(proj, proj, par, w4)


def _attn_kernel(q0_ref, k0_ref, v0_ref, q1_ref, k1_ref, v1_ref, q2_ref, k2_ref, v2_ref, par_ref, o_ref,
                 perm_ref, qd_ref, kd_ref, vd_ref, bias_ref, og_ref, lg_ref, *, S):
    refs = ((q0_ref, k0_ref, v0_ref), (q1_ref, k1_ref, v1_ref), (q2_ref, k2_ref, v2_ref))
    ct = ATT_TILE

    for g, (_, dil) in enumerate(ATT_GROUPS):
        q_ref, k_ref, v_ref = refs[g]
        par = par_ref[g]
        qg, kg, slope = par[0:1], par[1:2], par[2:3, 0:1]
        L = S // dil
        Q = min(ATT_QBLK, L)
        KW = min(Q + 2 * ATT_RADIUS, L)
        nq = L // Q
        nq_shift = int(math.log2(nq))

        for var in range(3):
            qi = var * ATT_RADIUS + lax.broadcasted_iota(jnp.int32, (Q, KW), 0)
            rel = jnp.abs(qi - lax.broadcasted_iota(jnp.int32, (Q, KW), 1))
            bias_ref[var, :Q, :KW] = jnp.where(rel <= ATT_RADIUS, -slope * rel.astype(F32), NEG_INF)

        per = ct // dil
        if dil > 1:
            pi = lax.broadcasted_iota(jnp.int32, (ct, ct), 0)
            pj = lax.broadcasted_iota(jnp.int32, (ct, ct), 1)
            src_tok = (pi & (per - 1)) * dil + lax.shift_right_logical(pi, int(math.log2(per)))
            perm_ref[...] = jnp.where(pj == src_tok, 1.0, 0.0).astype(BF16)

        def prep_tile(i, q_ref=q_ref, k_ref=k_ref, v_ref=v_ref, qg=qg, kg=kg, dil=dil, per=per, L=L):
            r0 = pl.multiple_of(i * ct, ct)
            xs = (q_ref[pl.ds(r0, ct), :].astype(F32), k_ref[pl.ds(r0, ct), :].astype(F32))
            sums = [jnp.sum(x * x, axis=-1, keepdims=True) for x in xs]
            yield
            tiles = [(x * lax.rsqrt(ss * (1.0 / LANES) + NORM_EPS) * gain).astype(BF16)
                     for x, ss, gain in zip(xs, sums, (qg, kg))]
            tiles.append(v_ref[pl.ds(r0, ct), :])
            if dil > 1:
                tiles = [_dot(perm_ref[...], x) for x in tiles]
                yield
            for dst_ref, y in zip((qd_ref, kd_ref, vd_ref), tiles):
                if dil == 1:
                    dst_ref[pl.ds(r0, ct), :] = y
                else:
                    y = y.astype(BF16)
                    for r in range(dil):
                        dst_ref[pl.ds(pl.multiple_of(r * L + i * per, 16), per), :] = y[r * per:(r + 1) * per, :]

        prep_unroll = min(4, S // ct)

        def prep(i, _, prep_tile=prep_tile, prep_unroll=prep_unroll):
            _lockstep([prep_tile(i * prep_unroll + u) for u in range(prep_unroll)])
            return 0

        lax.fori_loop(0, S // ct // prep_unroll, prep, 0)

        def rows(start, size, dil=dil):
            if dil == 1:
                return pl.ds(start, size)
            return pl.ds(start, size, stride=dil)

        unroll = min(ATT_UNROLL, dil * nq)

        def qblock(it, g=g, dil=dil, L=L, Q=Q, KW=KW, nq=nq, nq_shift=nq_shift, rows=rows):
            r = lax.shift_right_logical(it, nq_shift)
            m0 = (it & (nq - 1)) * Q
            ks = jnp.clip(m0 - ATT_RADIUS, 0, L - KW)
            var = lax.shift_right_logical(m0 - ks, int(math.log2(ATT_RADIUS)))
            base = r * L
            qn = qd_ref[pl.ds(pl.multiple_of(base + m0, 16), Q), :]
            kk = kd_ref[pl.ds(pl.multiple_of(base + ks, 16), KW), :]
            s = _dot_nt(qn, kk)
            yield
            s = s + bias_ref[var, :Q, :KW]
            m = jnp.max(s, axis=-1, keepdims=True)
            yield
            pb = jnp.exp(s - m).astype(BF16)
            vv = vd_ref[pl.ds(pl.multiple_of(base + ks, 16), KW), :]
            ol = _dot(pb, jnp.concatenate([vv, jnp.ones((KW, LANES), BF16)], axis=1))
            yield
            l = ol[:, LANES:]
            og_ref[g, rows(m0 * dil + r, Q), :] = ol[:, :LANES] / l
            lg_ref[g, rows(m0 * dil + r, Q), :] = m + jnp.log(l)

        def qblocks(i, _, unroll=unroll, qblock=qblock):
            _lockstep([qblock(i * unroll + u) for u in range(unroll)])
            return 0

        lax.fori_loop(0, (dil * nq) // unroll, qblocks, 0)

    def merge(i, _):
        r0 = pl.multiple_of(i * ct, ct)
        l0, l1, l2 = (lg_ref[g, pl.ds(r0, ct), :] for g in range(3))
        m = jnp.maximum(jnp.maximum(l0, l1), l2)
        e0, e1, e2 = jnp.exp(l0 - m), jnp.exp(l1 - m), jnp.exp(l2 - m)
        o = (e0 * og_ref[0, pl.ds(r0, ct), :] + e1 * og_ref[1, pl.ds(r0, ct), :]
             + e2 * og_ref[2, pl.ds(r0, ct), :]) / (e0 + e1 + e2)
        o_ref[pl.ds(r0, ct), :] = o.astype(BF16)
        return 0

    lax.fori_loop(0, S // ct, merge, 0)


def _attention(proj, par, B, S):
    H = ATT_HEADS_PER_GROUP
    ngroups = len(ATT_GROUPS)

    def spec(cb0, g):
        return pl.BlockSpec((None, S, LANES), lambda b, j: (cb0 + g * H + j, b, 0))

    in_specs = []
    for g in range(ngroups):
        in_specs += [spec(CB_AQ, g), spec(CB_AK, g), spec(CB_AV, g)]
    in_specs.append(pl.BlockSpec((None, ngroups, SUBLANES, LANES), lambda b, j: (j, 0, 0, 0)))
    return pl.pallas_call(
        functools.partial(_attn_kernel, S=S),
        grid=(B, H),
        in_specs=in_specs,
        out_specs=pl.BlockSpec((None, S, LANES), lambda b, j: (j, b, 0)),
        out_shape=jax.ShapeDtypeStruct((H, B * S, LANES), BF16),
        scratch_shapes=[pltpu.VMEM((ATT_TILE, ATT_TILE), BF16),
                        pltpu.VMEM((S, LANES), BF16), pltpu.VMEM((S, LANES), BF16), pltpu.VMEM((S, LANES), BF16),
                        pltpu.VMEM((3, ATT_QBLK, ATT_QBLK + 2 * ATT_RADIUS), F32),
                        pltpu.VMEM((ngroups, S, LANES), F32), pltpu.VMEM((ngroups, S, LANES), F32)],
        compiler_params=_params(("parallel", "parallel")),
        name="attention",
    )(*([proj] * (3 * ngroups)), par)


def _cat_heads(ref):
    return jnp.concatenate([ref[c] for c in range(ref.shape[0])], axis=-1)


def _merge_kernel(x_ref, hg_ref, rg_ref, att_ref, ga_ref, gb_ref, gc_ref, wb_ref, wo_ref, ng_ref,
                  x1_ref, h2_ref):
    merged = _sigmoid(_cat_heads(ga_ref).astype(F32)) * _dot(_cat_heads(hg_ref), wb_ref[0])
    merged += _sigmoid(_cat_heads(gb_ref).astype(F32)) * _dot(_cat_heads(rg_ref), wb_ref[1])
    merged += _sigmoid(_cat_heads(gc_ref).astype(F32)) * _dot(_cat_heads(att_ref), wb_ref[2])
    x1 = x_ref[...] + _dot(merged.astype(BF16), wo_ref[...])
    x1_ref[...] = x1
    ms = jnp.mean(x1 * x1, axis=-1, keepdims=True)
    h2_ref[...] = (x1 * lax.rsqrt(ms + NORM_EPS) * ng_ref[...]).astype(BF16)


def _merge(x, proj, hg, rg, att, wb, wo, ng):
    T = x.shape[0]
    tm = MERGE_TM
    nb = BRANCH_W // LANES
    ngate = D_MODEL // LANES

    branch = pl.BlockSpec((nb, tm, LANES), lambda i: (0, i, 0))

    def gate(n):
        return pl.BlockSpec((ngate, tm, LANES), lambda i: (CB_GATES // ngate + n, i, 0))

    row = pl.BlockSpec((tm, D_MODEL), lambda i: (i, 0))
    return pl.pallas_call(
        _merge_kernel,
        grid=(T // tm,),
        in_specs=[row, branch, branch, branch, gate(0), gate(1), gate(2),
                  pl.BlockSpec((3, BRANCH_W, D_MODEL), lambda i: (0, 0, 0)),
                  pl.BlockSpec((D_MODEL, D_MODEL), lambda i: (0, 0)),
                  pl.BlockSpec((1, D_MODEL), lambda i: (0, 0))],
        out_specs=[row, row],
        out_shape=[jax.ShapeDtypeStruct((T, D_MODEL), F32), jax.ShapeDtypeStruct((T, D_MODEL), BF16)],
        compiler_params=_params(("parallel",)),
        name="merge",
    )(x, hg, rg, att, proj, proj, proj, wb, wo, ng)


def _ffn_kernel(h_ref, x_ref, w1_ref, w3_ref, w2_ref, o_ref, acc_ref):
    f = pl.program_id(1)
    h = h_ref[...]
    h1 = _dot(h, w1_ref[...])
    y = _dot((h1 * _sigmoid(h1) * _dot(h, w3_ref[...])).astype(BF16), w2_ref[...])

    @pl.when(f == 0)
    def _():
        acc_ref[...] = x_ref[...] + y

    @pl.when(f > 0)
    def _():
        acc_ref[...] += y

    @pl.when(f == pl.num_programs(1) - 1)
    def _():
        o_ref[...] = acc_ref[...]


def _ffn_dense(h2, x1, w1, w3, w2):
    T = h2.shape[0]
    tm, tf = FFN_TM, FFN_TF
    return pl.pallas_call(
        _ffn_kernel,
        grid=(T // tm, FFN_DENSE // tf),
        in_specs=[pl.BlockSpec((tm, D_MODEL), lambda i, f: (i, 0)),
                  pl.BlockSpec((tm, D_MODEL), lambda i, f: (i, 0)),
                  pl.BlockSpec((D_MODEL, tf), lambda i, f: (0, f)),
                  pl.BlockSpec((D_MODEL, tf), lambda i, f: (0, f)),
                  pl.BlockSpec((tf, D_MODEL), lambda i, f: (f, 0))],
        out_specs=pl.BlockSpec((tm, D_MODEL), lambda i, f: (i, 0)),
        out_shape=jax.ShapeDtypeStruct((T, D_MODEL), F32),
        scratch_shapes=[pltpu.VMEM((tm, D_MODEL), F32)],
        compiler_params=_params(("parallel", "arbitrary")),
        name="ffn_dense",
    )(h2, x1, w1, w3, w2)


def _router_kernel(h_ref, wr_ref, o_ref, cnt_ref, base_ref):
    tm = ROUTER_TM

    @pl.when(pl.program_id(0) == 0)
    def _():
        base_ref[...] = jnp.zeros_like(base_ref)

    lane = lax.broadcasted_iota(jnp.int32, (tm, LANES), 1)
    logits = jnp.where(lane < N_EXPERTS, _dot(h_ref[...], wr_ref[...]), -jnp.inf)
    m1 = jnp.max(logits, axis=-1, keepdims=True)
    i1 = jnp.min(jnp.where(logits == m1, lane, LANES), axis=-1, keepdims=True)
    rest = jnp.where(lane == i1, -jnp.inf, logits)
    m2 = jnp.max(rest, axis=-1, keepdims=True)
    i2 = jnp.min(jnp.where(rest == m2, lane, LANES), axis=-1, keepdims=True)
    e21 = jnp.exp(m2 - m1)
    g1 = 1.0 / (1.0 + e21)
    g2 = e21 / (1.0 + e21)

    onehot = jnp.where(lane == i1, 1.0, jnp.where(lane == i2, 1.0, 0.0))
    ti = lax.broadcasted_iota(jnp.int32, (tm, tm), 0)
    si = lax.broadcasted_iota(jnp.int32, (tm, tm), 1)
    before = jnp.where(si < ti, 1.0, 0.0).astype(BF16)
    pos = _dot(before, onehot.astype(BF16)) + base_ref[0:1, :]
    r1 = jnp.sum(jnp.where(lane == i1, pos, 0.0), axis=-1, keepdims=True)
    r2 = jnp.sum(jnp.where(lane == i2, pos, 0.0), axis=-1, keepdims=True)
    total = base_ref[0:1, :] + jnp.sum(onehot, axis=0, keepdims=True)
    base_ref[...] = jnp.broadcast_to(total, base_ref.shape)
    cnt_ref[...] = jnp.broadcast_to(total, cnt_ref.shape)

    out = jnp.where(lane == 0, i1.astype(F32), 0.0)
    out = jnp.where(lane == 1, i2.astype(F32), out)
    out = jnp.where(lane == 2, g1, out)
    out = jnp.where(lane == 3, g2, out)
    out = jnp.where(lane == 4, r1, out)
    out = jnp.where(lane == 5, r2, out)
    o_ref[...] = out


def _router(h2, wr):
    T = h2.shape[0]
    tm = ROUTER_TM
    return pl.pallas_call(
        _router_kernel,
        grid=(T // tm,),
        in_specs=[pl.BlockSpec((tm, D_MODEL), lambda i: (i, 0)),
                  pl.BlockSpec((D_MODEL, LANES), lambda i: (0, 0))],
        out_specs=[pl.BlockSpec((tm, LANES), lambda i: (i, 0)),
                   pl.BlockSpec((SUBLANES, LANES), lambda i: (0, 0))],
        out_shape=[jax.ShapeDtypeStruct((T, LANES), F32), jax.ShapeDtypeStruct((SUBLANES, LANES), F32)],
        scratch_shapes=[pltpu.VMEM((SUBLANES, LANES), F32)],
        compiler_params=_params(("arbitrary",)),
        name="moe_router",
    )(h2, wr)


def _expert_kernel(be_ref, nu_ref, x_ref, w1_ref, w3_ref, w2_ref, o_ref, acc_ref):
    i = pl.program_id(0)
    f = pl.program_id(1)

    @pl.when(f == 0)
    def _():
        acc_ref[...] = jnp.zeros_like(acc_ref)

    @pl.when(i < nu_ref[0])
    def _():
        x = x_ref[...]
        h1 = _dot(x, w1_ref[...])
        acc_ref[...] += _dot((h1 * _sigmoid(h1) * _dot(x, w3_ref[...])).astype(BF16), w2_ref[...])

    @pl.when(f == pl.num_programs(1) - 1)
    def _():
        o_ref[...] = acc_ref[...].astype(BF16)


def _experts(rows, block_expert, n_used, w1, w3, w2):
    n_rows = rows.shape[0]
    R, tf = MOE_ROWS, MOE_TF
    grid_spec = pltpu.PrefetchScalarGridSpec(
        num_scalar_prefetch=2,
        grid=(n_rows // R, FFN_EXPERT // tf),
        in_specs=[pl.BlockSpec((R, D_MODEL), lambda i, f, be, nu: (i, 0)),
                  pl.BlockSpec((None, D_MODEL, tf), lambda i, f, be, nu: (be[i], 0, f)),
                  pl.BlockSpec((None, D_MODEL, tf), lambda i, f, be, nu: (be[i], 0, f)),
                  pl.BlockSpec((None, tf, D_MODEL), lambda i, f, be, nu: (be[i], f, 0))],
        out_specs=pl.BlockSpec((R, D_MODEL), lambda i, f, be, nu: (i, 0)),
        scratch_shapes=[pltpu.VMEM((R, D_MODEL), F32)],
    )
    return pl.pallas_call(
        _expert_kernel,
        grid_spec=grid_spec,
        out_shape=jax.ShapeDtypeStruct((n_rows, D_MODEL), BF16),
        compiler_params=_params(("parallel", "arbitrary")),
        name="moe_experts",
    )(block_expert, n_used, rows, w1, w3, w2)


def _combine_kernel(x_ref, y1_ref, y2_ref, info_ref, o_ref):
    info = info_ref[...]
    o_ref[...] = (x_ref[...] + info[:, 2:3] * y1_ref[...].astype(F32)
                  + info[:, 3:4] * y2_ref[...].astype(F32))


def _combine(x1, y1, y2, info):
    T = x1.shape[0]
    tm = COMBINE_TM
    row = pl.BlockSpec((tm, D_MODEL), lambda i: (i, 0))
    return pl.pallas_call(
        _combine_kernel,
        grid=(T // tm,),
        in_specs=[row, row, row, pl.BlockSpec((tm, LANES), lambda i: (i, 0))],
        out_specs=row,
        out_shape=jax.ShapeDtypeStruct((T, D_MODEL), F32),
        compiler_params=_params(("parallel",)),
        name="moe_combine",
    )(x1, y1, y2, info)


def _moe(h2, x1, wr, w1, w3, w2):
    T = h2.shape[0]
    R = MOE_ROWS
    n_blocks = -(-(T * TOP_K) // R) + N_EXPERTS
    info, cnt = _router(h2, wr)
    expert = info[:, 0:2].astype(jnp.int32)
    rank = info[:, 4:6].astype(jnp.int32)
    counts = cnt[0, :N_EXPERTS].astype(jnp.int32)
    blocks_per = (counts + R - 1) // R
    bend = jnp.cumsum(blocks_per)
    pstart = (bend - blocks_per) * R
    dest = pstart[expert] + rank
    n_used = bend[-1:]
    block_expert = jnp.minimum(
        jnp.searchsorted(bend, jnp.arange(n_blocks, dtype=jnp.int32), side='right'),
        N_EXPERTS - 1).astype(jnp.int32)
    token = jnp.broadcast_to(jnp.arange(T, dtype=jnp.int32)[:, None], (T, TOP_K))
    tok_of_row = jnp.zeros((n_blocks * R,), jnp.int32).at[dest.reshape(-1)].set(
        token.reshape(-1), unique_indices=True, mode='promise_in_bounds')
    rows = h2.at[tok_of_row].get(mode='promise_in_bounds')
    y = _experts(rows, block_expert, n_used, w1, w3, w2)
    y1 = y.at[dest[:, 0]].get(mode='promise_in_bounds')
    y2 = y.at[dest[:, 1]].get(mode='promise_in_bounds')
    return _combine(x1, y1, y2, info)


def _pad_rows(rows, n):
    rows = jnp.stack(rows, axis=-2)
    pad = [(0, 0)] * (rows.ndim - 2) + [(0, n - rows.shape[-2]), (0, 0)]
    return jnp.pad(rows, pad)


def _hgrn_params(lb, norm_g):
    lb = lb.reshape(2, HG_HEADS, HG_DK)
    rows = []
    for d in range(2):
        rows += [jnp.log(jnp.maximum(lb[d], LB_EPS)), jnp.log1p(-lb[d]), 1.0 - lb[d]]
    rows.append(norm_g.reshape(HG_HEADS, HG_DK))
    return _pad_rows(rows, SUBLANES)


def _rg_params(conv_w, conv_b, b_a, b_x, lam):
    ng = RG_WIDTH // LANES
    g = lambda t: t.reshape(ng, LANES)
    rows = [g(conv_w[j]) for j in range(RG_CONV_W)] + [g(conv_b)]
    rows += [g(b_a[0]), g(b_x[0]), g(b_a[1]), g(b_x[1]), g(lam[0]), g(lam[1])]
    return _pad_rows(rows, 2 * SUBLANES)


def _rg_gate_weights(w_a, w_x):
    ng = RG_WIDTH // LANES
    per = RG_BLOCKS // ng

    def dense(w):
        w = w.reshape(ng, per, RG_BLOCK_W, RG_BLOCK_W)
        eye = jnp.eye(per, dtype=w.dtype)
        return jnp.einsum('gpcd,pq->gpcqd', w, eye).reshape(ng, LANES, LANES)

    return jnp.concatenate([dense(w_a[0]), dense(w_x[0]), dense(w_a[1]), dense(w_x[1])], axis=-1).astype(BF16)


def _attn_params(q_g, k_g):
    H = ATT_HEADS_PER_GROUP
    groups = []
    for gi, (_, dil) in enumerate(ATT_GROUPS):
        heads = jnp.arange(gi * H + 1, (gi + 1) * H + 1, dtype=F32)
        slopes = 2.0 ** (-8.0 * heads / ATT_HEADS) * dil
        rows = [jnp.broadcast_to(q_g * (ATT_DH ** -0.5), (H, LANES)), jnp.broadcast_to(k_g, (H, LANES)),
                jnp.broadcast_to(slopes[:, None], (H, LANES))]
        groups.append(_pad_rows(rows, SUBLANES))
    return jnp.stack(groups, axis=1)


def _trunk(x3, p):
    B, S, _ = x3.shape
    x = x3.reshape(B * S, D_MODEL)
    sm = jax.nn.softmax(p['hg_lb_logits'].astype(F32), axis=1)
    lower_bounds = jnp.cumsum(sm, axis=1) - sm[:, :1]
    for l in range(DEPTH):
        proj = _inproj(x, p['norm_mix_g'][l][None], p['w_in'][l].astype(BF16))
        hg = _hgrn(proj, _hgrn_params(lower_bounds[:, l], p['hg_norm_g'][l]), B, S)
        rg = _rglru(proj,
                    _rg_params(p['rg_conv_w'][l], p['rg_conv_b'][l], p['rg_b_a'][l], p['rg_b_x'][l],
                               p['rg_lambda'][l]),
                    _rg_gate_weights(p['rg_w_a'][l], p['rg_w_x'][l]), B, S)
        att = _attention(proj, _attn_params(p['attn_q_g'][l], p['attn_k_g'][l]), B, S)
        x1, h2 = _merge(x, proj, hg, rg, att, p['w_branch'][l].astype(BF16), p['w_out'][l].astype(BF16),
                        p['norm_ffn_g'][l][None])
        j = l // 2
        if l % 2 == 0:
            x = _ffn_dense(h2, x1, p['ffn_w1'][j].astype(BF16), p['ffn_w3'][j].astype(BF16),
                           p['ffn_w2'][j].astype(BF16))
        else:
            wr = jnp.pad(p['moe_router'][j], ((0, 0), (0, LANES - N_EXPERTS))).astype(BF16)
            x = _moe(h2, x1, wr, p['moe_w1'][j].astype(BF16), p['moe_w3'][j].astype(BF16),
                     p['moe_w2'][j].astype(BF16))
    return x.reshape(B, S, D_MODEL)


def kernel(x_prompt, x_sample, norm_mix_g, w_in, hg_lb_logits, hg_norm_g, rg_conv_w, rg_conv_b, rg_w_a, rg_b_a,
           rg_w_x, rg_b_x, rg_lambda, attn_q_g, attn_k_g, w_branch, w_out, norm_ffn_g, ffn_w1, ffn_w3, ffn_w2,
           moe_router, moe_w1, moe_w3, moe_w2):
    p = dict(norm_mix_g=norm_mix_g, w_in=w_in, hg_lb_logits=hg_lb_logits, hg_norm_g=hg_norm_g,
             rg_conv_w=rg_conv_w, rg_conv_b=rg_conv_b, rg_w_a=rg_w_a, rg_b_a=rg_b_a, rg_w_x=rg_w_x,
             rg_b_x=rg_b_x, rg_lambda=rg_lambda, attn_q_g=attn_q_g, attn_k_g=attn_k_g,
             w_branch=w_branch, w_out=w_out, norm_ffn_g=norm_ffn_g, ffn_w1=ffn_w1, ffn_w3=ffn_w3,
             ffn_w2=ffn_w2, moe_router=moe_router, moe_w1=moe_w1, moe_w3=moe_w3, moe_w2=moe_w2)
    return (_trunk(x_prompt, p), _trunk(x_sample, p))
```

```python
import functools
import math

import jax
import jax.numpy as jnp
from jax import lax
from jax.experimental import pallas as pl
from jax.experimental.pallas import tpu as pltpu

F32 = jnp.float32
BF16 = jnp.bfloat16

LANES = 128
SUBLANES = 8
VMEM_LIMIT_BYTES = 56 * 1024 * 1024

D_MODEL = 1024
DEPTH = 2
BRANCH_W = 512
HG_HEADS = 4
HG_DK = 128
HG_CHUNK = 64
HG_HEADS_PER_STEP = 2
HG_LEVELS = 6
LB_EPS = 1e-20
RG_WIDTH = 512
RG_BLOCKS = 8
RG_BLOCK_W = RG_WIDTH // RG_BLOCKS
RG_CONV_W = 4
RG_C = 8.0
RG_TILE = 256
RG_HALO = 8
ATT_GROUPS = ((128, 1), (512, 4), (2048, 16))
ATT_HEADS_PER_GROUP = 4
ATT_HEADS = ATT_HEADS_PER_GROUP * len(ATT_GROUPS)
ATT_DH = 128
ATT_RADIUS = 64
ATT_QBLK = 128
ATT_UNROLL = 8
ATT_TILE = 256
FFN_DENSE = 2816
N_EXPERTS = 8
TOP_K = 2
FFN_EXPERT = 3584
NORM_EPS = 1e-6
NEG_INF = -1e30
LOG2_E = 1.4426950408889634

CB_HQ, CB_HF, CB_HB, CB_HV, CB_HG = 0, 4, 8, 12, 16
CB_RX, CB_RGATE = 20, 24
CB_AQ, CB_AK, CB_AV = 28, 40, 52
CB_GATES = 64
IN_COLS = 11264
N_CB = IN_COLS // LANES

INPROJ_TM, INPROJ_TN = 1024, 1024
MERGE_TM = 512
FFN_TM, FFN_TF = 512, 1408
MOE_ROWS, MOE_TF = 512, 1792
ROUTER_TM = 512
COMBINE_TM = 512


def _params(sem):
    return pltpu.CompilerParams(dimension_semantics=sem, vmem_limit_bytes=VMEM_LIMIT_BYTES)


def _dot(a, b):
    return jnp.dot(a, b, preferred_element_type=F32)


def _dot_nt(a, b):
    return lax.dot_general(a, b, (((1,), (1,)), ((), ())), preferred_element_type=F32)


def _sigmoid(x):
    return jax.nn.sigmoid(x)


def _lockstep(gens):
    results = [None] * len(gens)
    active = list(range(len(gens)))
    while active:
        for idx in list(active):
            try:
                next(gens[idx])
            except StopIteration as stop:
                results[idx] = stop.value
                active.remove(idx)
    return results


def _inproj_kernel(x_ref, g_ref, w_ref, o_ref, h_ref):
    @pl.when(pl.program_id(1) == 0)
    def _():
        x = x_ref[...]
        ms = jnp.mean(x * x, axis=-1, keepdims=True)
        h_ref[...] = (x * lax.rsqrt(ms + NORM_EPS) * g_ref[...]).astype(BF16)

    acc = _dot(h_ref[...], w_ref[...])
    for c in range(INPROJ_TN // LANES):
        o_ref[c] = acc[:, c * LANES:(c + 1) * LANES].astype(BF16)


def _inproj(x, g, w):
    T = x.shape[0]
    return pl.pallas_call(
        _inproj_kernel,
        grid=(T // INPROJ_TM, IN_COLS // INPROJ_TN),
        in_specs=[
            pl.BlockSpec((INPROJ_TM, D_MODEL), lambda i, j: (i, 0)),
            pl.BlockSpec((1, D_MODEL), lambda i, j: (0, 0)),
            pl.BlockSpec((D_MODEL, INPROJ_TN), lambda i, j: (0, j)),
        ],
        out_specs=pl.BlockSpec((INPROJ_TN // LANES, INPROJ_TM, LANES), lambda i, j: (j, i, 0)),
        out_shape=jax.ShapeDtypeStruct((N_CB, T, LANES), BF16),
        scratch_shapes=[pltpu.VMEM((INPROJ_TM, D_MODEL), BF16)],
        compiler_params=_params(("parallel", "arbitrary")),
        name="inproj",
    )(x, g, w)


def _hgrn_chain(q, vb, z, c0, c1, oml, code, sign_ref, states, slot, reverse):
    C = HG_CHUNK
    nv = C // SUBLANES
    e = jnp.exp(-jnp.abs(z))
    log_sig = jnp.minimum(z, 0.0) - jnp.log(1.0 + e)
    t = c1 + log_sig
    log_f = jnp.maximum(c0, t) + jnp.log(1.0 + jnp.exp(-jnp.abs(c0 - t)))
    k = oml * (jnp.where(z >= 0.0, e, 1.0) / (1.0 + e))
    sc_diag = _dot_nt(q.astype(BF16), k.astype(BF16))
    vt = vb.astype(F32).T.astype(BF16)
    yield
    x3 = log_f.reshape(nv, SUBLANES, LANES)
    sub3 = lax.broadcasted_iota(jnp.int32, (nv, SUBLANES, LANES), 1)
    d = 1
    while d < SUBLANES:
        if reverse:
            x3 = x3 + jnp.where(sub3 < SUBLANES - d, pltpu.roll(x3, SUBLANES - d, 1), 0.0)
        else:
            x3 = x3 + jnp.where(sub3 >= d, pltpu.roll(x3, d, 1), 0.0)
        d *= 2
    edge = 0 if reverse else SUBLANES - 1
    tot = jnp.broadcast_to(x3[:, edge:edge + 1, :], (nv, SUBLANES, LANES))
    groups = [None] * nv
    run = None
    for g in (range(nv - 1, -1, -1) if reverse else range(nv)):
        groups[g] = x3[g] if run is None else x3[g] + run
        run = tot[g] if run is None else run + tot[g]
    a3 = jnp.stack(groups) * LOG2_E
    a = a3.reshape(C, LANES)
    sub = lax.broadcasted_iota(jnp.int32, (C, LANES), 0) & (SUBLANES - 1)
    yield
    q_in = (q * jnp.exp2(a)).astype(BF16)
    a_last = a[0:1, :] if reverse else a[C - 1:C, :]
    k_end = (k * jnp.exp2(a_last - a)).astype(BF16)
    st = states[slot]
    o_inter = _dot_nt(q_in, st.astype(BF16))
    states[slot] = st * jnp.exp2(a_last) + _dot(vt, k_end)
    yield

    def pick(r):
        return jnp.broadcast_to(a3[:, r:r + 1, :], (nv, SUBLANES, LANES)).reshape(C, LANES)

    level_scores = []
    for b in range(HG_LEVELS):
        h = 1 << b
        m = h if reverse else h - 1
        if h == 1:
            if reverse:
                ref = jnp.where((sub & 1) == 0, pltpu.roll(a, C - 1, 0), a)
            else:
                ref = jnp.where((sub & 1) == 1, pltpu.roll(a, 1, 0), a)
        elif 2 * h < SUBLANES:
            ref = jnp.where(sub < 2 * h, pick(m), pick(m + 2 * h))
        elif 2 * h == SUBLANES:
            ref = pick(m)
        else:
            pieces = [jnp.broadcast_to(a[blk * 2 * h + m:blk * 2 * h + m + 1, :], (2 * h, LANES))
                      for blk in range(C // (2 * h))]
            ref = jnp.concatenate(pieces, axis=0) if len(pieces) > 1 else pieces[0]
        eb = jnp.exp2(((ref - a) if reverse else (a - ref)) * sign_ref[b])
        level_scores.append(_dot_nt((q * eb).astype(BF16), (k * eb).astype(BF16)))
        yield
    scores = jnp.where(code == HG_LEVELS, sc_diag, 0.0)
    for b in range(HG_LEVELS):
        scores = jnp.where(code == b, level_scores[b], scores)
    o = o_inter + _dot(scores.astype(BF16), vb)
    yield
    return o


def _hgrn_kernel(*refs, S):
    C = HG_CHUNK
    n = S // C
    nh = HG_HEADS_PER_STEP
    head_refs = [refs[5 * hh:5 * hh + 5] for hh in range(nh)]
    par_ref, o_ref, acc_ref, code_ref, sign_ref = refs[5 * nh:]

    ti = lax.broadcasted_iota(jnp.int32, (C, C), 0)
    si = lax.broadcasted_iota(jnp.int32, (C, C), 1)
    x = ti ^ si
    lvl = jnp.zeros((C, C), jnp.int32)
    for b in range(1, HG_LEVELS):
        lvl = lvl + jnp.where(x >= (1 << b), 1, 0)
    diag = jnp.where(ti == si, HG_LEVELS, -1)
    code_ref[0] = jnp.where(ti > si, lvl, diag)
    code_ref[1] = jnp.where(ti < si, lvl, diag)
    row = lax.broadcasted_iota(jnp.int32, (C, LANES), 0)
    for b in range(HG_LEVELS):
        sign_ref[b] = jnp.where((row & (1 << b)) != 0, 1.0, -1.0)

    def chain(ci, hh, d, states):
        q_ref, zf_ref, zb_ref, v_ref, _ = head_refs[hh]
        z_ref = zb_ref if d else zf_ref
        par = par_ref[hh]
        r0 = pl.multiple_of(ci * C, C)
        return _hgrn_chain(q_ref[pl.ds(r0, C), :].astype(F32), v_ref[pl.ds(r0, C), :],
                           z_ref[pl.ds(r0, C), :].astype(F32), par[3 * d:3 * d + 1], par[3 * d + 1:3 * d + 2],
                           par[3 * d + 2:3 * d + 3], code_ref[d], sign_ref, states, 2 * hh + d, d == 1)

    def make_body(accumulate):
        def body(i, carry):
            states = list(carry)
            jobs = []
            for hh in range(nh):
                jobs += [(2 * i, hh, 0), (2 * i + 1, hh, 0), (n - 1 - 2 * i, hh, 1), (n - 2 - 2 * i, hh, 1)]
            outs = _lockstep([chain(ci, hh, d, states) for ci, hh, d in jobs])
            for (ci, hh, _), o in zip(jobs, outs):
                r0 = pl.multiple_of(ci * C, C)
                if accumulate:
                    acc_ref[hh, pl.ds(r0, C), :] += o
                else:
                    acc_ref[hh, pl.ds(r0, C), :] = o
            return tuple(states)
        return body

    zero = jnp.zeros((LANES, HG_DK), F32)
    carry = lax.fori_loop(0, n // 4, make_body(False), (zero,) * (2 * nh))
    lax.fori_loop(n // 4, n // 2, make_body(True), carry)

    ft = 256

    def fin(i, _):
        r0 = pl.multiple_of(i * ft, ft)
        for hh in range(nh):
            o = acc_ref[hh, pl.ds(r0, ft), :]
            y = o * lax.rsqrt(jnp.mean(o * o, axis=-1, keepdims=True) + NORM_EPS) * par_ref[hh][6:7]
            g = head_refs[hh][4][pl.ds(r0, ft), :].astype(F32)
            o_ref[hh, pl.ds(r0, ft), :] = (y * (g * _sigmoid(g))).astype(BF16)
        return 0

    lax.fori_loop(0, S // ft, fin, 0)


def _hgrn(proj, par, B, S):
    nh = HG_HEADS_PER_STEP

    def spec(cb0, hh):
        return pl.BlockSpec((None, S, LANES), lambda b, h: (cb0 + nh * h + hh, b, 0))

    in_specs = []
    for hh in range(nh):
        in_specs += [spec(CB_HQ, hh), spec(CB_HF, hh), spec(CB_HB, hh), spec(CB_HV, hh), spec(CB_HG, hh)]
    in_specs.append(pl.BlockSpec((nh, SUBLANES, LANES), lambda b, h: (h, 0, 0)))
    return pl.pallas_call(
        functools.partial(_hgrn_kernel, S=S),
        grid=(B, HG_HEADS // nh),
        in_specs=in_specs,
        out_specs=pl.BlockSpec((nh, S, LANES), lambda b, h: (h, b, 0)),
        out_shape=jax.ShapeDtypeStruct((HG_HEADS, B * S, LANES), BF16),
        scratch_shapes=[pltpu.VMEM((nh, S, LANES), F32), pltpu.VMEM((2, HG_CHUNK, HG_CHUNK), jnp.int32),
                        pltpu.VMEM((HG_LEVELS, HG_CHUNK, LANES), F32)],
        compiler_params=_params(("parallel", "parallel")),
        name="hgrn2",
    )(*([proj] * (5 * nh)), par)


def _linear_scan_tile(a, u, carry, reverse):
    n = a.shape[0]
    nv = n // SUBLANES
    a3 = a.reshape(nv, SUBLANES, LANES)
    u3 = u.reshape(nv, SUBLANES, LANES)
    sub = lax.broadcasted_iota(jnp.int32, (nv, SUBLANES, LANES), 1)
    d = 1
    while d < SUBLANES:
        if reverse:
            keep = sub < SUBLANES - d
            shift = SUBLANES - d
        else:
            keep = sub >= d
            shift = d
        a_s = jnp.where(keep, pltpu.roll(a3, shift, 1), 1.0)
        u_s = jnp.where(keep, pltpu.roll(u3, shift, 1), 0.0)
        u3 = a3 * u_s + u3
        a3 = a3 * a_s
        d *= 2
        yield
    edge = 0 if reverse else SUBLANES - 1
    a_tot = jnp.broadcast_to(a3[:, edge:edge + 1, :], a3.shape)
    u_tot = jnp.broadcast_to(u3[:, edge:edge + 1, :], u3.shape)
    groups = [None] * nv
    for g in (range(nv - 1, -1, -1) if reverse else range(nv)):
        groups[g] = u3[g] + a3[g] * carry
        carry = u_tot[g] + a_tot[g] * carry
        if g % 4 == 0:
            yield
    return jnp.stack(groups).reshape(n, LANES), carry


def _rg_kernel(x_ref, gate_ref, par_ref, w_ref, o_ref, xpad_ref, xc_ref, h_ref, *, S):
    TS = RG_TILE
    n = S // TS
    par = par_ref[...]
    conv_b = par[4:5]

    xpad_ref[pl.ds(0, RG_HALO), :] = jnp.zeros((RG_HALO, LANES), F32)
    xpad_ref[pl.ds(S + RG_HALO, RG_HALO), :] = jnp.zeros((RG_HALO, LANES), F32)

    def copy(i, _):
        r0 = pl.multiple_of(i * TS, TS)
        xpad_ref[pl.ds(r0 + RG_HALO, TS), :] = x_ref[pl.ds(r0, TS), :].astype(F32)
        return 0

    lax.fori_loop(0, n, copy, 0)

    def conv(i, _):
        r0 = pl.multiple_of(i * TS, TS)
        xc = conv_b
        for j in range(RG_CONV_W):
            off = RG_HALO - RG_CONV_W // 2 + j
            xc = xc + par[j:j + 1] * xpad_ref[pl.ds(r0 + off, TS), :]
        xc_ref[pl.ds(r0, TS), :] = xc
        return 0

    lax.fori_loop(0, n, conv, 0)

    def tile(i, carries, d, final):
        r0 = pl.multiple_of(i * TS, TS)
        xc = xc_ref[pl.ds(r0, TS), :]
        gts = _dot(xc.astype(BF16), w_ref[:, d * 2 * LANES:(d + 1) * 2 * LANES])
        yield
        r = _sigmoid(gts[:, :LANES] + par[5 + 2 * d:6 + 2 * d])
        ig = _sigmoid(gts[:, LANES:] + par[6 + 2 * d:7 + 2 * d])
        lam = par[9 + d:10 + d]
        softplus_neg_lam = jnp.maximum(-lam, 0.0) + jnp.log1p(jnp.exp(-jnp.abs(lam)))
        log_a = (-RG_C) * r * softplus_neg_lam
        a = jnp.exp(log_a)
        y = 1.0 - a * a
        u = jnp.where(y > 0.0, y * lax.rsqrt(y), 0.0) * (ig * xc)
        yield
        h, carries[d] = yield from _linear_scan_tile(a, u, carries[d], reverse=(d == 1))
        yield
        if final:
            gate = gate_ref[pl.ds(r0, TS), :].astype(F32)
            o_ref[pl.ds(r0, TS), :] = (jax.nn.gelu(gate) * (h_ref[pl.ds(r0, TS), :] + h)).astype(BF16)
        else:
            h_ref[pl.ds(r0, TS), :] = h

    def make_body(final):
        def body(i, carry):
            carries = list(carry)
            _lockstep([tile(i, carries, 0, final), tile(n - 1 - i, carries, 1, final)])
            return tuple(carries)
        return body

    zero = jnp.zeros((SUBLANES, LANES), F32)
    carry = lax.fori_loop(0, n // 2, make_body(False), (zero, zero))
    lax.fori_loop(n // 2, n, make_body(True), carry)


def _rglru(proj, par, w4, B, S):
    ng = RG_WIDTH // LANES

    def spec(cb0):
        return pl.BlockSpec((None, S, LANES), lambda b, g: (cb0 + g, b, 0))

    return pl.pallas_call(
        functools.partial(_rg_kernel, S=S),
        grid=(B, ng),
        in_specs=[spec(CB_RX), spec(CB_RGATE),
                  pl.BlockSpec((None, 2 * SUBLANES, LANES), lambda b, g: (g, 0, 0)),
                  pl.BlockSpec((None, LANES, 4 * LANES), lambda b, g: (g, 0, 0))],
        out_specs=pl.BlockSpec((None, S, LANES), lambda b, g: (g, b, 0)),
        out_shape=jax.ShapeDtypeStruct((ng, B * S, LANES), BF16),
        scratch_shapes=[pltpu.VMEM((S + 2 * RG_HALO, LANES), F32), pltpu.VMEM((S, LANES), F32),
                        pltpu.VMEM((S, LANES), F32)],
        compiler_params=_params(("parallel", "parallel")),
        name="rglru",
    )(proj, proj, par, w4)


def _attn_kernel(q0_ref, k0_ref, v0_ref, q1_ref, k1_ref, v1_ref, q2_ref, k2_ref, v2_ref, par_ref, o_ref,
                 perm_ref, qd_ref, kd_ref, vd_ref, bias_ref, og_ref, lg_ref, *, S):
    refs = ((q0_ref, k0_ref, v0_ref), (q1_ref, k1_ref, v1_ref), (q2_ref, k2_ref, v2_ref))
    ct = ATT_TILE

    for g, (_, dil) in enumerate(ATT_GROUPS):
        q_ref, k_ref, v_ref = refs[g]
        par = par_ref[g]
        qg, kg, slope = par[0:1], par[1:2], par[2:3, 0:1]
        L = S // dil
        Q = min(ATT_QBLK, L)
        KW = min(Q + 2 * ATT_RADIUS, L)
        nq = L // Q
        nq_shift = int(math.log2(nq))

        for var in range(3):
            qi = var * ATT_RADIUS + lax.broadcasted_iota(jnp.int32, (Q, KW), 0)
            rel = jnp.abs(qi - lax.broadcasted_iota(jnp.int32, (Q, KW), 1))
            bias_ref[var, :Q, :KW] = jnp.where(rel <= ATT_RADIUS, -slope * rel.astype(F32), NEG_INF)

        per = ct // dil
        if dil > 1:
            pi = lax.broadcasted_iota(jnp.int32, (ct, ct), 0)
            pj = lax.broadcasted_iota(jnp.int32, (ct, ct), 1)
            src_tok = (pi & (per - 1)) * dil + lax.shift_right_logical(pi, int(math.log2(per)))
            perm_ref[...] = jnp.where(pj == src_tok, 1.0, 0.0).astype(BF16)

        def prep_tile(i, q_ref=q_ref, k_ref=k_ref, v_ref=v_ref, qg=qg, kg=kg, dil=dil, per=per, L=L):
            r0 = pl.multiple_of(i * ct, ct)
            xs = (q_ref[pl.ds(r0, ct), :].astype(F32), k_ref[pl.ds(r0, ct), :].astype(F32))
            sums = [jnp.sum(x * x, axis=-1, keepdims=True) for x in xs]
            yield
            tiles = [(x * lax.rsqrt(ss * (1.0 / LANES) + NORM_EPS) * gain).astype(BF16)
                     for x, ss, gain in zip(xs, sums, (qg, kg))]
            tiles.append(v_ref[pl.ds(r0, ct), :])
            if dil > 1:
                y = _dot(perm_ref[...], jnp.concatenate(tiles, axis=1))
                tiles = [y[:, c * LANES:(c + 1) * LANES] for c in range(3)]
                yield
            for dst_ref, y in zip((qd_ref, kd_ref, vd_ref), tiles):
                if dil == 1:
                    dst_ref[pl.ds(r0, ct), :] = y
                else:
                    y = y.astype(BF16)
                    for r in range(dil):
                        dst_ref[pl.ds(pl.multiple_of(r * L + i * per, 16), per), :] = y[r * per:(r + 1) * per, :]

        prep_unroll = min(4, S // ct)

        def prep(i, _, prep_tile=prep_tile, prep_unroll=prep_unroll):
            _lockstep([prep_tile(i * prep_unroll + u) for u in range(prep_unroll)])
            return 0

        lax.fori_loop(0, S // ct // prep_unroll, prep, 0)

        def rows(start, size, dil=dil):
            if dil == 1:
                return pl.ds(start, size)
            return pl.ds(start, size, stride=dil)

        unroll = min(ATT_UNROLL, dil * nq)

        def qblock(it, g=g, dil=dil, L=L, Q=Q, KW=KW, nq=nq, nq_shift=nq_shift, rows=rows):
            r = lax.shift_right_logical(it, nq_shift)
            m0 = (it & (nq - 1)) * Q
            ks = jnp.clip(m0 - ATT_RADIUS, 0, L - KW)
            var = lax.shift_right_logical(m0 - ks, int(math.log2(ATT_RADIUS)))
            base = r * L
            qn = qd_ref[pl.ds(pl.multiple_of(base + m0, 16), Q), :]
            kk = kd_ref[pl.ds(pl.multiple_of(base + ks, 16), KW), :]
            s = _dot_nt(qn, kk)
            yield
            s = s + bias_ref[var, :Q, :KW]
            m = jnp.max(s, axis=-1, keepdims=True)
            yield
            pb = jnp.exp(s - m).astype(BF16)
            vv = vd_ref[pl.ds(pl.multiple_of(base + ks, 16), KW), :]
            ol = _dot(pb, jnp.concatenate([vv, jnp.ones((KW, LANES), BF16)], axis=1))
            yield
            l = ol[:, LANES:]
            og_ref[g, rows(m0 * dil + r, Q), :] = ol[:, :LANES] / l
            lg_ref[g, rows(m0 * dil + r, Q), :] = m + jnp.log(l)

        def qblocks(i, _, unroll=unroll, qblock=qblock):
            _lockstep([qblock(i * unroll + u) for u in range(unroll)])
            return 0

        lax.fori_loop(0, (dil * nq) // unroll, qblocks, 0)

    def merge(i, _):
        r0 = pl.multiple_of(i * ct, ct)
        l0, l1, l2 = (lg_ref[g, pl.ds(r0, ct), :] for g in range(3))
        m = jnp.maximum(jnp.maximum(l0, l1), l2)
        e0, e1, e2 = jnp.exp(l0 - m), jnp.exp(l1 - m), jnp.exp(l2 - m)
        o = (e0 * og_ref[0, pl.ds(r0, ct), :] + e1 * og_ref[1, pl.ds(r0, ct), :]
             + e2 * og_ref[2, pl.ds(r0, ct), :]) / (e0 + e1 + e2)
        o_ref[pl.ds(r0, ct), :] = o.astype(BF16)
        return 0

    lax.fori_loop(0, S // ct, merge, 0)


def _attention(proj, par, B, S):
    H = ATT_HEADS_PER_GROUP
    ngroups = len(ATT_GROUPS)

    def spec(cb0, g):
        return pl.BlockSpec((None, S, LANES), lambda b, j: (cb0 + g * H + j, b, 0))

    in_specs = []
    for g in range(ngroups):
        in_specs += [spec(CB_AQ, g), spec(CB_AK, g), spec(CB_AV, g)]
    in_specs.append(pl.BlockSpec((None, ngroups, SUBLANES, LANES), lambda b, j: (j, 0, 0, 0)))
    return pl.pallas_call(
        functools.partial(_attn_kernel, S=S),
        grid=(B, H),
        in_specs=in_specs,
        out_specs=pl.BlockSpec((None, S, LANES), lambda b, j: (j, b, 0)),
        out_shape=jax.ShapeDtypeStruct((H, B * S, LANES), BF16),
        scratch_shapes=[pltpu.VMEM((ATT_TILE, ATT_TILE), BF16),
                        pltpu.VMEM((S, LANES), BF16), pltpu.VMEM((S, LANES), BF16), pltpu.VMEM((S, LANES), BF16),
                        pltpu.VMEM((3, ATT_QBLK, ATT_QBLK + 2 * ATT_RADIUS), F32),
                        pltpu.VMEM((ngroups, S, LANES), F32), pltpu.VMEM((ngroups, S, LANES), F32)],
        compiler_params=_params(("parallel", "parallel")),
        name="attention",
    )(*([proj] * (3 * ngroups)), par)


def _cat_heads(ref):
    return jnp.concatenate([ref[c] for c in range(ref.shape[0])], axis=-1)


def _merge_kernel(x_ref, hg_ref, rg_ref, att_ref, ga_ref, gb_ref, gc_ref, wb_ref, wo_ref, ng_ref,
                  x1_ref, h2_ref):
    merged = _sigmoid(_cat_heads(ga_ref).astype(F32)) * _dot(_cat_heads(hg_ref), wb_ref[0])
    merged += _sigmoid(_cat_heads(gb_ref).astype(F32)) * _dot(_cat_heads(rg_ref), wb_ref[1])
    merged += _sigmoid(_cat_heads(gc_ref).astype(F32)) * _dot(_cat_heads(att_ref), wb_ref[2])
    x1 = x_ref[...] + _dot(merged.astype(BF16), wo_ref[...])
    x1_ref[...] = x1
    ms = jnp.mean(x1 * x1, axis=-1, keepdims=True)
    h2_ref[...] = (x1 * lax.rsqrt(ms + NORM_EPS) * ng_ref[...]).astype(BF16)


def _merge(x, proj, hg, rg, att, wb, wo, ng):
    T = x.shape[0]
    tm = MERGE_TM
    nb = BRANCH_W // LANES
    ngate = D_MODEL // LANES

    branch = pl.BlockSpec((nb, tm, LANES), lambda i: (0, i, 0))

    def gate(n):
        return pl.BlockSpec((ngate, tm, LANES), lambda i: (CB_GATES // ngate + n, i, 0))

    row = pl.BlockSpec((tm, D_MODEL), lambda i: (i, 0))
    return pl.pallas_call(
        _merge_kernel,
        grid=(T // tm,),
        in_specs=[row, branch, branch, branch, gate(0), gate(1), gate(2),
                  pl.BlockSpec((3, BRANCH_W, D_MODEL), lambda i: (0, 0, 0)),
                  pl.BlockSpec((D_MODEL, D_MODEL), lambda i: (0, 0)),
                  pl.BlockSpec((1, D_MODEL), lambda i: (0, 0))],
        out_specs=[row, row],
        out_shape=[jax.ShapeDtypeStruct((T, D_MODEL), F32), jax.ShapeDtypeStruct((T, D_MODEL), BF16)],
        compiler_params=_params(("parallel",)),
        name="merge",
    )(x, hg, rg, att, proj, proj, proj, wb, wo, ng)


def _ffn_kernel(h_ref, x_ref, w1_ref, w3_ref, w2_ref, o_ref, acc_ref):
    f = pl.program_id(1)
    h = h_ref[...]
    h1 = _dot(h, w1_ref[...])
    y = _dot((h1 * _sigmoid(h1) * _dot(h, w3_ref[...])).astype(BF16), w2_ref[...])

    @pl.when(f == 0)
    def _():
        acc_ref[...] = x_ref[...] + y

    @pl.when(f > 0)
    def _():
        acc_ref[...] += y

    @pl.when(f == pl.num_programs(1) - 1)
    def _():
        o_ref[...] = acc_ref[...]


def _ffn_dense(h2, x1, w1, w3, w2):
    T = h2.shape[0]
    tm, tf = FFN_TM, FFN_TF
    return pl.pallas_call(
        _ffn_kernel,
        grid=(T // tm, FFN_DENSE // tf),
        in_specs=[pl.BlockSpec((tm, D_MODEL), lambda i, f: (i, 0)),
                  pl.BlockSpec((tm, D_MODEL), lambda i, f: (i, 0)),
                  pl.BlockSpec((D_MODEL, tf), lambda i, f: (0, f)),
                  pl.BlockSpec((D_MODEL, tf), lambda i, f: (0, f)),
                  pl.BlockSpec((tf, D_MODEL), lambda i, f: (f, 0))],
        out_specs=pl.BlockSpec((tm, D_MODEL), lambda i, f: (i, 0)),
        out_shape=jax.ShapeDtypeStruct((T, D_MODEL), F32),
        scratch_shapes=[pltpu.VMEM((tm, D_MODEL), F32)],
        compiler_params=_params(("parallel", "arbitrary")),
        name="ffn_dense",
    )(h2, x1, w1, w3, w2)


def _router_kernel(h_ref, wr_ref, o_ref, cnt_ref, base_ref):
    tm = ROUTER_TM

    @pl.when(pl.program_id(0) == 0)
    def _():
        base_ref[...] = jnp.zeros_like(base_ref)

    lane = lax.broadcasted_iota(jnp.int32, (tm, LANES), 1)
    logits = jnp.where(lane < N_EXPERTS, _dot(h_ref[...], wr_ref[...]), -jnp.inf)
    m1 = jnp.max(logits, axis=-1, keepdims=True)
    i1 = jnp.min(jnp.where(logits == m1, lane, LANES), axis=-1, keepdims=True)
    rest = jnp.where(lane == i1, -jnp.inf, logits)
    m2 = jnp.max(rest, axis=-1, keepdims=True)
    i2 = jnp.min(jnp.where(rest == m2, lane, LANES), axis=-1, keepdims=True)
    e21 = jnp.exp(m2 - m1)
    g1 = 1.0 / (1.0 + e21)
    g2 = e21 / (1.0 + e21)

    onehot = jnp.where(lane == i1, 1.0, jnp.where(lane == i2, 1.0, 0.0))
    ti = lax.broadcasted_iota(jnp.int32, (tm, tm), 0)
    si = lax.broadcasted_iota(jnp.int32, (tm, tm), 1)
    before = jnp.where(si < ti, 1.0, 0.0).astype(BF16)
    pos = _dot(before, onehot.astype(BF16)) + base_ref[0:1, :]
    r1 = jnp.sum(jnp.where(lane == i1, pos, 0.0), axis=-1, keepdims=True)
    r2 = jnp.sum(jnp.where(lane == i2, pos, 0.0), axis=-1, keepdims=True)
    total = base_ref[0:1, :] + jnp.sum(onehot, axis=0, keepdims=True)
    base_ref[...] = jnp.broadcast_to(total, base_ref.shape)
    cnt_ref[...] = jnp.broadcast_to(total, cnt_ref.shape)

    out = jnp.where(lane == 0, i1.astype(F32), 0.0)
    out = jnp.where(lane == 1, i2.astype(F32), out)
    out = jnp.where(lane == 2, g1, out)
    out = jnp.where(lane == 3, g2, out)
    out = jnp.where(lane == 4, r1, out)
    out = jnp.where(lane == 5, r2, out)
    o_ref[...] = out


def _router(h2, wr):
    T = h2.shape[0]
    tm = ROUTER_TM
    return pl.pallas_call(
        _router_kernel,
        grid=(T // tm,),
        in_specs=[pl.BlockSpec((tm, D_MODEL), lambda i: (i, 0)),
                  pl.BlockSpec((D_MODEL, LANES), lambda i: (0, 0))],
        out_specs=[pl.BlockSpec((tm, LANES), lambda i: (i, 0)),
                   pl.BlockSpec((SUBLANES, LANES), lambda i: (0, 0))],
        out_shape=[jax.ShapeDtypeStruct((T, LANES), F32), jax.ShapeDtypeStruct((SUBLANES, LANES), F32)],
        scratch_shapes=[pltpu.VMEM((SUBLANES, LANES), F32)],
        compiler_params=_params(("arbitrary",)),
        name="moe_router",
    )(h2, wr)


def _expert_kernel(be_ref, nu_ref, x_ref, w1_ref, w3_ref, w2_ref, o_ref, acc_ref):
    i = pl.program_id(0)
    f = pl.program_id(1)

    @pl.when(f == 0)
    def _():
        acc_ref[...] = jnp.zeros_like(acc_ref)

    @pl.when(i < nu_ref[0])
    def _():
        x = x_ref[...]
        h1 = _dot(x, w1_ref[...])
        acc_ref[...] += _dot((h1 * _sigmoid(h1) * _dot(x, w3_ref[...])).astype(BF16), w2_ref[...])

    @pl.when(f == pl.num_programs(1) - 1)
    def _():
        o_ref[...] = acc_ref[...].astype(BF16)


def _experts(rows, block_expert, n_used, w1, w3, w2):
    n_rows = rows.shape[0]
    R, tf = MOE_ROWS, MOE_TF
    grid_spec = pltpu.PrefetchScalarGridSpec(
        num_scalar_prefetch=2,
        grid=(n_rows // R, FFN_EXPERT // tf),
        in_specs=[pl.BlockSpec((R, D_MODEL), lambda i, f, be, nu: (i, 0)),
                  pl.BlockSpec((None, D_MODEL, tf), lambda i, f, be, nu: (be[i], 0, f)),
                  pl.BlockSpec((None, D_MODEL, tf), lambda i, f, be, nu: (be[i], 0, f)),
                  pl.BlockSpec((None, tf, D_MODEL), lambda i, f, be, nu: (be[i], f, 0))],
        out_specs=pl.BlockSpec((R, D_MODEL), lambda i, f, be, nu: (i, 0)),
        scratch_shapes=[pltpu.VMEM((R, D_MODEL), F32)],
    )
    return pl.pallas_call(
        _expert_kernel,
        grid_spec=grid_spec,
        out_shape=jax.ShapeDtypeStruct((n_rows, D_MODEL), BF16),
        compiler_params=_params(("parallel", "arbitrary")),
        name="moe_experts",
    )(block_expert, n_used, rows, w1, w3, w2)


def _combine_kernel(x_ref, y1_ref, y2_ref, info_ref, o_ref):
    info = info_ref[...]
    o_ref[...] = (x_ref[...] + info[:, 2:3] * y1_ref[...].astype(F32)
                  + info[:, 3:4] * y2_ref[...].astype(F32))


def _combine(x1, y1, y2, info):
    T = x1.shape[0]
    tm = COMBINE_TM
    row = pl.BlockSpec((tm, D_MODEL), lambda i: (i, 0))
    return pl.pallas_call(
        _combine_kernel,
        grid=(T // tm,),
        in_specs=[row, row, row, pl.BlockSpec((tm, LANES), lambda i: (i, 0))],
        out_specs=row,
        out_shape=jax.ShapeDtypeStruct((T, D_MODEL), F32),
        compiler_params=_params(("parallel",)),
        name="moe_combine",
    )(x1, y1, y2, info)


def _moe(h2, x1, wr, w1, w3, w2):
    T = h2.shape[0]
    R = MOE_ROWS
    n_blocks = -(-(T * TOP_K) // R) + N_EXPERTS
    info, cnt = _router(h2, wr)
    expert = info[:, 0:2].astype(jnp.int32)
    rank = info[:, 4:6].astype(jnp.int32)
    counts = cnt[0, :N_EXPERTS].astype(jnp.int32)
    blocks_per = (counts + R - 1) // R
    bend = jnp.cumsum(blocks_per)
    pstart = (bend - blocks_per) * R
    dest = pstart[expert] + rank
    n_used = bend[-1:]
    block_expert = jnp.minimum(
        jnp.searchsorted(bend, jnp.arange(n_blocks, dtype=jnp.int32), side='right'),
        N_EXPERTS - 1).astype(jnp.int32)
    token = jnp.broadcast_to(jnp.arange(T, dtype=jnp.int32)[:, None], (T, TOP_K))
    tok_of_row = jnp.zeros((n_blocks * R,), jnp.int32).at[dest.reshape(-1)].set(
        token.reshape(-1), unique_indices=True, mode='promise_in_bounds')
    rows = h2.at[tok_of_row].get(mode='promise_in_bounds')
    y = _experts(rows, block_expert, n_used, w1, w3, w2)
    y1 = y.at[dest[:, 0]].get(mode='promise_in_bounds')
    y2 = y.at[dest[:, 1]].get(mode='promise_in_bounds')
    return _combine(x1, y1, y2, info)


def _pad_rows(rows, n):
    rows = jnp.stack(rows, axis=-2)
    pad = [(0, 0)] * (rows.ndim - 2) + [(0, n - rows.shape[-2]), (0, 0)]
    return jnp.pad(rows, pad)


def _hgrn_params(lb, norm_g):
    lb = lb.reshape(2, HG_HEADS, HG_DK)
    rows = []
    for d in range(2):
        rows += [jnp.log(jnp.maximum(lb[d], LB_EPS)), jnp.log1p(-lb[d]), 1.0 - lb[d]]
    rows.append(norm_g.reshape(HG_HEADS, HG_DK))
    return _pad_rows(rows, SUBLANES)


def _rg_params(conv_w, conv_b, b_a, b_x, lam):
    ng = RG_WIDTH // LANES
    g = lambda t: t.reshape(ng, LANES)
    rows = [g(conv_w[j]) for j in range(RG_CONV_W)] + [g(conv_b)]
    rows += [g(b_a[0]), g(b_x[0]), g(b_a[1]), g(b_x[1]), g(lam[0]), g(lam[1])]
    return _pad_rows(rows, 2 * SUBLANES)


def _rg_gate_weights(w_a, w_x):
    ng = RG_WIDTH // LANES
    per = RG_BLOCKS // ng

    def dense(w):
        w = w.reshape(ng, per, RG_BLOCK_W, RG_BLOCK_W)
        eye = jnp.eye(per, dtype=w.dtype)
        return jnp.einsum('gpcd,pq->gpcqd', w, eye).reshape(ng, LANES, LANES)

    return jnp.concatenate([dense(w_a[0]), dense(w_x[0]), dense(w_a[1]), dense(w_x[1])], axis=-1).astype(BF16)


def _attn_params(q_g, k_g):
    H = ATT_HEADS_PER_GROUP
    groups = []
    for gi, (_, dil) in enumerate(ATT_GROUPS):
        heads = jnp.arange(gi * H + 1, (gi + 1) * H + 1, dtype=F32)
        slopes = 2.0 ** (-8.0 * heads / ATT_HEADS) * dil
        rows = [jnp.broadcast_to(q_g * (ATT_DH ** -0.5), (H, LANES)), jnp.broadcast_to(k_g, (H, LANES)),
                jnp.broadcast_to(slopes[:, None], (H, LANES))]
        groups.append(_pad_rows(rows, SUBLANES))
    return jnp.stack(groups, axis=1)


def _trunk(x3, p):
    B, S, _ = x3.shape
    x = x3.reshape(B * S, D_MODEL)
    sm = jax.nn.softmax(p['hg_lb_logits'].astype(F32), axis=1)
    lower_bounds = jnp.cumsum(sm, axis=1) - sm[:, :1]
    for l in range(DEPTH):
        proj = _inproj(x, p['norm_mix_g'][l][None], p['w_in'][l].astype(BF16))
        hg = _hgrn(proj, _hgrn_params(lower_bounds[:, l], p['hg_norm_g'][l]), B, S)
        rg = _rglru(proj,
                    _rg_params(p['rg_conv_w'][l], p['rg_conv_b'][l], p['rg_b_a'][l], p['rg_b_x'][l],
                               p['rg_lambda'][l]),
                    _rg_gate_weights(p['rg_w_a'][l], p['rg_w_x'][l]), B, S)
        att = _attention(proj, _attn_params(p['attn_q_g'][l], p['attn_k_g'][l]), B, S)
        x1, h2 = _merge(x, proj, hg, rg, att, p['w_branch'][l].astype(BF16), p['w_out'][l].astype(BF16),
                        p['norm_ffn_g'][l][None])
        j = l // 2
        if l % 2 == 0:
            x = _ffn_dense(h2, x1, p['ffn_w1'][j].astype(BF16), p['ffn_w3'][j].astype(BF16),
                           p['ffn_w2'][j].astype(BF16))
        else:
            wr = jnp.pad(p['moe_router'][j], ((0, 0), (0, LANES - N_EXPERTS))).astype(BF16)
            x = _moe(h2, x1, wr, p['moe_w1'][j].astype(BF16), p['moe_w3'][j].astype(BF16),
                     p['moe_w2'][j].astype(BF16))
    return x.reshape(B, S, D_MODEL)


def kernel(x_prompt, x_sample, norm_mix_g, w_in, hg_lb_logits, hg_norm_g, rg_conv_w, rg_conv_b, rg_w_a, rg_b_a,
           rg_w_x, rg_b_x, rg_lambda, attn_q_g, attn_k_g, w_branch, w_out, norm_ffn_g, ffn_w1, ffn_w3, ffn_w2,
           moe_router, moe_w1, moe_w3, moe_w2):
    p = dict(norm_mix_g=norm_mix_g, w_in=w_in, hg_lb_logits=hg_lb_logits, hg_norm_g=hg_norm_g,
             rg_conv_w=rg_conv_w, rg_conv_b=rg_conv_b, rg_w_a=rg_w_a, rg_b_a=rg_b_a, rg_w_x=rg_w_x,
             rg_b_x=rg_b_x, rg_lambda=rg_lambda, attn_q_g=attn_q_g, attn_k_g=attn_k_g,
             w_branch=w_branch, w_out=w_out, norm_ffn_g=norm_ffn_g, ffn_w1=ffn_w1, ffn_w3=ffn_w3,
             ffn_w2=ffn_w2, moe_router=moe_router, moe_w1=moe_w1, moe_w3=moe_w3, moe_w2=moe_w2)
    return (_trunk(x_prompt, p), _trunk(x_sample, p))
```

```python
import functools
import math

import jax
import jax.numpy as jnp
from jax import lax
from jax.experimental import pallas as pl
from jax.experimental.pallas import tpu as pltpu

F32 = jnp.float32
BF16 = jnp.bfloat16

LANES = 128
SUBLANES = 8
VMEM_LIMIT_BYTES = 56 * 1024 * 1024

D_MODEL = 1024
DEPTH = 2
BRANCH_W = 512
HG_HEADS = 4
HG_DK = 128
HG_CHUNK = 64
HG_HEADS_PER_STEP = 2
HG_LEVELS = 6
LB_EPS = 1e-20
RG_WIDTH = 512
RG_BLOCKS = 8
RG_BLOCK_W = RG_WIDTH // RG_BLOCKS
RG_CONV_W = 4
RG_C = 8.0
RG_TILE = 256
RG_HALO = 8
ATT_GROUPS = ((128, 1), (512, 4), (2048, 16))
ATT_HEADS_PER_GROUP = 4
ATT_HEADS = ATT_HEADS_PER_GROUP * len(ATT_GROUPS)
ATT_DH = 128
ATT_RADIUS = 64
ATT_QBLK = 128
ATT_UNROLL = 8
ATT_TILE = 256
FFN_DENSE = 2816
N_EXPERTS = 8
TOP_K = 2
FFN_EXPERT = 3584
NORM_EPS = 1e-6
NEG_INF = -1e30
LOG2_E = 1.4426950408889634

CB_HQ, CB_HF, CB_HB, CB_HV, CB_HG = 0, 4, 8, 12, 16
CB_RX, CB_RGATE = 20, 24
CB_AQ, CB_AK, CB_AV = 28, 40, 52
CB_GATES = 64
IN_COLS = 11264
N_CB = IN_COLS // LANES

INPROJ_TM, INPROJ_TN = 1024, 1024
MERGE_TM = 512
FFN_TM, FFN_TF = 512, 2816
MOE_ROWS, MOE_TF = 512, 1792
ROUTER_TM = 512
COMBINE_TM = 512


def _params(sem):
    return pltpu.CompilerParams(dimension_semantics=sem, vmem_limit_bytes=VMEM_LIMIT_BYTES)


def _dot(a, b):
    return jnp.dot(a, b, preferred_element_type=F32)


def _dot_nt(a, b):
    return lax.dot_general(a, b, (((1,), (1,)), ((), ())), preferred_element_type=F32)


def _sigmoid(x):
    return jax.nn.sigmoid(x)


def _lockstep(gens):
    results = [None] * len(gens)
    active = list(range(len(gens)))
    while active:
        for idx in list(active):
            try:
                next(gens[idx])
            except StopIteration as stop:
                results[idx] = stop.value
                active.remove(idx)
    return results


def _inproj_kernel(x_ref, g_ref, w_ref, o_ref, h_ref):
    @pl.when(pl.program_id(1) == 0)
    def _():
        x = x_ref[...]
        ms = jnp.mean(x * x, axis=-1, keepdims=True)
        h_ref[...] = (x * lax.rsqrt(ms + NORM_EPS) * g_ref[...]).astype(BF16)

    acc = _dot(h_ref[...], w_ref[...])
    for c in range(INPROJ_TN // LANES):
        o_ref[c] = acc[:, c * LANES:(c + 1) * LANES].astype(BF16)


def _inproj(x, g, w):
    T = x.shape[0]
    return pl.pallas_call(
        _inproj_kernel,
        grid=(T // INPROJ_TM, IN_COLS // INPROJ_TN),
        in_specs=[
            pl.BlockSpec((INPROJ_TM, D_MODEL), lambda i, j: (i, 0)),
            pl.BlockSpec((1, D_MODEL), lambda i, j: (0, 0)),
            pl.BlockSpec((D_MODEL, INPROJ_TN), lambda i, j: (0, j)),
        ],
        out_specs=pl.BlockSpec((INPROJ_TN // LANES, INPROJ_TM, LANES), lambda i, j: (j, i, 0)),
        out_shape=jax.ShapeDtypeStruct((N_CB, T, LANES), BF16),
        scratch_shapes=[pltpu.VMEM((INPROJ_TM, D_MODEL), BF16)],
        compiler_params=_params(("parallel", "arbitrary")),
        name="inproj",
    )(x, g, w)


def _hgrn_chain(q, vb, z, c0, c1, oml, code, sign_ref, states, slot, reverse):
    C = HG_CHUNK
    nv = C // SUBLANES
    e = jnp.exp(-jnp.abs(z))
    log_sig = jnp.minimum(z, 0.0) - jnp.log(1.0 + e)
    t = c1 + log_sig
    log_f = jnp.maximum(c0, t) + jnp.log(1.0 + jnp.exp(-jnp.abs(c0 - t)))
    k = oml * (jnp.where(z >= 0.0, e, 1.0) / (1.0 + e))
    sc_diag = _dot_nt(q.astype(BF16), k.astype(BF16))
    vt = vb.astype(F32).T.astype(BF16)
    yield
    x3 = log_f.reshape(nv, SUBLANES, LANES)
    sub3 = lax.broadcasted_iota(jnp.int32, (nv, SUBLANES, LANES), 1)
    d = 1
    while d < SUBLANES:
        if reverse:
            x3 = x3 + jnp.where(sub3 < SUBLANES - d, pltpu.roll(x3, SUBLANES - d, 1), 0.0)
        else:
            x3 = x3 + jnp.where(sub3 >= d, pltpu.roll(x3, d, 1), 0.0)
        d *= 2
    edge = 0 if reverse else SUBLANES - 1
    tot = jnp.broadcast_to(x3[:, edge:edge + 1, :], (nv, SUBLANES, LANES))
    groups = [None] * nv
    run = None
    for g in (range(nv - 1, -1, -1) if reverse else range(nv)):
        groups[g] = x3[g] if run is None else x3[g] + run
        run = tot[g] if run is None else run + tot[g]
    a3 = jnp.stack(groups) * LOG2_E
    a = a3.reshape(C, LANES)
    sub = lax.broadcasted_iota(jnp.int32, (C, LANES), 0) & (SUBLANES - 1)
    yield
    q_in = (q * jnp.exp2(a)).astype(BF16)
    a_last = a[0:1, :] if reverse else a[C - 1:C, :]
    k_end = (k * jnp.exp2(a_last - a)).astype(BF16)
    st = states[slot]
    o_inter = _dot_nt(q_in, st.astype(BF16))
    states[slot] = st * jnp.exp2(a_last) + _dot(vt, k_end)
    yield

    def pick(r):
        return jnp.broadcast_to(a3[:, r:r + 1, :], (nv, SUBLANES, LANES)).reshape(C, LANES)

    level_scores = []
    for b in range(HG_LEVELS):
        h = 1 << b
        m = h if reverse else h - 1
        if h == 1:
            if reverse:
                ref = jnp.where((sub & 1) == 0, pltpu.roll(a, C - 1, 0), a)
            else:
                ref = jnp.where((sub & 1) == 1, pltpu.roll(a, 1, 0), a)
        elif 2 * h < SUBLANES:
            ref = jnp.where(sub < 2 * h, pick(m), pick(m + 2 * h))
        elif 2 * h == SUBLANES:
            ref = pick(m)
        else:
            pieces = [jnp.broadcast_to(a[blk * 2 * h + m:blk * 2 * h + m + 1, :], (2 * h, LANES))
                      for blk in range(C // (2 * h))]
            ref = jnp.concatenate(pieces, axis=0) if len(pieces) > 1 else pieces[0]
        eb = jnp.exp2(((ref - a) if reverse else (a - ref)) * sign_ref[b])
        level_scores.append(_dot_nt((q * eb).astype(BF16), (k * eb).astype(BF16)))
        yield
    scores = jnp.where(code == HG_LEVELS, sc_diag, 0.0)
    for b in range(HG_LEVELS):
        scores = jnp.where(code == b, level_scores[b], scores)
    o = o_inter + _dot(scores.astype(BF16), vb)
    yield
    return o


def _hgrn_kernel(*refs, S):
    C = HG_CHUNK
    n = S // C
    nh = HG_HEADS_PER_STEP
    head_refs = [refs[5 * hh:5 * hh + 5] for hh in range(nh)]
    par_ref, o_ref, acc_ref, code_ref, sign_ref = refs[5 * nh:]

    ti = lax.broadcasted_iota(jnp.int32, (C, C), 0)
    si = lax.broadcasted_iota(jnp.int32, (C, C), 1)
    x = ti ^ si
    lvl = jnp.zeros((C, C), jnp.int32)
    for b in range(1, HG_LEVELS):
        lvl = lvl + jnp.where(x >= (1 << b), 1, 0)
    diag = jnp.where(ti == si, HG_LEVELS, -1)
    code_ref[0] = jnp.where(ti > si, lvl, diag)
    code_ref[1] = jnp.where(ti < si, lvl, diag)
    row = lax.broadcasted_iota(jnp.int32, (C, LANES), 0)
    for b in range(HG_LEVELS):
        sign_ref[b] = jnp.where((row & (1 << b)) != 0, 1.0, -1.0)

    def chain(ci, hh, d, states):
        q_ref, zf_ref, zb_ref, v_ref, _ = head_refs[hh]
        z_ref = zb_ref if d else zf_ref
        par = par_ref[hh]
        r0 = pl.multiple_of(ci * C, C)
        return _hgrn_chain(q_ref[pl.ds(r0, C), :].astype(F32), v_ref[pl.ds(r0, C), :],
                           z_ref[pl.ds(r0, C), :].astype(F32), par[3 * d:3 * d + 1], par[3 * d + 1:3 * d + 2],
                           par[3 * d + 2:3 * d + 3], code_ref[d], sign_ref, states, 2 * hh + d, d == 1)

    def make_body(accumulate):
        def body(i, carry):
            states = list(carry)
            jobs = []
            for hh in range(nh):
                jobs += [(2 * i, hh, 0), (2 * i + 1, hh, 0), (n - 1 - 2 * i, hh, 1), (n - 2 - 2 * i, hh, 1)]
            outs = _lockstep([chain(ci, hh, d, states) for ci, hh, d in jobs])
            for (ci, hh, _), o in zip(jobs, outs):
                r0 = pl.multiple_of(ci * C, C)
                if accumulate:
                    acc_ref[hh, pl.ds(r0, C), :] += o
                else:
                    acc_ref[hh, pl.ds(r0, C), :] = o
            return tuple(states)
        return body

    zero = jnp.zeros((LANES, HG_DK), F32)
    carry = lax.fori_loop(0, n // 4, make_body(False), (zero,) * (2 * nh))
    lax.fori_loop(n // 4, n // 2, make_body(True), carry)

    ft = 256

    def fin(i, _):
        r0 = pl.multiple_of(i * ft, ft)
        for hh in range(nh):
            o = acc_ref[hh, pl.ds(r0, ft), :]
            y = o * lax.rsqrt(jnp.mean(o * o, axis=-1, keepdims=True) + NORM_EPS) * par_ref[hh][6:7]
            g = head_refs[hh][4][pl.ds(r0, ft), :].astype(F32)
            o_ref[hh, pl.ds(r0, ft), :] = (y * (g * _sigmoid(g))).astype(BF16)
        return 0

    lax.fori_loop(0, S // ft, fin, 0)


def _hgrn(proj, par, B, S):
    nh = HG_HEADS_PER_STEP

    def spec(cb0, hh):
        return pl.BlockSpec((None, S, LANES), lambda b, h: (cb0 + nh * h + hh, b, 0))

    in_specs = []
    for hh in range(nh):
        in_specs += [spec(CB_HQ, hh), spec(CB_HF, hh), spec(CB_HB, hh), spec(CB_HV, hh), spec(CB_HG, hh)]
    in_specs.append(pl.BlockSpec((nh, SUBLANES, LANES), lambda b, h: (h, 0, 0)))
    return pl.pallas_call(
        functools.partial(_hgrn_kernel, S=S),
        grid=(B, HG_HEADS // nh),
        in_specs=in_specs,
        out_specs=pl.BlockSpec((nh, S, LANES), lambda b, h: (h, b, 0)),
        out_shape=jax.ShapeDtypeStruct((HG_HEADS, B * S, LANES), BF16),
        scratch_shapes=[pltpu.VMEM((nh, S, LANES), F32), pltpu.VMEM((2, HG_CHUNK, HG_CHUNK), jnp.int32),
                        pltpu.VMEM((HG_LEVELS, HG_CHUNK, LANES), F32)],
        compiler_params=_params(("parallel", "parallel")),
        name="hgrn2",
    )(*([proj] * (5 * nh)), par)


def _linear_scan_tile(a, u, carry, reverse):
    n = a.shape[0]
    nv = n // SUBLANES
    a3 = a.reshape(nv, SUBLANES, LANES)
    u3 = u.reshape(nv, SUBLANES, LANES)
    sub = lax.broadcasted_iota(jnp.int32, (nv, SUBLANES, LANES), 1)
    d = 1
    while d < SUBLANES:
        if reverse:
            keep = sub < SUBLANES - d
            shift = SUBLANES - d
        else:
            keep = sub >= d
            shift = d
        a_s = jnp.where(keep, pltpu.roll(a3, shift, 1), 1.0)
        u_s = jnp.where(keep, pltpu.roll(u3, shift, 1), 0.0)
        u3 = a3 * u_s + u3
        a3 = a3 * a_s
        d *= 2
        yield
    edge = 0 if reverse else SUBLANES - 1
    a_tot = jnp.broadcast_to(a3[:, edge:edge + 1, :], a3.shape)
    u_tot = jnp.broadcast_to(u3[:, edge:edge + 1, :], u3.shape)
    groups = [None] * nv
    for g in (range(nv - 1, -1, -1) if reverse else range(nv)):
        groups[g] = u3[g] + a3[g] * carry
        carry = u_tot[g] + a_tot[g] * carry
        if g % 4 == 0:
            yield
    return jnp.stack(groups).reshape(n, LANES), carry


def _rg_kernel(x_ref, gate_ref, par_ref, w_ref, o_ref, xpad_ref, xc_ref, h_ref, *, S):
    TS = RG_TILE
    n = S // TS
    par = par_ref[...]
    conv_b = par[4:5]

    xpad_ref[pl.ds(0, RG_HALO), :] = jnp.zeros((RG_HALO, LANES), F32)
    xpad_ref[pl.ds(S + RG_HALO, RG_HALO), :] = jnp.zeros((RG_HALO, LANES), F32)

    def copy(i, _):
        r0 = pl.multiple_of(i * TS, TS)
        xpad_ref[pl.ds(r0 + RG_HALO, TS), :] = x_ref[pl.ds(r0, TS), :].astype(F32)
        return 0

    lax.fori_loop(0, n, copy, 0)

    def conv(i, _):
        r0 = pl.multiple_of(i * TS, TS)
        xc = conv_b
        for j in range(RG_CONV_W):
            off = RG_HALO - RG_CONV_W // 2 + j
            xc = xc + par[j:j + 1] * xpad_ref[pl.ds(r0 + off, TS), :]
        xc_ref[pl.ds(r0, TS), :] = xc
        return 0

    lax.fori_loop(0, n, conv, 0)

    def tile(i, carries, d, final):
        r0 = pl.multiple_of(i * TS, TS)
        xc = xc_ref[pl.ds(r0, TS), :]
        gts = _dot(xc.astype(BF16), w_ref[:, d * 2 * LANES:(d + 1) * 2 * LANES])
        yield
        r = _sigmoid(gts[:, :LANES] + par[5 + 2 * d:6 + 2 * d])
        ig = _sigmoid(gts[:, LANES:] + par[6 + 2 * d:7 + 2 * d])
        lam = par[9 + d:10 + d]
        softplus_neg_lam = jnp.maximum(-lam, 0.0) + jnp.log1p(jnp.exp(-jnp.abs(lam)))
        log_a = (-RG_C) * r * softplus_neg_lam
        a = jnp.exp(log_a)
        y = 1.0 - a * a
        u = jnp.where(y > 0.0, y * lax.rsqrt(y), 0.0) * (ig * xc)
        yield
        h, carries[d] = yield from _linear_scan_tile(a, u, carries[d], reverse=(d == 1))
        yield
        if final:
            gate = gate_ref[pl.ds(r0, TS), :].astype(F32)
            o_ref[pl.ds(r0, TS), :] = (jax.nn.gelu(gate) * (h_ref[pl.ds(r0, TS), :] + h)).astype(BF16)
        else:
            h_ref[pl.ds(r0, TS), :] = h

    def make_body(final):
        def body(i, carry):
            carries = list(carry)
            _lockstep([tile(i, carries, 0, final), tile(n - 1 - i, carries, 1, final)])
            return tuple(carries)
        return body

    zero = jnp.zeros((SUBLANES, LANES), F32)
    carry = lax.fori_loop(0, n // 2, make_body(False), (zero, zero))
    lax.fori_loop(n // 2, n, make_body(True), carry)


def _rglru(proj, par, w4, B, S):
    ng = RG_WIDTH // LANES

    def spec(cb0):
        return pl.BlockSpec((None, S, LANES), lambda b, g: (cb0 + g, b, 0))

    return pl.pallas_call(
        functools.partial(_rg_kernel, S=S),
        grid=(B, ng),
        in_specs=[spec(CB_RX), spec(CB_RGATE),
                  pl.BlockSpec((None, 2 * SUBLANES, LANES), lambda b, g: (g, 0, 0)),
                  pl.BlockSpec((None, LANES, 4 * LANES), lambda b, g: (g, 0, 0))],
        out_specs=pl.BlockSpec((None, S, LANES), lambda b, g: (g, b, 0)),
        out_shape=jax.ShapeDtypeStruct((ng, B * S, LANES), BF16),
        scratch_shapes=[pltpu.VMEM((S + 2 * RG_HALO, LANES), F32), pltpu.VMEM((S, LANES), F32),
                        pltpu.VMEM((S, LANES), F32)],
        compiler_params=_params(("parallel", "parallel")),
        name="rglru",
    )(proj, proj, par, w4)


def _attn_kernel(q0_ref, k0_ref, v0_ref, q1_ref, k1_ref, v1_ref, q2_ref, k2_ref, v2_ref, par_ref, o_ref,
                 perm_ref, qd_ref, kd_ref, vd_ref, bias_ref, og_ref, lg_ref, *, S):
    refs = ((q0_ref, k0_ref, v0_ref), (q1_ref, k1_ref, v1_ref), (q2_ref, k2_ref, v2_ref))
    ct = ATT_TILE

    for g, (_, dil) in enumerate(ATT_GROUPS):
        q_ref, k_ref, v_ref = refs[g]
        par = par_ref[g]
        qg, kg, slope = par[0:1], par[1:2], par[2:3, 0:1]
        L = S // dil
        Q = min(ATT_QBLK, L)
        KW = min(Q + 2 * ATT_RADIUS, L)
        nq = L // Q
        nq_shift = int(math.log2(nq))

        for var in range(3):
            qi = var * ATT_RADIUS + lax.broadcasted_iota(jnp.int32, (Q, KW), 0)
            rel = jnp.abs(qi - lax.broadcasted_iota(jnp.int32, (Q, KW), 1))
            bias_ref[var, :Q, :KW] = jnp.where(rel <= ATT_RADIUS, -slope * rel.astype(F32), NEG_INF)

        per = ct // dil
        if dil > 1:
            pi = lax.broadcasted_iota(jnp.int32, (ct, ct), 0)
            pj = lax.broadcasted_iota(jnp.int32, (ct, ct), 1)
            src_tok = (pi & (per - 1)) * dil + lax.shift_right_logical(pi, int(math.log2(per)))
            perm_ref[...] = jnp.where(pj == src_tok, 1.0, 0.0).astype(BF16)

        def prep_tile(i, q_ref=q_ref, k_ref=k_ref, v_ref=v_ref, qg=qg, kg=kg, dil=dil, per=per, L=L):
            r0 = pl.multiple_of(i * ct, ct)
            xs = (q_ref[pl.ds(r0, ct), :].astype(F32), k_ref[pl.ds(r0, ct), :].astype(F32))
            sums = [jnp.sum(x * x, axis=-1, keepdims=True) for x in xs]
            yield
            tiles = [(x * lax.rsqrt(ss * (1.0 / LANES) + NORM_EPS) * gain).astype(BF16)
                     for x, ss, gain in zip(xs, sums, (qg, kg))]
            tiles.append(v_ref[pl.ds(r0, ct), :])
            if dil > 1:
                y = _dot(perm_ref[...], jnp.concatenate(tiles, axis=1))
                tiles = [y[:, c * LANES:(c + 1) * LANES] for c in range(3)]
                yield
            for dst_ref, y in zip((qd_ref, kd_ref, vd_ref), tiles):
                if dil == 1:
                    dst_ref[pl.ds(r0, ct), :] = y
                else:
                    y = y.astype(BF16)
                    for r in range(dil):
                        dst_ref[pl.ds(pl.multiple_of(r * L + i * per, 16), per), :] = y[r * per:(r + 1) * per, :]

        prep_unroll = min(4, S // ct)

        def prep(i, _, prep_tile=prep_tile, prep_unroll=prep_unroll):
            _lockstep([prep_tile(i * prep_unroll + u) for u in range(prep_unroll)])
            return 0

        lax.fori_loop(0, S // ct // prep_unroll, prep, 0)

        def rows(start, size, dil=dil):
            if dil == 1:
                return pl.ds(start, size)
            return pl.ds(start, size, stride=dil)

        unroll = min(ATT_UNROLL, dil * nq)

        def qblock(it, g=g, dil=dil, L=L, Q=Q, KW=KW, nq=nq, nq_shift=nq_shift, rows=rows):
            r = lax.shift_right_logical(it, nq_shift)
            m0 = (it & (nq - 1)) * Q
            ks = jnp.clip(m0 - ATT_RADIUS, 0, L - KW)
            var = lax.shift_right_logical(m0 - ks, int(math.log2(ATT_RADIUS)))
            base = r * L
            qn = qd_ref[pl.ds(pl.multiple_of(base + m0, 16), Q), :]
            kk = kd_ref[pl.ds(pl.multiple_of(base + ks, 16), KW), :]
            s = _dot_nt(qn, kk)
            yield
            s = s + bias_ref[var, :Q, :KW]
            m = jnp.max(s, axis=-1, keepdims=True)
            yield
            pb = jnp.exp(s - m).astype(BF16)
            vv = vd_ref[pl.ds(pl.multiple_of(base + ks, 16), KW), :]
            ol = _dot(pb, jnp.concatenate([vv, jnp.ones((KW, LANES), BF16)], axis=1))
            yield
            l = ol[:, LANES:]
            og_ref[g, rows(m0 * dil + r, Q), :] = ol[:, :LANES] / l
            lg_ref[g, rows(m0 * dil + r, Q), :] = m + jnp.log(l)

        def qblocks(i, _, unroll=unroll, qblock=qblock):
            _lockstep([qblock(i * unroll + u) for u in range(unroll)])
            return 0

        lax.fori_loop(0, (dil * nq) // unroll, qblocks, 0)

    def merge(i, _):
        r0 = pl.multiple_of(i * ct, ct)
        l0, l1, l2 = (lg_ref[g, pl.ds(r0, ct), :] for g in range(3))
        m = jnp.maximum(jnp.maximum(l0, l1), l2)
        e0, e1, e2 = jnp.exp(l0 - m), jnp.exp(l1 - m), jnp.exp(l2 - m)
        o = (e0 * og_ref[0, pl.ds(r0, ct), :] + e1 * og_ref[1, pl.ds(r0, ct), :]
             + e2 * og_ref[2, pl.ds(r0, ct), :]) / (e0 + e1 + e2)
        o_ref[pl.ds(r0, ct), :] = o.astype(BF16)
        return 0

    lax.fori_loop(0, S // ct, merge, 0)


def _attention(proj, par, B, S):
    H = ATT_HEADS_PER_GROUP
    ngroups = len(ATT_GROUPS)

    def spec(cb0, g):
        return pl.BlockSpec((None, S, LANES), lambda b, j: (cb0 + g * H + j, b, 0))

    in_specs = []
    for g in range(ngroups):
        in_specs += [spec(CB_AQ, g), spec(CB_AK, g), spec(CB_AV, g)]
    in_specs.append(pl.BlockSpec((None, ngroups, SUBLANES, LANES), lambda b, j: (j, 0, 0, 0)))
    return pl.pallas_call(
        functools.partial(_attn_kernel, S=S),
        grid=(B, H),
        in_specs=in_specs,
        out_specs=pl.BlockSpec((None, S, LANES), lambda b, j: (j, b, 0)),
        out_shape=jax.ShapeDtypeStruct((H, B * S, LANES), BF16),
        scratch_shapes=[pltpu.VMEM((ATT_TILE, ATT_TILE), BF16),
                        pltpu.VMEM((S, LANES), BF16), pltpu.VMEM((S, LANES), BF16), pltpu.VMEM((S, LANES), BF16),
                        pltpu.VMEM((3, ATT_QBLK, ATT_QBLK + 2 * ATT_RADIUS), F32),
                        pltpu.VMEM((ngroups, S, LANES), F32), pltpu.VMEM((ngroups, S, LANES), F32)],
        compiler_params=_params(("parallel", "parallel")),
        name="attention",
    )(*([proj] * (3 * ngroups)), par)


def _cat_heads(ref):
    return jnp.concatenate([ref[c] for c in range(ref.shape[0])], axis=-1)


def _merge_kernel(x_ref, hg_ref, rg_ref, att_ref, ga_ref, gb_ref, gc_ref, wb_ref, wo_ref, ng_ref,
                  x1_ref, h2_ref):
    merged = _sigmoid(_cat_heads(ga_ref).astype(F32)) * _dot(_cat_heads(hg_ref), wb_ref[0])
    merged += _sigmoid(_cat_heads(gb_ref).astype(F32)) * _dot(_cat_heads(rg_ref), wb_ref[1])
    merged += _sigmoid(_cat_heads(gc_ref).astype(F32)) * _dot(_cat_heads(att_ref), wb_ref[2])
    x1 = x_ref[...] + _dot(merged.astype(BF16), wo_ref[...])
    x1_ref[...] = x1
    ms = jnp.mean(x1 * x1, axis=-1, keepdims=True)
    h2_ref[...] = (x1 * lax.rsqrt(ms + NORM_EPS) * ng_ref[...]).astype(BF16)


def _merge(x, proj, hg, rg, att, wb, wo, ng):
    T = x.shape[0]
    tm = MERGE_TM
    nb = BRANCH_W // LANES
    ngate = D_MODEL // LANES

    branch = pl.BlockSpec((nb, tm, LANES), lambda i: (0, i, 0))

    def gate(n):
        return pl.BlockSpec((ngate, tm, LANES), lambda i: (CB_GATES // ngate + n, i, 0))

    row = pl.BlockSpec((tm, D_MODEL), lambda i: (i, 0))
    return pl.pallas_call(
        _merge_kernel,
        grid=(T // tm,),
        in_specs=[row, branch, branch, branch, gate(0), gate(1), gate(2),
                  pl.BlockSpec((3, BRANCH_W, D_MODEL), lambda i: (0, 0, 0)),
                  pl.BlockSpec((D_MODEL, D_MODEL), lambda i: (0, 0)),
                  pl.BlockSpec((1, D_MODEL), lambda i: (0, 0))],
        out_specs=[row, row],
        out_shape=[jax.ShapeDtypeStruct((T, D_MODEL), F32), jax.ShapeDtypeStruct((T, D_MODEL), BF16)],
        compiler_params=_params(("parallel",)),
        name="merge",
    )(x, hg, rg, att, proj, proj, proj, wb, wo, ng)


def _ffn_kernel(h_ref, x_ref, w1_ref, w3_ref, w2_ref, o_ref, acc_ref):
    f = pl.program_id(1)
    h = h_ref[...]
    h1 = _dot(h, w1_ref[...])
    y = _dot((h1 * _sigmoid(h1) * _dot(h, w3_ref[...])).astype(BF16), w2_ref[...])

    @pl.when(f == 0)
    def _():
        acc_ref[...] = x_ref[...] + y

    @pl.when(f > 0)
    def _():
        acc_ref[...] += y

    @pl.when(f == pl.num_programs(1) - 1)
    def _():
        o_ref[...] = acc_ref[...]


def _ffn_dense(h2, x1, w1, w3, w2):
    T = h2.shape[0]
    tm, tf = FFN_TM, FFN_TF
    return pl.pallas_call(
        _ffn_kernel,
        grid=(T // tm, FFN_DENSE // tf),
        in_specs=[pl.BlockSpec((tm, D_MODEL), lambda i, f: (i, 0)),
                  pl.BlockSpec((tm, D_MODEL), lambda i, f: (i, 0)),
                  pl.BlockSpec((D_MODEL, tf), lambda i, f: (0, f), pipeline_mode=pl.Buffered(1)),
                  pl.BlockSpec((D_MODEL, tf), lambda i, f: (0, f), pipeline_mode=pl.Buffered(1)),
                  pl.BlockSpec((tf, D_MODEL), lambda i, f: (f, 0), pipeline_mode=pl.Buffered(1))],
        out_specs=pl.BlockSpec((tm, D_MODEL), lambda i, f: (i, 0)),
        out_shape=jax.ShapeDtypeStruct((T, D_MODEL), F32),
        scratch_shapes=[pltpu.VMEM((tm, D_MODEL), F32)],
        compiler_params=_params(("parallel", "arbitrary")),
        name="ffn_dense",
    )(h2, x1, w1, w3, w2)


def _router_kernel(h_ref, wr_ref, o_ref, cnt_ref, base_ref):
    tm = ROUTER_TM

    @pl.when(pl.program_id(0) == 0)
    def _():
        base_ref[...] = jnp.zeros_like(base_ref)

    lane = lax.broadcasted_iota(jnp.int32, (tm, LANES), 1)
    logits = jnp.where(lane < N_EXPERTS, _dot(h_ref[...], wr_ref[...]), -jnp.inf)
    m1 = jnp.max(logits, axis=-1, keepdims=True)
    i1 = jnp.min(jnp.where(logits == m1, lane, LANES), axis=-1, keepdims=True)
    rest = jnp.where(lane == i1, -jnp.inf, logits)
    m2 = jnp.max(rest, axis=-1, keepdims=True)
    i2 = jnp.min(jnp.where(rest == m2, lane, LANES), axis=-1, keepdims=True)
    e21 = jnp.exp(m2 - m1)
    g1 = 1.0 / (1.0 + e21)
    g2 = e21 / (1.0 + e21)

    onehot = jnp.where(lane == i1, 1.0, jnp.where(lane == i2, 1.0, 0.0))
    ti = lax.broadcasted_iota(jnp.int32, (tm, tm), 0)
    si = lax.broadcasted_iota(jnp.int32, (tm, tm), 1)
    before = jnp.where(si < ti, 1.0, 0.0).astype(BF16)
    pos = _dot(before, onehot.astype(BF16)) + base_ref[0:1, :]
    r1 = jnp.sum(jnp.where(lane == i1, pos, 0.0), axis=-1, keepdims=True)
    r2 = jnp.sum(jnp.where(lane == i2, pos, 0.0), axis=-1, keepdims=True)
    total = base_ref[0:1, :] + jnp.sum(onehot, axis=0, keepdims=True)
    base_ref[...] = jnp.broadcast_to(total, base_ref.shape)
    cnt_ref[...] = jnp.broadcast_to(total, cnt_ref.shape)

    out = jnp.where(lane == 0, i1.astype(F32), 0.0)
    out = jnp.where(lane == 1, i2.astype(F32), out)
    out = jnp.where(lane == 2, g1, out)
    out = jnp.where(lane == 3, g2, out)
    out = jnp.where(lane == 4, r1, out)
    out = jnp.where(lane == 5, r2, out)
    o_ref[...] = out


def _router(h2, wr):
    T = h2.shape[0]
    tm = ROUTER_TM
    return pl.pallas_call(
        _router_kernel,
        grid=(T // tm,),
        in_specs=[pl.BlockSpec((tm, D_MODEL), lambda i: (i, 0)),
                  pl.BlockSpec((D_MODEL, LANES), lambda i: (0, 0))],
        out_specs=[pl.BlockSpec((tm, LANES), lambda i: (i, 0)),
                   pl.BlockSpec((SUBLANES, LANES), lambda i: (0, 0))],
        out_shape=[jax.ShapeDtypeStruct((T, LANES), F32), jax.ShapeDtypeStruct((SUBLANES, LANES), F32)],
        scratch_shapes=[pltpu.VMEM((SUBLANES, LANES), F32)],
        compiler_params=_params(("arbitrary",)),
        name="moe_router",
    )(h2, wr)


def _expert_kernel(be_ref, nu_ref, x_ref, w1_ref, w3_ref, w2_ref, o_ref, acc_ref):
    i = pl.program_id(0)
    f = pl.program_id(1)

    @pl.when(f == 0)
    def _():
        acc_ref[...] = jnp.zeros_like(acc_ref)

    @pl.when(i < nu_ref[0])
    def _():
        x = x_ref[...]
        h1 = _dot(x, w1_ref[...])
        acc_ref[...] += _dot((h1 * _sigmoid(h1) * _dot(x, w3_ref[...])).astype(BF16), w2_ref[...])

    @pl.when(f == pl.num_programs(1) - 1)
    def _():
        o_ref[...] = acc_ref[...].astype(BF16)


def _experts(rows, block_expert, n_used, w1, w3, w2):
    n_rows = rows.shape[0]
    R, tf = MOE_ROWS, MOE_TF
    grid_spec = pltpu.PrefetchScalarGridSpec(
        num_scalar_prefetch=2,
        grid=(n_rows // R, FFN_EXPERT // tf),
        in_specs=[pl.BlockSpec((R, D_MODEL), lambda i, f, be, nu: (i, 0)),
                  pl.BlockSpec((None, D_MODEL, tf), lambda i, f, be, nu: (be[i], 0, f)),
                  pl.BlockSpec((None, D_MODEL, tf), lambda i, f, be, nu: (be[i], 0, f)),
                  pl.BlockSpec((None, tf, D_MODEL), lambda i, f, be, nu: (be[i], f, 0))],
        out_specs=pl.BlockSpec((R, D_MODEL), lambda i, f, be, nu: (i, 0)),
        scratch_shapes=[pltpu.VMEM((R, D_MODEL), F32)],
    )
    return pl.pallas_call(
        _expert_kernel,
        grid_spec=grid_spec,
        out_shape=jax.ShapeDtypeStruct((n_rows, D_MODEL), BF16),
        compiler_params=_params(("parallel", "arbitrary")),
        name="moe_experts",
    )(block_expert, n_used, rows, w1, w3, w2)


def _combine_kernel(x_ref, y1_ref, y2_ref, info_ref, o_ref):
    info = info_ref[...]
    o_ref[...] = (x_ref[...] + info[:, 2:3] * y1_ref[...].astype(F32)
                  + info[:, 3:4] * y2_ref[...].astype(F32))


def _combine(x1, y1, y2, info):
    T = x1.shape[0]
    tm = COMBINE_TM
    row = pl.BlockSpec((tm, D_MODEL), lambda i: (i, 0))
    return pl.pallas_call(
        _combine_kernel,
        grid=(T // tm,),
        in_specs=[row, row, row, pl.BlockSpec((tm, LANES), lambda i: (i, 0))],
        out_specs=row,
        out_shape=jax.ShapeDtypeStruct((T, D_MODEL), F32),
        compiler_params=_params(("parallel",)),
        name="moe_combine",
    )(x1, y1, y2, info)


def _moe(h2, x1, wr, w1, w3, w2):
    T = h2.shape[0]
    R = MOE_ROWS
    n_blocks = -(-(T * TOP_K) // R) + N_EXPERTS
    info, cnt = _router(h2, wr)
    expert = info[:, 0:2].astype(jnp.int32)
    rank = info[:, 4:6].astype(jnp.int32)
    counts = cnt[0, :N_EXPERTS].astype(jnp.int32)
    blocks_per = (counts + R - 1) // R
    bend = jnp.cumsum(blocks_per)
    pstart = (bend - blocks_per) * R
    dest = pstart[expert] + rank
    n_used = bend[-1:]
    block_expert = jnp.minimum(
        jnp.searchsorted(bend, jnp.arange(n_blocks, dtype=jnp.int32), side='right'),
        N_EXPERTS - 1).astype(jnp.int32)
    order = jnp.argsort(expert.reshape(-1), stable=True).astype(jnp.int32)
    row = jnp.arange(n_blocks * R, dtype=jnp.int32)
    row_expert = block_expert[row // R]
    j = row - pstart[row_expert]
    starts = jnp.cumsum(counts) - counts
    assignment = order[jnp.clip(starts[row_expert] + j, 0, T * TOP_K - 1)]
    tok_of_row = jnp.where(j < counts[row_expert], assignment // TOP_K, 0)
    rows = h2.at[tok_of_row].get(mode='promise_in_bounds')
    y = _experts(rows, block_expert, n_used, w1, w3, w2)
    y1 = y.at[dest[:, 0]].get(mode='promise_in_bounds')
    y2 = y.at[dest[:, 1]].get(mode='promise_in_bounds')
    return _combine(x1, y1, y2, info)


def _pad_rows(rows, n):
    rows = jnp.stack(rows, axis=-2)
    pad = [(0, 0)] * (rows.ndim - 2) + [(0, n - rows.shape[-2]), (0, 0)]
    return jnp.pad(rows, pad)


def _hgrn_params(lb, norm_g):
    lb = lb.reshape(2, HG_HEADS, HG_DK)
    rows = []
    for d in range(2):
        rows += [jnp.log(jnp.maximum(lb[d], LB_EPS)), jnp.log1p(-lb[d]), 1.0 - lb[d]]
    rows.append(norm_g.reshape(HG_HEADS, HG_DK))
    return _pad_rows(rows, SUBLANES)


def _rg_params(conv_w, conv_b, b_a, b_x, lam):
    ng = RG_WIDTH // LANES
    g = lambda t: t.reshape(ng, LANES)
    rows = [g(conv_w[j]) for j in range(RG_CONV_W)] + [g(conv_b)]
    rows += [g(b_a[0]), g(b_x[0]), g(b_a[1]), g(b_x[1]), g(lam[0]), g(lam[1])]
    return _pad_rows(rows, 2 * SUBLANES)


def _rg_gate_weights(w_a, w_x):
    ng = RG_WIDTH // LANES
    per = RG_BLOCKS // ng

    def dense(w):
        w = w.reshape(ng, per, RG_BLOCK_W, RG_BLOCK_W)
        eye = jnp.eye(per, dtype=w.dtype)
        return jnp.einsum('gpcd,pq->gpcqd', w, eye).reshape(ng, LANES, LANES)

    return jnp.concatenate([dense(w_a[0]), dense(w_x[0]), dense(w_a[1]), dense(w_x[1])], axis=-1).astype(BF16)


def _attn_params(q_g, k_g):
    H = ATT_HEADS_PER_GROUP
    groups = []
    for gi, (_, dil) in enumerate(ATT_GROUPS):
        heads = jnp.arange(gi * H + 1, (gi + 1) * H + 1, dtype=F32)
        slopes = 2.0 ** (-8.0 * heads / ATT_HEADS) * dil
        rows = [jnp.broadcast_to(q_g * (ATT_DH ** -0.5), (H, LANES)), jnp.broadcast_to(k_g, (H, LANES)),
                jnp.broadcast_to(slopes[:, None], (H, LANES))]
        groups.append(_pad_rows(rows, SUBLANES))
    return jnp.stack(groups, axis=1)


def _trunk(x3, p):
    B, S, _ = x3.shape
    x = x3.reshape(B * S, D_MODEL)
    sm = jax.nn.softmax(p['hg_lb_logits'].astype(F32), axis=1)
    lower_bounds = jnp.cumsum(sm, axis=1) - sm[:, :1]
    for l in range(DEPTH):
        proj = _inproj(x, p['norm_mix_g'][l][None], p['w_in'][l].astype(BF16))
        hg = _hgrn(proj, _hgrn_params(lower_bounds[:, l], p['hg_norm_g'][l]), B, S)
        rg = _rglru(proj,
                    _rg_params(p['rg_conv_w'][l], p['rg_conv_b'][l], p['rg_b_a'][l], p['rg_b_x'][l],
                               p['rg_lambda'][l]),
                    _rg_gate_weights(p['rg_w_a'][l], p['rg_w_x'][l]), B, S)
        att = _attention(proj, _attn_params(p['attn_q_g'][l], p['attn_k_g'][l]), B, S)
        x1, h2 = _merge(x, proj, hg, rg, att, p['w_branch'][l].astype(BF16), p['w_out'][l].astype(BF16),
                        p['norm_ffn_g'][l][None])
        j = l // 2
        if l % 2 == 0:
            x = _ffn_dense(h2, x1, p['ffn_w1'][j].astype(BF16), p['ffn_w3'][j].astype(BF16),
                           p['ffn_w2'][j].astype(BF16))
        else:
            wr = jnp.pad(p['moe_router'][j], ((0, 0), (0, LANES - N_EXPERTS))).astype(BF16)
            x = _moe(h2, x1, wr, p['moe_w1'][j].astype(BF16), p['moe_w3'][j].astype(BF16),
                     p['moe_w2'][j].astype(BF16))
    return x.reshape(B, S, D_MODEL)


def kernel(x_prompt, x_sample, norm_mix_g, w_in, hg_lb_logits, hg_norm_g, rg_conv_w, rg_conv_b, rg_w_a, rg_b_a,
           rg_w_x, rg_b_x, rg_lambda, attn_q_g, attn_k_g, w_branch, w_out, norm_ffn_g, ffn_w1, ffn_w3, ffn_w2,
           moe_router, moe_w1, moe_w3, moe_w2):
    p = dict(norm_mix_g=norm_mix_g, w_in=w_in, hg_lb_logits=hg_lb_logits, hg_norm_g=hg_norm_g,
             rg_conv_w=rg_conv_w, rg_conv_b=rg_conv_b, rg_w_a=rg_w_a, rg_b_a=rg_b_a, rg_w_x=rg_w_x,
             rg_b_x=rg_b_x, rg_lambda=rg_lambda, attn_q_g=attn_q_g, attn_k_g=attn_k_g,
             w_branch=w_branch, w_out=w_out, norm_ffn_g=norm_ffn_g, ffn_w1=ffn_w1, ffn_w3=ffn_w3,
             ffn_w2=ffn_w2, moe_router=moe_router, moe_w1=moe_w1, moe_w3=moe_w3, moe_w2=moe_w2)
    return (_trunk(x_prompt, p), _trunk(x_sample, p))
```

```python
import functools
import math

import jax
import jax.numpy as jnp
from jax import lax
from jax.experimental import pallas as pl
from jax.experimental.pallas import tpu as pltpu

F32 = jnp.float32
BF16 = jnp.bfloat16

LANES = 128
SUBLANES = 8
VMEM_LIMIT_BYTES = 56 * 1024 * 1024

D_MODEL = 1024
DEPTH = 2
BRANCH_W = 512
HG_HEADS = 4
HG_DK = 128
HG_CHUNK = 64
HG_HEADS_PER_STEP = 2
HG_LEVELS = 6
LB_EPS = 1e-20
RG_WIDTH = 512
RG_BLOCKS = 8
RG_BLOCK_W = RG_WIDTH // RG_BLOCKS
RG_CONV_W = 4
RG_C = 8.0
RG_TILE = 256
RG_HALO = 8
ATT_GROUPS = ((128, 1), (512, 4), (2048, 16))
ATT_HEADS_PER_GROUP = 4
ATT_HEADS = ATT_HEADS_PER_GROUP * len(ATT_GROUPS)
ATT_DH = 128
ATT_RADIUS = 64
ATT_QBLK = 128
ATT_UNROLL = 8
ATT_TILE = 256
FFN_DENSE = 2816
N_EXPERTS = 8
TOP_K = 2
FFN_EXPERT = 3584
NORM_EPS = 1e-6
NEG_INF = -1e30
LOG2_E = 1.4426950408889634

CB_HQ, CB_HF, CB_HB, CB_HV, CB_HG = 0, 4, 8, 12, 16
CB_RX, CB_RGATE = 20, 24
CB_AQ, CB_AK, CB_AV = 28, 40, 52
CB_GATES = 64
IN_COLS = 11264
N_CB = IN_COLS // LANES

INPROJ_TM, INPROJ_TN = 1024, 1024
MERGE_TM = 512
FFN_TM, FFN_TF = 512, 2816
MOE_ROWS, MOE_TF = 512, 1792
ROUTER_TM = 512
COMBINE_TM = 512


def _params(sem):
    return pltpu.CompilerParams(dimension_semantics=sem, vmem_limit_bytes=VMEM_LIMIT_BYTES)


def _dot(a, b):
    return jnp.dot(a, b, preferred_element_type=F32)


def _dot_nt(a, b):
    return lax.dot_general(a, b, (((1,), (1,)), ((), ())), preferred_element_type=F32)


def _sigmoid(x):
    return jax.nn.sigmoid(x)


def _lockstep(gens):
    results = [None] * len(gens)
    active = list(range(len(gens)))
    while active:
        for idx in list(active):
            try:
                next(gens[idx])
            except StopIteration as stop:
                results[idx] = stop.value
                active.remove(idx)
    return results


def _inproj_kernel(x_ref, g_ref, w_ref, o_ref, h_ref):
    @pl.when(pl.program_id(1) == 0)
    def _():
        x = x_ref[...]
        ms = jnp.mean(x * x, axis=-1, keepdims=True)
        h_ref[...] = (x * lax.rsqrt(ms + NORM_EPS) * g_ref[...]).astype(BF16)

    acc = _dot(h_ref[...], w_ref[...])
    for c in range(INPROJ_TN // LANES):
        o_ref[c] = acc[:, c * LANES:(c + 1) * LANES].astype(BF16)


def _inproj(x, g, w):
    T = x.shape[0]
    return pl.pallas_call(
        _inproj_kernel,
        grid=(T // INPROJ_TM, IN_COLS // INPROJ_TN),
        in_specs=[
            pl.BlockSpec((INPROJ_TM, D_MODEL), lambda i, j: (i, 0)),
            pl.BlockSpec((1, D_MODEL), lambda i, j: (0, 0)),
            pl.BlockSpec((D_MODEL, INPROJ_TN), lambda i, j: (0, j)),
        ],
        out_specs=pl.BlockSpec((INPROJ_TN // LANES, INPROJ_TM, LANES), lambda i, j: (j, i, 0)),
        out_shape=jax.ShapeDtypeStruct((N_CB, T, LANES), BF16),
        scratch_shapes=[pltpu.VMEM((INPROJ_TM, D_MODEL), BF16)],
        compiler_params=_params(("parallel", "arbitrary")),
        name="inproj",
    )(x, g, w)


def _hgrn_chain(q, vb, z, c0, c1, oml, code, sign_ref, states, slot, reverse):
    C = HG_CHUNK
    nv = C // SUBLANES
    e = jnp.exp(-jnp.abs(z))
    log_sig = jnp.minimum(z, 0.0) - jnp.log(1.0 + e)
    t = c1 + log_sig
    log_f = jnp.maximum(c0, t) + jnp.log(1.0 + jnp.exp(-jnp.abs(c0 - t)))
    k = oml * (jnp.where(z >= 0.0, e, 1.0) / (1.0 + e))
    sc_diag = _dot_nt(q.astype(BF16), k.astype(BF16))
    vt = vb.astype(F32).T.astype(BF16)
    yield
    x3 = log_f.reshape(nv, SUBLANES, LANES)
    sub3 = lax.broadcasted_iota(jnp.int32, (nv, SUBLANES, LANES), 1)
    d = 1
    while d < SUBLANES:
        if reverse:
            x3 = x3 + jnp.where(sub3 < SUBLANES - d, pltpu.roll(x3, SUBLANES - d, 1), 0.0)
        else:
            x3 = x3 + jnp.where(sub3 >= d, pltpu.roll(x3, d, 1), 0.0)
        d *= 2
    edge = 0 if reverse else SUBLANES - 1
    tot = jnp.broadcast_to(x3[:, edge:edge + 1, :], (nv, SUBLANES, LANES))
    groups = [None] * nv
    run = None
    for g in (range(nv - 1, -1, -1) if reverse else range(nv)):
        groups[g] = x3[g] if run is None else x3[g] + run
        run = tot[g] if run is None else run + tot[g]
    a3 = jnp.stack(groups) * LOG2_E
    a = a3.reshape(C, LANES)
    sub = lax.broadcasted_iota(jnp.int32, (C, LANES), 0) & (SUBLANES - 1)
    yield
    q_in = (q * jnp.exp2(a)).astype(BF16)
    a_last = a[0:1, :] if reverse else a[C - 1:C, :]
    k_end = (k * jnp.exp2(a_last - a)).astype(BF16)
    st = states[slot]
    o_inter = _dot_nt(q_in, st.astype(BF16))
    states[slot] = st * jnp.exp2(a_last) + _dot(vt, k_end)
    yield

    def pick(r):
        return jnp.broadcast_to(a3[:, r:r + 1, :], (nv, SUBLANES, LANES)).reshape(C, LANES)

    level_scores = []
    for b in range(HG_LEVELS):
        h = 1 << b
        m = h if reverse else h - 1
        if h == 1:
            if reverse:
                ref = jnp.where((sub & 1) == 0, pltpu.roll(a, C - 1, 0), a)
            else:
                ref = jnp.where((sub & 1) == 1, pltpu.roll(a, 1, 0), a)
        elif 2 * h < SUBLANES:
            ref = jnp.where(sub < 2 * h, pick(m), pick(m + 2 * h))
        elif 2 * h == SUBLANES:
            ref = pick(m)
        else:
            pieces = [jnp.broadcast_to(a[blk * 2 * h + m:blk * 2 * h + m + 1, :], (2 * h, LANES))
                      for blk in range(C // (2 * h))]
            ref = jnp.concatenate(pieces, axis=0) if len(pieces) > 1 else pieces[0]
        eb = jnp.exp2(((ref - a) if reverse else (a - ref)) * sign_ref[b])
        level_scores.append(_dot_nt((q * eb).astype(BF16), (k * eb).astype(BF16)))
        yield
    scores = jnp.where(code == HG_LEVELS, sc_diag, 0.0)
    for b in range(HG_LEVELS):
        scores = jnp.where(code == b, level_scores[b], scores)
    o = o_inter + _dot(scores.astype(BF16), vb)
    yield
    return o


def _hgrn_kernel(*refs, S):
    C = HG_CHUNK
    n = S // C
    nh = HG_HEADS_PER_STEP
    head_refs = [refs[5 * hh:5 * hh + 5] for hh in range(nh)]
    par_ref, o_ref, acc_ref, code_ref, sign_ref = refs[5 * nh:]

    ti = lax.broadcasted_iota(jnp.int32, (C, C), 0)
    si = lax.broadcasted_iota(jnp.int32, (C, C), 1)
    x = ti ^ si
    lvl = jnp.zeros((C, C), jnp.int32)
    for b in range(1, HG_LEVELS):
        lvl = lvl + jnp.where(x >= (1 << b), 1, 0)
    diag = jnp.where(ti == si, HG_LEVELS, -1)
    code_ref[0] = jnp.where(ti > si, lvl, diag)
    code_ref[1] = jnp.where(ti < si, lvl, diag)
    row = lax.broadcasted_iota(jnp.int32, (C, LANES), 0)
    for b in range(HG_LEVELS):
        sign_ref[b] = jnp.where((row & (1 << b)) != 0, 1.0, -1.0)

    def chain(ci, hh, d, states):
        q_ref, zf_ref, zb_ref, v_ref, _ = head_refs[hh]
        z_ref = zb_ref if d else zf_ref
        par = par_ref[hh]
        r0 = pl.multiple_of(ci * C, C)
        return _hgrn_chain(q_ref[pl.ds(r0, C), :].astype(F32), v_ref[pl.ds(r0, C), :],
                           z_ref[pl.ds(r0, C), :].astype(F32), par[3 * d:3 * d + 1], par[3 * d + 1:3 * d + 2],
                           par[3 * d + 2:3 * d + 3], code_ref[d], sign_ref, states, 2 * hh + d, d == 1)

    def make_body(accumulate):
        def body(i, carry):
            states = list(carry)
            jobs = []
            for hh in range(nh):
                jobs += [(2 * i, hh, 0), (2 * i + 1, hh, 0), (n - 1 - 2 * i, hh, 1), (n - 2 - 2 * i, hh, 1)]
            outs = _lockstep([chain(ci, hh, d, states) for ci, hh, d in jobs])
            for (ci, hh, _), o in zip(jobs, outs):
                r0 = pl.multiple_of(ci * C, C)
                if accumulate:
                    acc_ref[hh, pl.ds(r0, C), :] += o
                else:
                    acc_ref[hh, pl.ds(r0, C), :] = o
            return tuple(states)
        return body

    zero = jnp.zeros((LANES, HG_DK), F32)
    carry = lax.fori_loop(0, n // 4, make_body(False), (zero,) * (2 * nh))
    lax.fori_loop(n // 4, n // 2, make_body(True), carry)

    ft = 256

    def fin(i, _):
        r0 = pl.multiple_of(i * ft, ft)
        for hh in range(nh):
            o = acc_ref[hh, pl.ds(r0, ft), :]
            y = o * lax.rsqrt(jnp.mean(o * o, axis=-1, keepdims=True) + NORM_EPS) * par_ref[hh][6:7]
            g = head_refs[hh][4][pl.ds(r0, ft), :].astype(F32)
            o_ref[hh, pl.ds(r0, ft), :] = (y * (g * _sigmoid(g))).astype(BF16)
        return 0

    lax.fori_loop(0, S // ft, fin, 0)


def _hgrn(proj, par, B, S):
    nh = HG_HEADS_PER_STEP

    def spec(cb0, hh):
        return pl.BlockSpec((None, S, LANES), lambda b, h: (cb0 + nh * h + hh, b, 0))

    in_specs = []
    for hh in range(nh):
        in_specs += [spec(CB_HQ, hh), spec(CB_HF, hh), spec(CB_HB, hh), spec(CB_HV, hh), spec(CB_HG, hh)]
    in_specs.append(pl.BlockSpec((nh, SUBLANES, LANES), lambda b, h: (h, 0, 0)))
    return pl.pallas_call(
        functools.partial(_hgrn_kernel, S=S),
        grid=(B, HG_HEADS // nh),
        in_specs=in_specs,
        out_specs=pl.BlockSpec((nh, S, LANES), lambda b, h: (h, b, 0)),
        out_shape=jax.ShapeDtypeStruct((HG_HEADS, B * S, LANES), BF16),
        scratch_shapes=[pltpu.VMEM((nh, S, LANES), F32), pltpu.VMEM((2, HG_CHUNK, HG_CHUNK), jnp.int32),
                        pltpu.VMEM((HG_LEVELS, HG_CHUNK, LANES), F32)],
        compiler_params=_params(("parallel", "parallel")),
        name="hgrn2",
    )(*([proj] * (5 * nh)), par)


def _linear_scan_tile(a, u, carry, reverse):
    n = a.shape[0]
    nv = n // SUBLANES
    a3 = a.reshape(nv, SUBLANES, LANES)
    u3 = u.reshape(nv, SUBLANES, LANES)
    sub = lax.broadcasted_iota(jnp.int32, (nv, SUBLANES, LANES), 1)
    d = 1
    while d < SUBLANES:
        if reverse:
            keep = sub < SUBLANES - d
            shift = SUBLANES - d
        else:
            keep = sub >= d
            shift = d
        a_s = jnp.where(keep, pltpu.roll(a3, shift, 1), 1.0)
        u_s = jnp.where(keep, pltpu.roll(u3, shift, 1), 0.0)
        u3 = a3 * u_s + u3
        a3 = a3 * a_s
        d *= 2
        yield
    edge = 0 if reverse else SUBLANES - 1
    a_tot = jnp.broadcast_to(a3[:, edge:edge + 1, :], a3.shape)
    u_tot = jnp.broadcast_to(u3[:, edge:edge + 1, :], u3.shape)
    groups = [None] * nv
    for g in (range(nv - 1, -1, -1) if reverse else range(nv)):
        groups[g] = u3[g] + a3[g] * carry
        carry = u_tot[g] + a_tot[g] * carry
        if g % 4 == 0:
            yield
    return jnp.stack(groups).reshape(n, LANES), carry


def _rg_kernel(x_ref, gate_ref, par_ref, w_ref, o_ref, xpad_ref, xc_ref, h_ref, *, S):
    TS = RG_TILE
    n = S // TS
    par = par_ref[...]
    conv_b = par[4:5]

    xpad_ref[pl.ds(0, RG_HALO), :] = jnp.zeros((RG_HALO, LANES), F32)
    xpad_ref[pl.ds(S + RG_HALO, RG_HALO), :] = jnp.zeros((RG_HALO, LANES), F32)

    def copy(i, _):
        r0 = pl.multiple_of(i * TS, TS)
        xpad_ref[pl.ds(r0 + RG_HALO, TS), :] = x_ref[pl.ds(r0, TS), :].astype(F32)
        return 0

    lax.fori_loop(0, n, copy, 0)

    def conv(i, _):
        r0 = pl.multiple_of(i * TS, TS)
        xc = conv_b
        for j in range(RG_CONV_W):
            off = RG_HALO - RG_CONV_W // 2 + j
            xc = xc + par[j:j + 1] * xpad_ref[pl.ds(r0 + off, TS), :]
        xc_ref[pl.ds(r0, TS), :] = xc
        return 0

    lax.fori_loop(0, n, conv, 0)

    def tile(i, carries, d, final):
        r0 = pl.multiple_of(i * TS, TS)
        xc = xc_ref[pl.ds(r0, TS), :]
        gts = _dot(xc.astype(BF16), w_ref[:, d * 2 * LANES:(d + 1) * 2 * LANES])
        yield
        r = _sigmoid(gts[:, :LANES] + par[5 + 2 * d:6 + 2 * d])
        ig = _sigmoid(gts[:, LANES:] + par[6 + 2 * d:7 + 2 * d])
        lam = par[9 + d:10 + d]
        softplus_neg_lam = jnp.maximum(-lam, 0.0) + jnp.log1p(jnp.exp(-jnp.abs(lam)))
        log_a = (-RG_C) * r * softplus_neg_lam
        a = jnp.exp(log_a)
        y = 1.0 - a * a
        u = jnp.where(y > 0.0, y * lax.rsqrt(y), 0.0) * (ig * xc)
        yield
        h, carries[d] = yield from _linear_scan_tile(a, u, carries[d], reverse=(d == 1))
        yield
        if final:
            gate = gate_ref[pl.ds(r0, TS), :].astype(F32)
            o_ref[pl.ds(r0, TS), :] = (jax.nn.gelu(gate) * (h_ref[pl.ds(r0, TS), :] + h)).astype(BF16)
        else:
            h_ref[pl.ds(r0, TS), :] = h

    def make_body(final):
        def body(i, carry):
            carries = list(carry)
            _lockstep([tile(i, carries, 0, final), tile(n - 1 - i, carries, 1, final)])
            return tuple(carries)
        return body

    zero = jnp.zeros((SUBLANES, LANES), F32)
    carry = lax.fori_loop(0, n // 2, make_body(False), (zero, zero))
    lax.fori_loop(n // 2, n, make_body(True), carry)


def _rglru(proj, par, w4, B, S):
    ng = RG_WIDTH // LANES

    def spec(cb0):
        return pl.BlockSpec((None, S, LANES), lambda b, g: (cb0 + g, b, 0))

    return pl.pallas_call(
        functools.partial(_rg_kernel, S=S),
        grid=(B, ng),
        in_specs=[spec(CB_RX), spec(CB_RGATE),
                  pl.BlockSpec((None, 2 * SUBLANES, LANES), lambda b, g: (g, 0, 0)),
                  pl.BlockSpec((None, LANES, 4 * LANES), lambda b, g: (g, 0, 0))],
        out_specs=pl.BlockSpec((None, S, LANES), lambda b, g: (g, b, 0)),
        out_shape=jax.ShapeDtypeStruct((ng, B * S, LANES), BF16),
        scratch_shapes=[pltpu.VMEM((S + 2 * RG_HALO, LANES), F32), pltpu.VMEM((S, LANES), F32),
                        pltpu.VMEM((S, LANES), F32)],
        compiler_params=_params(("parallel", "parallel")),
        name="rglru",
    )(proj, proj, par, w4)


def _attn_kernel(q0_ref, k0_ref, v0_ref, q1_ref, k1_ref, v1_ref, q2_ref, k2_ref, v2_ref, par_ref, o_ref,
                 perm_ref, qd_ref, kd_ref, vd_ref, bias_ref, og_ref, lg_ref, *, S):
    refs = ((q0_ref, k0_ref, v0_ref), (q1_ref, k1_ref, v1_ref), (q2_ref, k2_ref, v2_ref))
    ct = ATT_TILE

    for g, (_, dil) in enumerate(ATT_GROUPS):
        q_ref, k_ref, v_ref = refs[g]
        par = par_ref[g]
        qg, kg, slope = par[0:1], par[1:2], par[2:3, 0:1]
        L = S // dil
        Q = min(ATT_QBLK, L)
        KW = min(Q + 2 * ATT_RADIUS, L)
        nq = L // Q
        nq_shift = int(math.log2(nq))

        for var in range(3):
            qi = var * ATT_RADIUS + lax.broadcasted_iota(jnp.int32, (Q, KW), 0)
            rel = jnp.abs(qi - lax.broadcasted_iota(jnp.int32, (Q, KW), 1))
            bias_ref[var, :Q, :KW] = jnp.where(rel <= ATT_RADIUS, -slope * rel.astype(F32), NEG_INF)

        per = ct // dil
        if dil > 1:
            pi = lax.broadcasted_iota(jnp.int32, (ct, ct), 0)
            pj = lax.broadcasted_iota(jnp.int32, (ct, ct), 1)
            src_tok = (pi & (per - 1)) * dil + lax.shift_right_logical(pi, int(math.log2(per)))
            perm_ref[...] = jnp.where(pj == src_tok, 1.0, 0.0).astype(BF16)

        def prep_tile(i, q_ref=q_ref, k_ref=k_ref, v_ref=v_ref, qg=qg, kg=kg, dil=dil, per=per, L=L):
            r0 = pl.multiple_of(i * ct, ct)
            xs = (q_ref[pl.ds(r0, ct), :].astype(F32), k_ref[pl.ds(r0, ct), :].astype(F32))
            sums = [jnp.sum(x * x, axis=-1, keepdims=True) for x in xs]
            yield
            tiles = [(x * lax.rsqrt(ss * (1.0 / LANES) + NORM_EPS) * gain).astype(BF16)
                     for x, ss, gain in zip(xs, sums, (qg, kg))]
            tiles.append(v_ref[pl.ds(r0, ct), :])
            if dil > 1:
                y = _dot(perm_ref[...], jnp.concatenate(tiles, axis=1))
                tiles = [y[:, c * LANES:(c + 1) * LANES] for c in range(3)]
                yield
            for dst_ref, y in zip((qd_ref, kd_ref, vd_ref), tiles):
                if dil == 1:
                    dst_ref[pl.ds(r0, ct), :] = y
                else:
                    y = y.astype(BF16)
                    for r in range(dil):
                        dst_ref[pl.ds(pl.multiple_of(r * L + i * per, 16), per), :] = y[r * per:(r + 1) * per, :]

        prep_unroll = min(4, S // ct)

        def prep(i, _, prep_tile=prep_tile, prep_unroll=prep_unroll):
            _lockstep([prep_tile(i * prep_unroll + u) for u in range(prep_unroll)])
            return 0

        lax.fori_loop(0, S // ct // prep_unroll, prep, 0)

        def rows(start, size, dil=dil):
            if dil == 1:
                return pl.ds(start, size)
            return pl.ds(start, size, stride=dil)

        unroll = min(ATT_UNROLL, dil * nq)

        def qblock(it, g=g, dil=dil, L=L, Q=Q, KW=KW, nq=nq, nq_shift=nq_shift, rows=rows):
            r = lax.shift_right_logical(it, nq_shift)
            m0 = (it & (nq - 1)) * Q
            ks = jnp.clip(m0 - ATT_RADIUS, 0, L - KW)
            var = lax.shift_right_logical(m0 - ks, int(math.log2(ATT_RADIUS)))
            base = r * L
            qn = qd_ref[pl.ds(pl.multiple_of(base + m0, 16), Q), :]
            kk = kd_ref[pl.ds(pl.multiple_of(base + ks, 16), KW), :]
            s = _dot_nt(qn, kk)
            yield
            s = s + bias_ref[var, :Q, :KW]
            m = jnp.max(s, axis=-1, keepdims=True)
            yield
            pb = jnp.exp(s - m).astype(BF16)
            vv = vd_ref[pl.ds(pl.multiple_of(base + ks, 16), KW), :]
            ol = _dot(pb, jnp.concatenate([vv, jnp.ones((KW, LANES), BF16)], axis=1))
            yield
            l = ol[:, LANES:]
            og_ref[g, rows(m0 * dil + r, Q), :] = ol[:, :LANES] / l
            lg_ref[g, rows(m0 * dil + r, Q), :] = m + jnp.log(l)

        def qblocks(i, _, unroll=unroll, qblock=qblock):
            _lockstep([qblock(i * unroll + u) for u in range(unroll)])
            return 0

        lax.fori_loop(0, (dil * nq) // unroll, qblocks, 0)

    def merge(i, _):
        r0 = pl.multiple_of(i * ct, ct)
        l0, l1, l2 = (lg_ref[g, pl.ds(r0, ct), :] for g in range(3))
        m = jnp.maximum(jnp.maximum(l0, l1), l2)
        e0, e1, e2 = jnp.exp(l0 - m), jnp.exp(l1 - m), jnp.exp(l2 - m)
        o = (e0 * og_ref[0, pl.ds(r0, ct), :] + e1 * og_ref[1, pl.ds(r0, ct), :]
             + e2 * og_ref[2, pl.ds(r0, ct), :]) / (e0 + e1 + e2)
        o_ref[pl.ds(r0, ct), :] = o.astype(BF16)
        return 0

    lax.fori_loop(0, S // ct, merge, 0)


def _attention(proj, par, B, S):
    H = ATT_HEADS_PER_GROUP
    ngroups = len(ATT_GROUPS)

    def spec(cb0, g):
        return pl.BlockSpec((None, S, LANES), lambda b, j: (cb0 + g * H + j, b, 0))

    in_specs = []
    for g in range(ngroups):
        in_specs += [spec(CB_AQ, g), spec(CB_AK, g), spec(CB_AV, g)]
    in_specs.append(pl.BlockSpec((None, ngroups, SUBLANES, LANES), lambda b, j: (j, 0, 0, 0)))
    return pl.pallas_call(
        functools.partial(_attn_kernel, S=S),
        grid=(B, H),
        in_specs=in_specs,
        out_specs=pl.BlockSpec((None, S, LANES), lambda b, j: (j, b, 0)),
        out_shape=jax.ShapeDtypeStruct((H, B * S, LANES), BF16),
        scratch_shapes=[pltpu.VMEM((ATT_TILE, ATT_TILE), BF16),
                        pltpu.VMEM((S, LANES), BF16), pltpu.VMEM((S, LANES), BF16), pltpu.VMEM((S, LANES), BF16),
                        pltpu.VMEM((3, ATT_QBLK, ATT_QBLK + 2 * ATT_RADIUS), F32),
                        pltpu.VMEM((ngroups, S, LANES), F32), pltpu.VMEM((ngroups, S, LANES), F32)],
        compiler_params=_params(("parallel", "parallel")),
        name="attention",
    )(*([proj] * (3 * ngroups)), par)


def _cat_heads(ref):
    return jnp.concatenate([ref[c] for c in range(ref.shape[0])], axis=-1)


def _merge_kernel(x_ref, hg_ref, rg_ref, att_ref, ga_ref, gb_ref, gc_ref, wb_ref, wo_ref, ng_ref,
                  x1_ref, h2_ref):
    merged = _sigmoid(_cat_heads(ga_ref).astype(F32)) * _dot(_cat_heads(hg_ref), wb_ref[0])
    merged += _sigmoid(_cat_heads(gb_ref).astype(F32)) * _dot(_cat_heads(rg_ref), wb_ref[1])
    merged += _sigmoid(_cat_heads(gc_ref).astype(F32)) * _dot(_cat_heads(att_ref), wb_ref[2])
    x1 = x_ref[...] + _dot(merged.astype(BF16), wo_ref[...])
    x1_ref[...] = x1
    ms = jnp.mean(x1 * x1, axis=-1, keepdims=True)
    h2_ref[...] = (x1 * lax.rsqrt(ms + NORM_EPS) * ng_ref[...]).astype(BF16)


def _merge(x, proj, hg, rg, att, wb, wo, ng):
    T = x.shape[0]
    tm = MERGE_TM
    nb = BRANCH_W // LANES
    ngate = D_MODEL // LANES

    branch = pl.BlockSpec((nb, tm, LANES), lambda i: (0, i, 0))

    def gate(n):
        return pl.BlockSpec((ngate, tm, LANES), lambda i: (CB_GATES // ngate + n, i, 0))

    row = pl.BlockSpec((tm, D_MODEL), lambda i: (i, 0))
    return pl.pallas_call(
        _merge_kernel,
        grid=(T // tm,),
        in_specs=[row, branch, branch, branch, gate(0), gate(1), gate(2),
                  pl.BlockSpec((3, BRANCH_W, D_MODEL), lambda i: (0, 0, 0)),
                  pl.BlockSpec((D_MODEL, D_MODEL), lambda i: (0, 0)),
                  pl.BlockSpec((1, D_MODEL), lambda i: (0, 0))],
        out_specs=[row, row],
        out_shape=[jax.ShapeDtypeStruct((T, D_MODEL), F32), jax.ShapeDtypeStruct((T, D_MODEL), BF16)],
        compiler_params=_params(("parallel",)),
        name="merge",
    )(x, hg, rg, att, proj, proj, proj, wb, wo, ng)


def _ffn_kernel(h_ref, x_ref, w1_ref, w3_ref, w2_ref, o_ref, acc_ref):
    f = pl.program_id(1)
    h = h_ref[...]
    h1 = _dot(h, w1_ref[...])
    y = _dot((h1 * _sigmoid(h1) * _dot(h, w3_ref[...])).astype(BF16), w2_ref[...])

    @pl.when(f == 0)
    def _():
        acc_ref[...] = x_ref[...] + y

    @pl.when(f > 0)
    def _():
        acc_ref[...] += y

    @pl.when(f == pl.num_programs(1) - 1)
    def _():
        o_ref[...] = acc_ref[...]


def _ffn_dense(h2, x1, w1, w3, w2):
    T = h2.shape[0]
    tm, tf = FFN_TM, FFN_TF
    return pl.pallas_call(
        _ffn_kernel,
        grid=(T // tm, FFN_DENSE // tf),
        in_specs=[pl.BlockSpec((tm, D_MODEL), lambda i, f: (i, 0)),
                  pl.BlockSpec((tm, D_MODEL), lambda i, f: (i, 0)),
                  pl.BlockSpec((D_MODEL, tf), lambda i, f: (0, f), pipeline_mode=pl.Buffered(1)),
                  pl.BlockSpec((D_MODEL, tf), lambda i, f: (0, f), pipeline_mode=pl.Buffered(1)),
                  pl.BlockSpec((tf, D_MODEL), lambda i, f: (f, 0), pipeline_mode=pl.Buffered(1))],
        out_specs=pl.BlockSpec((tm, D_MODEL), lambda i, f: (i, 0)),
        out_shape=jax.ShapeDtypeStruct((T, D_MODEL), F32),
        scratch_shapes=[pltpu.VMEM((tm, D_MODEL), F32)],
        compiler_params=_params(("parallel", "arbitrary")),
        name="ffn_dense",
    )(h2, x1, w1, w3, w2)


def _router_kernel(h_ref, wr_ref, o_ref, cnt_ref, base_ref):
    tm = ROUTER_TM

    @pl.when(pl.program_id(0) == 0)
    def _():
        base_ref[...] = jnp.zeros_like(base_ref)

    lane = lax.broadcasted_iota(jnp.int32, (tm, LANES), 1)
    logits = jnp.where(lane < N_EXPERTS, _dot(h_ref[...], wr_ref[...]), -jnp.inf)
    m1 = jnp.max(logits, axis=-1, keepdims=True)
    i1 = jnp.min(jnp.where(logits == m1, lane, LANES), axis=-1, keepdims=True)
    rest = jnp.where(lane == i1, -jnp.inf, logits)
    m2 = jnp.max(rest, axis=-1, keepdims=True)
    i2 = jnp.min(jnp.where(rest == m2, lane, LANES), axis=-1, keepdims=True)
    e21 = jnp.exp(m2 - m1)
    g1 = 1.0 / (1.0 + e21)
    g2 = e21 / (1.0 + e21)

    onehot = jnp.where(lane == i1, 1.0, jnp.where(lane == i2, 1.0, 0.0))
    ti = lax.broadcasted_iota(jnp.int32, (tm, tm), 0)
    si = lax.broadcasted_iota(jnp.int32, (tm, tm), 1)
    before = jnp.where(si < ti, 1.0, 0.0).astype(BF16)
    pos = _dot(before, onehot.astype(BF16)) + base_ref[0:1, :]
    r1 = jnp.sum(jnp.where(lane == i1, pos, 0.0), axis=-1, keepdims=True)
    r2 = jnp.sum(jnp.where(lane == i2, pos, 0.0), axis=-1, keepdims=True)
    total = base_ref[0:1, :] + jnp.sum(onehot, axis=0, keepdims=True)
    base_ref[...] = jnp.broadcast_to(total, base_ref.shape)
    cnt_ref[...] = jnp.broadcast_to(total, cnt_ref.shape)

    out = jnp.where(lane == 0, i1.astype(F32), 0.0)
    out = jnp.where(lane == 1, i2.astype(F32), out)
    out = jnp.where(lane == 2, g1, out)
    out = jnp.where(lane == 3, g2, out)
    out = jnp.where(lane == 4, r1, out)
    out = jnp.where(lane == 5, r2, out)
    o_ref[...] = out


def _router(h2, wr):
    T = h2.shape[0]
    tm = ROUTER_TM
    return pl.pallas_call(
        _router_kernel,
        grid=(T // tm,),
        in_specs=[pl.BlockSpec((tm, D_MODEL), lambda i: (i, 0)),
                  pl.BlockSpec((D_MODEL, LANES), lambda i: (0, 0))],
        out_specs=[pl.BlockSpec((tm, LANES), lambda i: (i, 0)),
                   pl.BlockSpec((SUBLANES, LANES), lambda i: (0, 0))],
        out_shape=[jax.ShapeDtypeStruct((T, LANES), F32), jax.ShapeDtypeStruct((SUBLANES, LANES), F32)],
        scratch_shapes=[pltpu.VMEM((SUBLANES, LANES), F32)],
        compiler_params=_params(("arbitrary",)),
        name="moe_router",
    )(h2, wr)


def _expert_kernel(be_ref, nu_ref, x_ref, w1_ref, w3_ref, w2_ref, o_ref, acc_ref):
    i = pl.program_id(0)
    f = pl.program_id(1)

    @pl.when(f == 0)
    def _():
        acc_ref[...] = jnp.zeros_like(acc_ref)

    @pl.when(i < nu_ref[0])
    def _():
        x = x_ref[...]
        h1 = _dot(x, w1_ref[...])
        acc_ref[...] += _dot((h1 * _sigmoid(h1) * _dot(x, w3_ref[...])).astype(BF16), w2_ref[...])

    @pl.when(f == pl.num_programs(1) - 1)
    def _():
        o_ref[...] = acc_ref[...].astype(BF16)


def _experts(rows, block_expert, n_used, w1, w3, w2):
    n_rows = rows.shape[0]
    R, tf = MOE_ROWS, MOE_TF
    grid_spec = pltpu.PrefetchScalarGridSpec(
        num_scalar_prefetch=2,
        grid=(n_rows // R, FFN_EXPERT // tf),
        in_specs=[pl.BlockSpec((R, D_MODEL), lambda i, f, be, nu: (i, 0)),
                  pl.BlockSpec((None, D_MODEL, tf), lambda i, f, be, nu: (be[i], 0, f)),
                  pl.BlockSpec((None, D_MODEL, tf), lambda i, f, be, nu: (be[i], 0, f)),
                  pl.BlockSpec((None, tf, D_MODEL), lambda i, f, be, nu: (be[i], f, 0))],
        out_specs=pl.BlockSpec((R, D_MODEL), lambda i, f, be, nu: (i, 0)),
        scratch_shapes=[pltpu.VMEM((R, D_MODEL), F32)],
    )
    return pl.pallas_call(
        _expert_kernel,
        grid_spec=grid_spec,
        out_shape=jax.ShapeDtypeStruct((n_rows, D_MODEL), BF16),
        compiler_params=_params(("parallel", "arbitrary")),
        name="moe_experts",
    )(block_expert, n_used, rows, w1, w3, w2)


def _combine_kernel(x_ref, y1_ref, y2_ref, info_ref, o_ref):
    info = info_ref[...]
    o_ref[...] = (x_ref[...] + info[:, 2:3] * y1_ref[...].astype(F32)
                  + info[:, 3:4] * y2_ref[...].astype(F32))


def _combine(x1, y1, y2, info):
    T = x1.shape[0]
    tm = COMBINE_TM
    row = pl.BlockSpec((tm, D_MODEL), lambda i: (i, 0))
    return pl.pallas_call(
        _combine_kernel,
        grid=(T // tm,),
        in_specs=[row, row, row, pl.BlockSpec((tm, LANES), lambda i: (i, 0))],
        out_specs=row,
        out_shape=jax.ShapeDtypeStruct((T, D_MODEL), F32),
        compiler_params=_params(("parallel",)),
        name="moe_combine",
    )(x1, y1, y2, info)


def _moe(h2, x1, wr, w1, w3, w2):
    T = h2.shape[0]
    R = MOE_ROWS
    n_blocks = -(-(T * TOP_K) // R) + N_EXPERTS
    info, cnt = _router(h2, wr)
    expert = info[:, 0:2].astype(jnp.int32)
    rank = info[:, 4:6].astype(jnp.int32)
    counts = cnt[0, :N_EXPERTS].astype(jnp.int32)
    blocks_per = (counts + R - 1) // R
    bend = jnp.cumsum(blocks_per)
    pstart = (bend - blocks_per) * R
    lanes = jnp.arange(N_EXPERTS, dtype=jnp.int32)
    dest = jnp.sum(jnp.where(expert[..., None] == lanes, pstart, 0), axis=-1) + rank
    n_used = bend[-1:]
    block_expert = jnp.minimum(
        jnp.searchsorted(bend, jnp.arange(n_blocks, dtype=jnp.int32), side='right'),
        N_EXPERTS - 1).astype(jnp.int32)
    n_pad = n_blocks * R - T * TOP_K
    pad_end = jnp.cumsum(blocks_per * R - counts)
    pad_expert = jnp.sum(jnp.arange(n_pad, dtype=jnp.int32)[:, None] >= pad_end[None, :], axis=1).astype(jnp.int32)
    keys = jnp.concatenate([expert.reshape(-1) * 2, pad_expert * 2 + 1])
    token = jnp.broadcast_to(jnp.arange(T, dtype=jnp.int32)[:, None], (T, TOP_K)).reshape(-1)
    vals = jnp.concatenate([token, jnp.zeros((n_pad,), jnp.int32)])
    _, tok_of_row = lax.sort((keys, vals), num_keys=1, is_stable=True)
    rows = h2.at[tok_of_row].get(mode='promise_in_bounds')
    y = _experts(rows, block_expert, n_used, w1, w3, w2)
    y1 = y.at[dest[:, 0]].get(mode='promise_in_bounds')
    y2 = y.at[dest[:, 1]].get(mode='promise_in_bounds')
    return _combine(x1, y1, y2, info)


def _pad_rows(rows, n):
    rows = jnp.stack(rows, axis=-2)
    pad = [(0, 0)] * (rows.ndim - 2) + [(0, n - rows.shape[-2]), (0, 0)]
    return jnp.pad(rows, pad)


def _hgrn_params(lb, norm_g):
    lb = lb.reshape(2, HG_HEADS, HG_DK)
    rows = []
    for d in range(2):
        rows += [jnp.log(jnp.maximum(lb[d], LB_EPS)), jnp.log1p(-lb[d]), 1.0 - lb[d]]
    rows.append(norm_g.reshape(HG_HEADS, HG_DK))
    return _pad_rows(rows, SUBLANES)


def _rg_params(conv_w, conv_b, b_a, b_x, lam):
    ng = RG_WIDTH // LANES
    g = lambda t: t.reshape(ng, LANES)
    rows = [g(conv_w[j]) for j in range(RG_CONV_W)] + [g(conv_b)]
    rows += [g(b_a[0]), g(b_x[0]), g(b_a[1]), g(b_x[1]), g(lam[0]), g(lam[1])]
    return _pad_rows(rows, 2 * SUBLANES)


def _rg_gate_weights(w_a, w_x):
    ng = RG_WIDTH // LANES
    per = RG_BLOCKS // ng

    def dense(w):
        w = w.reshape(ng, per, RG_BLOCK_W, RG_BLOCK_W)
        eye = jnp.eye(per, dtype=w.dtype)
        return jnp.einsum('gpcd,pq->gpcqd', w, eye).reshape(ng, LANES, LANES)

    return jnp.concatenate([dense(w_a[0]), dense(w_x[0]), dense(w_a[1]), dense(w_x[1])], axis=-1).astype(BF16)


def _attn_params(q_g, k_g):
    H = ATT_HEADS_PER_GROUP
    groups = []
    for gi, (_, dil) in enumerate(ATT_GROUPS):
        heads = jnp.arange(gi * H + 1, (gi + 1) * H + 1, dtype=F32)
        slopes = 2.0 ** (-8.0 * heads / ATT_HEADS) * dil
        rows = [jnp.broadcast_to(q_g * (ATT_DH ** -0.5), (H, LANES)), jnp.broadcast_to(k_g, (H, LANES)),
                jnp.broadcast_to(slopes[:, None], (H, LANES))]
        groups.append(_pad_rows(rows, SUBLANES))
    return jnp.stack(groups, axis=1)


def _trunk(x3, p):
    B, S, _ = x3.shape
    x = x3.reshape(B * S, D_MODEL)
    sm = jax.nn.softmax(p['hg_lb_logits'].astype(F32), axis=1)
    lower_bounds = jnp.cumsum(sm, axis=1) - sm[:, :1]
    for l in range(DEPTH):
        proj = _inproj(x, p['norm_mix_g'][l][None], p['w_in'][l].astype(BF16))
        hg = _hgrn(proj, _hgrn_params(lower_bounds[:, l], p['hg_norm_g'][l]), B, S)
        rg = _rglru(proj,
                    _rg_params(p['rg_conv_w'][l], p['rg_conv_b'][l], p['rg_b_a'][l], p['rg_b_x'][l],
                               p['rg_lambda'][l]),
                    _rg_gate_weights(p['rg_w_a'][l], p['rg_w_x'][l]), B, S)
        att = _attention(proj, _attn_params(p['attn_q_g'][l], p['attn_k_g'][l]), B, S)
        x1, h2 = _merge(x, proj, hg, rg, att, p['w_branch'][l].astype(BF16), p['w_out'][l].astype(BF16),
                        p['norm_ffn_g'][l][None])
        j = l // 2
        if l % 2 == 0:
            x = _ffn_dense(h2, x1, p['ffn_w1'][j].astype(BF16), p['ffn_w3'][j].astype(BF16),
                           p['ffn_w2'][j].astype(BF16))
        else:
            wr = jnp.pad(p['moe_router'][j], ((0, 0), (0, LANES - N_EXPERTS))).astype(BF16)
            x = _moe(h2, x1, wr, p['moe_w1'][j].astype(BF16), p['moe_w3'][j].astype(BF16),
                     p['moe_w2'][j].astype(BF16))
    return x.reshape(B, S, D_MODEL)


def kernel(x_prompt, x_sample, norm_mix_g, w_in, hg_lb_logits, hg_norm_g, rg_conv_w, rg_conv_b, rg_w_a, rg_b_a,
           rg_w_x, rg_b_x, rg_lambda, attn_q_g, attn_k_g, w_branch, w_out, norm_ffn_g, ffn_w1, ffn_w3, ffn_w2,
           moe_router, moe_w1, moe_w3, moe_w2):
    p = dict(norm_mix_g=norm_mix_g, w_in=w_in, hg_lb_logits=hg_lb_logits, hg_norm_g=hg_norm_g,
             rg_conv_w=rg_conv_w, rg_conv_b=rg_conv_b, rg_w_a=rg_w_a, rg_b_a=rg_b_a, rg_w_x=rg_w_x,
             rg_b_x=rg_b_x, rg_lambda=rg_lambda, attn_q_g=attn_q_g, attn_k_g=attn_k_g,
             w_branch=w_branch, w_out=w_out, norm_ffn_g=norm_ffn_g, ffn_w1=ffn_w1, ffn_w3=ffn_w3,
             ffn_w2=ffn_w2, moe_router=moe_router, moe_w1=moe_w1, moe_w3=moe_w3, moe_w2=moe_w2)
    return (_trunk(x_prompt, p), _trunk(x_sample, p))
```

```python
import functools
import math

import jax
import jax.numpy as jnp
from jax import lax
from jax.experimental import pallas as pl
from jax.experimental.pallas import tpu as pltpu

F32 = jnp.float32
BF16 = jnp.bfloat16

LANES = 128
SUBLANES = 8
VMEM_LIMIT_BYTES = 56 * 1024 * 1024

D_MODEL = 1024
DEPTH = 2
BRANCH_W = 512
HG_HEADS = 4
HG_DK = 128
HG_CHUNK = 64
HG_HEADS_PER_STEP = 2
HG_LEVELS = 6
LB_EPS = 1e-20
RG_WIDTH = 512
RG_BLOCKS = 8
RG_BLOCK_W = RG_WIDTH // RG_BLOCKS
RG_CONV_W = 4
RG_C = 8.0
RG_TILE = 256
RG_HALO = 8
ATT_GROUPS = ((128, 1), (512, 4), (2048, 16))
ATT_HEADS_PER_GROUP = 4
ATT_HEADS = ATT_HEADS_PER_GROUP * len(ATT_GROUPS)
ATT_DH = 128
ATT_RADIUS = 64
ATT_QBLK = 128
ATT_UNROLL = 8
ATT_TILE = 256
FFN_DENSE = 2816
N_EXPERTS = 8
TOP_K = 2
FFN_EXPERT = 3584
NORM_EPS = 1e-6
NEG_INF = -1e30
LOG2_E = 1.4426950408889634

CB_HQ, CB_HF, CB_HB, CB_HV, CB_HG = 0, 4, 8, 12, 16
CB_RX, CB_RGATE = 20, 24
CB_AQ, CB_AK, CB_AV = 28, 40, 52
CB_GATES = 64
IN_COLS = 11264
N_CB = IN_COLS // LANES

INPROJ_TM, INPROJ_TN = 1024, 1024
MERGE_TM = 512
FFN_TM, FFN_TF = 512, 2816
MOE_ROWS, MOE_TF = 512, 3584
MOE_GATHER_PIECES = 3
ROUTER_TM = 512
COMBINE_TM = 512


def _params(sem):
    return pltpu.CompilerParams(dimension_semantics=sem, vmem_limit_bytes=VMEM_LIMIT_BYTES)


def _dot(a, b):
    return jnp.dot(a, b, preferred_element_type=F32)


def _dot_nt(a, b):
    return lax.dot_general(a, b, (((1,), (1,)), ((), ())), preferred_element_type=F32)


def _sigmoid(x):
    return jax.nn.sigmoid(x)


def _lockstep(gens):
    results = [None] * len(gens)
    active = list(range(len(gens)))
    while active:
        for idx in list(active):
            try:
                next(gens[idx])
            except StopIteration as stop:
                results[idx] = stop.value
                active.remove(idx)
    return results


def _inproj_kernel(x_ref, g_ref, w_ref, o_ref, h_ref):
    @pl.when(pl.program_id(1) == 0)
    def _():
        x = x_ref[...]
        ms = jnp.mean(x * x, axis=-1, keepdims=True)
        h_ref[...] = (x * lax.rsqrt(ms + NORM_EPS) * g_ref[...]).astype(BF16)

    acc = _dot(h_ref[...], w_ref[...])
    for c in range(INPROJ_TN // LANES):
        o_ref[c] = acc[:, c * LANES:(c + 1) * LANES].astype(BF16)


def _inproj(x, g, w):
    T = x.shape[0]
    return pl.pallas_call(
        _inproj_kernel,
        grid=(T // INPROJ_TM, IN_COLS // INPROJ_TN),
        in_specs=[
            pl.BlockSpec((INPROJ_TM, D_MODEL), lambda i, j: (i, 0)),
            pl.BlockSpec((1, D_MODEL), lambda i, j: (0, 0)),
            pl.BlockSpec((D_MODEL, INPROJ_TN), lambda i, j: (0, j)),
        ],
        out_specs=pl.BlockSpec((INPROJ_TN // LANES, INPROJ_TM, LANES), lambda i, j: (j, i, 0)),
        out_shape=jax.ShapeDtypeStruct((N_CB, T, LANES), BF16),
        scratch_shapes=[pltpu.VMEM((INPROJ_TM, D_MODEL), BF16)],
        compiler_params=_params(("parallel", "arbitrary")),
        name="inproj",
    )(x, g, w)


def _hgrn_chain(q, vb, z, c0, c1, oml, code, sign_ref, states, slot, reverse):
    C = HG_CHUNK
    nv = C // SUBLANES
    e = jnp.exp(-jnp.abs(z))
    log_sig = jnp.minimum(z, 0.0) - jnp.log(1.0 + e)
    t = c1 + log_sig
    log_f = jnp.maximum(c0, t) + jnp.log(1.0 + jnp.exp(-jnp.abs(c0 - t)))
    k = oml * (jnp.where(z >= 0.0, e, 1.0) / (1.0 + e))
    sc_diag = _dot_nt(q.astype(BF16), k.astype(BF16))
    vt = vb.astype(F32).T.astype(BF16)
    yield
    x3 = log_f.reshape(nv, SUBLANES, LANES)
    sub3 = lax.broadcasted_iota(jnp.int32, (nv, SUBLANES, LANES), 1)
    d = 1
    while d < SUBLANES:
        if reverse:
            x3 = x3 + jnp.where(sub3 < SUBLANES - d, pltpu.roll(x3, SUBLANES - d, 1), 0.0)
        else:
            x3 = x3 + jnp.where(sub3 >= d, pltpu.roll(x3, d, 1), 0.0)
        d *= 2
    edge = 0 if reverse else SUBLANES - 1
    tot = jnp.broadcast_to(x3[:, edge:edge + 1, :], (nv, SUBLANES, LANES))
    groups = [None] * nv
    run = None
    for g in (range(nv - 1, -1, -1) if reverse else range(nv)):
        groups[g] = x3[g] if run is None else x3[g] + run
        run = tot[g] if run is None else run + tot[g]
    a3 = jnp.stack(groups) * LOG2_E
    a = a3.reshape(C, LANES)
    sub = lax.broadcasted_iota(jnp.int32, (C, LANES), 0) & (SUBLANES - 1)
    yield
    q_in = (q * jnp.exp2(a)).astype(BF16)
    a_last = a[0:1, :] if reverse else a[C - 1:C, :]
    k_end = (k * jnp.exp2(a_last - a)).astype(BF16)
    st = states[slot]
    o_inter = _dot_nt(q_in, st.astype(BF16))
    states[slot] = st * jnp.exp2(a_last) + _dot(vt, k_end)
    yield

    def pick(r):
        return jnp.broadcast_to(a3[:, r:r + 1, :], (nv, SUBLANES, LANES)).reshape(C, LANES)

    level_scores = []
    for b in range(HG_LEVELS):
        h = 1 << b
        m = h if reverse else h - 1
        if h == 1:
            if reverse:
                ref = jnp.where((sub & 1) == 0, pltpu.roll(a, C - 1, 0), a)
            else:
                ref = jnp.where((sub & 1) == 1, pltpu.roll(a, 1, 0), a)
        elif 2 * h < SUBLANES:
            ref = jnp.where(sub < 2 * h, pick(m), pick(m + 2 * h))
        elif 2 * h == SUBLANES:
            ref = pick(m)
        else:
            pieces = [jnp.broadcast_to(a[blk * 2 * h + m:blk * 2 * h + m + 1, :], (2 * h, LANES))
                      for blk in range(C // (2 * h))]
            ref = jnp.concatenate(pieces, axis=0) if len(pieces) > 1 else pieces[0]
        eb = jnp.exp2(((ref - a) if reverse else (a - ref)) * sign_ref[b])
        level_scores.append(_dot_nt((q * eb).astype(BF16), (k * eb).astype(BF16)))
        yield
    scores = jnp.where(code == HG_LEVELS, sc_diag, 0.0)
    for b in range(HG_LEVELS):
        scores = jnp.where(code == b, level_scores[b], scores)
    o = o_inter + _dot(scores.astype(BF16), vb)
    yield
    return o


def _hgrn_kernel(*refs, S):
    C = HG_CHUNK
    n = S // C
    nh = HG_HEADS_PER_STEP
    head_refs = [refs[5 * hh:5 * hh + 5] for hh in range(nh)]
    par_ref, o_ref, acc_ref, code_ref, sign_ref = refs[5 * nh:]

    ti = lax.broadcasted_iota(jnp.int32, (C, C), 0)
    si = lax.broadcasted_iota(jnp.int32, (C, C), 1)
    x = ti ^ si
    lvl = jnp.zeros((C, C), jnp.int32)
    for b in range(1, HG_LEVELS):
        lvl = lvl + jnp.where(x >= (1 << b), 1, 0)
    diag = jnp.where(ti == si, HG_LEVELS, -1)
    code_ref[0] = jnp.where(ti > si, lvl, diag)
    code_ref[1] = jnp.where(ti < si, lvl, diag)
    row = lax.broadcasted_iota(jnp.int32, (C, LANES), 0)
    for b in range(HG_LEVELS):
        sign_ref[b] = jnp.where((row & (1 << b)) != 0, 1.0, -1.0)

    def chain(ci, hh, d, states):
        q_ref, zf_ref, zb_ref, v_ref, _ = head_refs[hh]
        z_ref = zb_ref if d else zf_ref
        par = par_ref[hh]
        r0 = pl.multiple_of(ci * C, C)
        return _hgrn_chain(q_ref[pl.ds(r0, C), :].astype(F32), v_ref[pl.ds(r0, C), :],
                           z_ref[pl.ds(r0, C), :].astype(F32), par[3 * d:3 * d + 1], par[3 * d + 1:3 * d + 2],
                           par[3 * d + 2:3 * d + 3], code_ref[d], sign_ref, states, 2 * hh + d, d == 1)

    def make_body(accumulate):
        def body(i, carry):
            states = list(carry)
            jobs = []
            for hh in range(nh):
                jobs += [(2 * i, hh, 0), (2 * i + 1, hh, 0), (n - 1 - 2 * i, hh, 1), (n - 2 - 2 * i, hh, 1)]
            outs = _lockstep([chain(ci, hh, d, states) for ci, hh, d in jobs])
            for (ci, hh, _), o in zip(jobs, outs):
                r0 = pl.multiple_of(ci * C, C)
                if accumulate:
                    acc_ref[hh, pl.ds(r0, C), :] += o
                else:
                    acc_ref[hh, pl.ds(r0, C), :] = o
            return tuple(states)
        return body

    zero = jnp.zeros((LANES, HG_DK), F32)
    carry = lax.fori_loop(0, n // 4, make_body(False), (zero,) * (2 * nh))
    lax.fori_loop(n // 4, n // 2, make_body(True), carry)

    ft = 256

    def fin(i, _):
        r0 = pl.multiple_of(i * ft, ft)
        for hh in range(nh):
            o = acc_ref[hh, pl.ds(r0, ft), :]
            y = o * lax.rsqrt(jnp.mean(o * o, axis=-1, keepdims=True) + NORM_EPS) * par_ref[hh][6:7]
            g = head_refs[hh][4][pl.ds(r0, ft), :].astype(F32)
            o_ref[hh, pl.ds(r0, ft), :] = (y * (g * _sigmoid(g))).astype(BF16)
        return 0

    lax.fori_loop(0, S // ft, fin, 0)


def _hgrn(proj, par, B, S):
    nh = HG_HEADS_PER_STEP

    def spec(cb0, hh):
        return pl.BlockSpec((None, S, LANES), lambda b, h: (cb0 + nh * h + hh, b, 0))

    in_specs = []
    for hh in range(nh):
        in_specs += [spec(CB_HQ, hh), spec(CB_HF, hh), spec(CB_HB, hh), spec(CB_HV, hh), spec(CB_HG, hh)]
    in_specs.append(pl.BlockSpec((nh, SUBLANES, LANES), lambda b, h: (h, 0, 0)))
    return pl.pallas_call(
        functools.partial(_hgrn_kernel, S=S),
        grid=(B, HG_HEADS // nh),
        in_specs=in_specs,
        out_specs=pl.BlockSpec((nh, S, LANES), lambda b, h: (h, b, 0)),
        out_shape=jax.ShapeDtypeStruct((HG_HEADS, B * S, LANES), BF16),
        scratch_shapes=[pltpu.VMEM((nh, S, LANES), F32), pltpu.VMEM((2, HG_CHUNK, HG_CHUNK), jnp.int32),
                        pltpu.VMEM((HG_LEVELS, HG_CHUNK, LANES), F32)],
        compiler_params=_params(("parallel", "parallel")),
        name="hgrn2",
    )(*([proj] * (5 * nh)), par)


def _linear_scan_tile(a, u, carry, reverse):
    n = a.shape[0]
    nv = n // SUBLANES
    a3 = a.reshape(nv, SUBLANES, LANES)
    u3 = u.reshape(nv, SUBLANES, LANES)
    sub = lax.broadcasted_iota(jnp.int32, (nv, SUBLANES, LANES), 1)
    d = 1
    while d < SUBLANES:
        if reverse:
            keep = sub < SUBLANES - d
            shift = SUBLANES - d
        else:
            keep = sub >= d
            shift = d
        a_s = jnp.where(keep, pltpu.roll(a3, shift, 1), 1.0)
        u_s = jnp.where(keep, pltpu.roll(u3, shift, 1), 0.0)
        u3 = a3 * u_s + u3
        a3 = a3 * a_s
        d *= 2
        yield
    edge = 0 if reverse else SUBLANES - 1
    a_tot = jnp.broadcast_to(a3[:, edge:edge + 1, :], a3.shape)
    u_tot = jnp.broadcast_to(u3[:, edge:edge + 1, :], u3.shape)
    groups = [None] * nv
    for g in (range(nv - 1, -1, -1) if reverse else range(nv)):
        groups[g] = u3[g] + a3[g] * carry
        carry = u_tot[g] + a_tot[g] * carry
        if g % 4 == 0:
            yield
    return jnp.stack(groups).reshape(n, LANES), carry


def _rg_kernel(x_ref, gate_ref, par_ref, w_ref, o_ref, xpad_ref, xc_ref, h_ref, *, S):
    TS = RG_TILE
    n = S // TS
    par = par_ref[...]
    conv_b = par[4:5]

    xpad_ref[pl.ds(0, RG_HALO), :] = jnp.zeros((RG_HALO, LANES), F32)
    xpad_ref[pl.ds(S + RG_HALO, RG_HALO), :] = jnp.zeros((RG_HALO, LANES), F32)

    def copy(i, _):
        r0 = pl.multiple_of(i * TS, TS)
        xpad_ref[pl.ds(r0 + RG_HALO, TS), :] = x_ref[pl.ds(r0, TS), :].astype(F32)
        return 0

    lax.fori_loop(0, n, copy, 0)

    def conv(i, _):
        r0 = pl.multiple_of(i * TS, TS)
        xc = conv_b
        for j in range(RG_CONV_W):
            off = RG_HALO - RG_CONV_W // 2 + j
            xc = xc + par[j:j + 1] * xpad_ref[pl.ds(r0 + off, TS), :]
        xc_ref[pl.ds(r0, TS), :] = xc
        return 0

    lax.fori_loop(0, n, conv, 0)

    def tile(i, carries, d, final):
        r0 = pl.multiple_of(i * TS, TS)
        xc = xc_ref[pl.ds(r0, TS), :]
        gts = _dot(xc.astype(BF16), w_ref[:, d * 2 * LANES:(d + 1) * 2 * LANES])
        yield
        r = _sigmoid(gts[:, :LANES] + par[5 + 2 * d:6 + 2 * d])
        ig = _sigmoid(gts[:, LANES:] + par[6 + 2 * d:7 + 2 * d])
        lam = par[9 + d:10 + d]
        softplus_neg_lam = jnp.maximum(-lam, 0.0) + jnp.log1p(jnp.exp(-jnp.abs(lam)))
        log_a = (-RG_C) * r * softplus_neg_lam
        a = jnp.exp(log_a)
        y = 1.0 - a * a
        u = jnp.where(y > 0.0, y * lax.rsqrt(y), 0.0) * (ig * xc)
        yield
        h, carries[d] = yield from _linear_scan_tile(a, u, carries[d], reverse=(d == 1))
        yield
        if final:
            gate = gate_ref[pl.ds(r0, TS), :].astype(F32)
            o_ref[pl.ds(r0, TS), :] = (jax.nn.gelu(gate) * (h_ref[pl.ds(r0, TS), :] + h)).astype(BF16)
        else:
            h_ref[pl.ds(r0, TS), :] = h

    def make_body(final):
        def body(i, carry):
            carries = list(carry)
            _lockstep([tile(i, carries, 0, final), tile(n - 1 - i, carries, 1, final)])
            return tuple(carries)
        return body

    zero = jnp.zeros((SUBLANES, LANES), F32)
    carry = lax.fori_loop(0, n // 2, make_body(False), (zero, zero))
    lax.fori_loop(n // 2, n, make_body(True), carry)


def _rglru(proj, par, w4, B, S):
    ng = RG_WIDTH // LANES

    def spec(cb0):
        return pl.BlockSpec((None, S, LANES), lambda b, g: (cb0 + g, b, 0))

    return pl.pallas_call(
        functools.partial(_rg_kernel, S=S),
        grid=(B, ng),
        in_specs=[spec(CB_RX), spec(CB_RGATE),
                  pl.BlockSpec((None, 2 * SUBLANES, LANES), lambda b, g: (g, 0, 0)),
                  pl.BlockSpec((None, LANES, 4 * LANES), lambda b, g: (g, 0, 0))],
        out_specs=pl.BlockSpec((None, S, LANES), lambda b, g: (g, b, 0)),
        out_shape=jax.ShapeDtypeStruct((ng, B * S, LANES), BF16),
        scratch_shapes=[pltpu.VMEM((S + 2 * RG_HALO, LANES), F32), pltpu.VMEM((S, LANES), F32),
                        pltpu.VMEM((S, LANES), F32)],
        compiler_params=_params(("parallel", "parallel")),
        name="rglru",
    )(proj, proj, par, w4)


def _attn_kernel(q0_ref, k0_ref, v0_ref, q1_ref, k1_ref, v1_ref, q2_ref, k2_ref, v2_ref, par_ref, o_ref,
                 perm_ref, qd_ref, kd_ref, vd_ref, bias_ref, og_ref, lg_ref, *, S):
    refs = ((q0_ref, k0_ref, v0_ref), (q1_ref, k1_ref, v1_ref), (q2_ref, k2_ref, v2_ref))
    ct = ATT_TILE

    for g, (_, dil) in enumerate(ATT_GROUPS):
        q_ref, k_ref, v_ref = refs[g]
        par = par_ref[g]
        qg, kg, slope = par[0:1], par[1:2], par[2:3, 0:1]
        L = S // dil
        Q = min(ATT_QBLK, L)
        KW = min(Q + 2 * ATT_RADIUS, L)
        nq = L // Q
        nq_shift = int(math.log2(nq))

        for var in range(3):
            qi = var * ATT_RADIUS + lax.broadcasted_iota(jnp.int32, (Q, KW), 0)
            rel = jnp.abs(qi - lax.broadcasted_iota(jnp.int32, (Q, KW), 1))
            bias_ref[var, :Q, :KW] = jnp.where(rel <= ATT_RADIUS, -slope * rel.astype(F32), NEG_INF)

        per = ct // dil
        if dil > 1:
            pi = lax.broadcasted_iota(jnp.int32, (ct, ct), 0)
            pj = lax.broadcasted_iota(jnp.int32, (ct, ct), 1)
            src_tok = (pi & (per - 1)) * dil + lax.shift_right_logical(pi, int(math.log2(per)))
            perm_ref[...] = jnp.where(pj == src_tok, 1.0, 0.0).astype(BF16)

        def prep_tile(i, q_ref=q_ref, k_ref=k_ref, v_ref=v_ref, qg=qg, kg=kg, dil=dil, per=per, L=L):
            r0 = pl.multiple_of(i * ct, ct)
            xs = (q_ref[pl.ds(r0, ct), :].astype(F32), k_ref[pl.ds(r0, ct), :].astype(F32))
            sums = [jnp.sum(x * x, axis=-1, keepdims=True) for x in xs]
            yield
            tiles = [(x * lax.rsqrt(ss * (1.0 / LANES) + NORM_EPS) * gain).astype(BF16)
                     for x, ss, gain in zip(xs, sums, (qg, kg))]
            tiles.append(v_ref[pl.ds(r0, ct), :])
            if dil > 1:
                y = _dot(perm_ref[...], jnp.concatenate(tiles, axis=1))
                tiles = [y[:, c * LANES:(c + 1) * LANES] for c in range(3)]
                yield
            for dst_ref, y in zip((qd_ref, kd_ref, vd_ref), tiles):
                if dil == 1:
                    dst_ref[pl.ds(r0, ct), :] = y
                else:
                    y = y.astype(BF16)
                    for r in range(dil):
                        dst_ref[pl.ds(pl.multiple_of(r * L + i * per, 16), per), :] = y[r * per:(r + 1) * per, :]

        prep_unroll = min(4, S // ct)

        def prep(i, _, prep_tile=prep_tile, prep_unroll=prep_unroll):
            _lockstep([prep_tile(i * prep_unroll + u) for u in range(prep_unroll)])
            return 0

        lax.fori_loop(0, S // ct // prep_unroll, prep, 0)

        def rows(start, size, dil=dil):
            if dil == 1:
                return pl.ds(start, size)
            return pl.ds(start, size, stride=dil)

        unroll = min(ATT_UNROLL, dil * nq)

        def qblock(it, g=g, dil=dil, L=L, Q=Q, KW=KW, nq=nq, nq_shift=nq_shift, rows=rows):
            r = lax.shift_right_logical(it, nq_shift)
            m0 = (it & (nq - 1)) * Q
            ks = jnp.clip(m0 - ATT_RADIUS, 0, L - KW)
            var = lax.shift_right_logical(m0 - ks, int(math.log2(ATT_RADIUS)))
            base = r * L
            qn = qd_ref[pl.ds(pl.multiple_of(base + m0, 16), Q), :]
            kk = kd_ref[pl.ds(pl.multiple_of(base + ks, 16), KW), :]
            s = _dot_nt(qn, kk)
            yield
            s = s + bias_ref[var, :Q, :KW]
            m = jnp.max(s, axis=-1, keepdims=True)
            yield
            pb = jnp.exp(s - m).astype(BF16)
            vv = vd_ref[pl.ds(pl.multiple_of(base + ks, 16), KW), :]
            ol = _dot(pb, jnp.concatenate([vv, jnp.ones((KW, LANES), BF16)], axis=1))
            yield
            l = ol[:, LANES:]
            og_ref[g, rows(m0 * dil + r, Q), :] = ol[:, :LANES] / l
            lg_ref[g, rows(m0 * dil + r, Q), :] = m + jnp.log(l)

        def qblocks(i, _, unroll=unroll, qblock=qblock):
            _lockstep([qblock(i * unroll + u) for u in range(unroll)])
            return 0

        lax.fori_loop(0, (dil * nq) // unroll, qblocks, 0)

    def merge(i, _):
        r0 = pl.multiple_of(i * ct, ct)
        l0, l1, l2 = (lg_ref[g, pl.ds(r0, ct), :] for g in range(3))
        m = jnp.maximum(jnp.maximum(l0, l1), l2)
        e0, e1, e2 = jnp.exp(l0 - m), jnp.exp(l1 - m), jnp.exp(l2 - m)
        o = (e0 * og_ref[0, pl.ds(r0, ct), :] + e1 * og_ref[1, pl.ds(r0, ct), :]
             + e2 * og_ref[2, pl.ds(r0, ct), :]) / (e0 + e1 + e2)
        o_ref[pl.ds(r0, ct), :] = o.astype(BF16)
        return 0

    lax.fori_loop(0, S // ct, merge, 0)


def _attention(proj, par, B, S):
    H = ATT_HEADS_PER_GROUP
    ngroups = len(ATT_GROUPS)

    def spec(cb0, g):
        return pl.BlockSpec((None, S, LANES), lambda b, j: (cb0 + g * H + j, b, 0))

    in_specs = []
    for g in range(ngroups):
        in_specs += [spec(CB_AQ, g), spec(CB_AK, g), spec(CB_AV, g)]
    in_specs.append(pl.BlockSpec((None, ngroups, SUBLANES, LANES), lambda b, j: (j, 0, 0, 0)))
    return pl.pallas_call(
        functools.partial(_attn_kernel, S=S),
        grid=(B, H),
        in_specs=in_specs,
        out_specs=pl.BlockSpec((None, S, LANES), lambda b, j: (j, b, 0)),
        out_shape=jax.ShapeDtypeStruct((H, B * S, LANES), BF16),
        scratch_shapes=[pltpu.VMEM((ATT_TILE, ATT_TILE), BF16),
                        pltpu.VMEM((S, LANES), BF16), pltpu.VMEM((S, LANES), BF16), pltpu.VMEM((S, LANES), BF16),
                        pltpu.VMEM((3, ATT_QBLK, ATT_QBLK + 2 * ATT_RADIUS), F32),
                        pltpu.VMEM((ngroups, S, LANES), F32), pltpu.VMEM((ngroups, S, LANES), F32)],
        compiler_params=_params(("parallel", "parallel")),
        name="attention",
    )(*([proj] * (3 * ngroups)), par)


def _cat_heads(ref):
    return jnp.concatenate([ref[c] for c in range(ref.shape[0])], axis=-1)


def _merge_kernel(x_ref, hg_ref, rg_ref, att_ref, ga_ref, gb_ref, gc_ref, wb_ref, wo_ref, ng_ref,
                  x1_ref, h2_ref):
    merged = _sigmoid(_cat_heads(ga_ref).astype(F32)) * _dot(_cat_heads(hg_ref), wb_ref[0])
    merged += _sigmoid(_cat_heads(gb_ref).astype(F32)) * _dot(_cat_heads(rg_ref), wb_ref[1])
    merged += _sigmoid(_cat_heads(gc_ref).astype(F32)) * _dot(_cat_heads(att_ref), wb_ref[2])
    x1 = x_ref[...] + _dot(merged.astype(BF16), wo_ref[...])
    x1_ref[...] = x1
    ms = jnp.mean(x1 * x1, axis=-1, keepdims=True)
    h2_ref[...] = (x1 * lax.rsqrt(ms + NORM_EPS) * ng_ref[...]).astype(BF16)


def _merge(x, proj, hg, rg, att, wb, wo, ng):
    T = x.shape[0]
    tm = MERGE_TM
    nb = BRANCH_W // LANES
    ngate = D_MODEL // LANES

    branch = pl.BlockSpec((nb, tm, LANES), lambda i: (0, i, 0))

    def gate(n):
        return pl.BlockSpec((ngate, tm, LANES), lambda i: (CB_GATES // ngate + n, i, 0))

    row = pl.BlockSpec((tm, D_MODEL), lambda i: (i, 0))
    return pl.pallas_call(
        _merge_kernel,
        grid=(T // tm,),
        in_specs=[row, branch, branch, branch, gate(0), gate(1), gate(2),
                  pl.BlockSpec((3, BRANCH_W, D_MODEL), lambda i: (0, 0, 0)),
                  pl.BlockSpec((D_MODEL, D_MODEL), lambda i: (0, 0)),
                  pl.BlockSpec((1, D_MODEL), lambda i: (0, 0))],
        out_specs=[row, row],
        out_shape=[jax.ShapeDtypeStruct((T, D_MODEL), F32), jax.ShapeDtypeStruct((T, D_MODEL), BF16)],
        compiler_params=_params(("parallel",)),
        name="merge",
    )(x, hg, rg, att, proj, proj, proj, wb, wo, ng)


def _ffn_kernel(h_ref, x_ref, w1_ref, w3_ref, w2_ref, o_ref, acc_ref):
    f = pl.program_id(1)
    h = h_ref[...]
    h1 = _dot(h, w1_ref[...])
    y = _dot((h1 * _sigmoid(h1) * _dot(h, w3_ref[...])).astype(BF16), w2_ref[...])

    @pl.when(f == 0)
    def _():
        acc_ref[...] = x_ref[...] + y

    @pl.when(f > 0)
    def _():
        acc_ref[...] += y

    @pl.when(f == pl.num_programs(1) - 1)
    def _():
        o_ref[...] = acc_ref[...]


def _ffn_dense(h2, x1, w1, w3, w2):
    T = h2.shape[0]
    tm, tf = FFN_TM, FFN_TF
    return pl.pallas_call(
        _ffn_kernel,
        grid=(T // tm, FFN_DENSE // tf),
        in_specs=[pl.BlockSpec((tm, D_MODEL), lambda i, f: (i, 0)),
                  pl.BlockSpec((tm, D_MODEL), lambda i, f: (i, 0)),
                  pl.BlockSpec((D_MODEL, tf), lambda i, f: (0, f), pipeline_mode=pl.Buffered(1)),
                  pl.BlockSpec((D_MODEL, tf), lambda i, f: (0, f), pipeline_mode=pl.Buffered(1)),
                  pl.BlockSpec((tf, D_MODEL), lambda i, f: (f, 0), pipeline_mode=pl.Buffered(1))],
        out_specs=pl.BlockSpec((tm, D_MODEL), lambda i, f: (i, 0)),
        out_shape=jax.ShapeDtypeStruct((T, D_MODEL), F32),
        scratch_shapes=[pltpu.VMEM((tm, D_MODEL), F32)],
        compiler_params=_params(("parallel", "arbitrary")),
        name="ffn_dense",
    )(h2, x1, w1, w3, w2)


def _router_kernel(h_ref, wr_ref, o_ref, cnt_ref, base_ref):
    tm = ROUTER_TM

    @pl.when(pl.program_id(0) == 0)
    def _():
        base_ref[...] = jnp.zeros_like(base_ref)

    lane = lax.broadcasted_iota(jnp.int32, (tm, LANES), 1)
    logits = jnp.where(lane < N_EXPERTS, _dot(h_ref[...], wr_ref[...]), -jnp.inf)
    m1 = jnp.max(logits, axis=-1, keepdims=True)
    i1 = jnp.min(jnp.where(logits == m1, lane, LANES), axis=-1, keepdims=True)
    rest = jnp.where(lane == i1, -jnp.inf, logits)
    m2 = jnp.max(rest, axis=-1, keepdims=True)
    i2 = jnp.min(jnp.where(rest == m2, lane, LANES), axis=-1, keepdims=True)
    e21 = jnp.exp(m2 - m1)
    g1 = 1.0 / (1.0 + e21)
    g2 = e21 / (1.0 + e21)

    onehot = jnp.where(lane == i1, 1.0, jnp.where(lane == i2, 1.0, 0.0))
    ti = lax.broadcasted_iota(jnp.int32, (tm, tm), 0)
    si = lax.broadcasted_iota(jnp.int32, (tm, tm), 1)
    before = jnp.where(si < ti, 1.0, 0.0).astype(BF16)
    pos = _dot(before, onehot.astype(BF16)) + base_ref[0:1, :]
    r1 = jnp.sum(jnp.where(lane == i1, pos, 0.0), axis=-1, keepdims=True)
    r2 = jnp.sum(jnp.where(lane == i2, pos, 0.0), axis=-1, keepdims=True)
    total = base_ref[0:1, :] + jnp.sum(onehot, axis=0, keepdims=True)
    base_ref[...] = jnp.broadcast_to(total, base_ref.shape)
    cnt_ref[...] = jnp.broadcast_to(total, cnt_ref.shape)

    out = jnp.where(lane == 0, i1.astype(F32), 0.0)
    out = jnp.where(lane == 1, i2.astype(F32), out)
    out = jnp.where(lane == 2, g1, out)
    out = jnp.where(lane == 3, g2, out)
    out = jnp.where(lane == 4, r1, out)
    out = jnp.where(lane == 5, r2, out)
    o_ref[...] = out


def _router(h2, wr):
    T = h2.shape[0]
    tm = ROUTER_TM
    return pl.pallas_call(
        _router_kernel,
        grid=(T // tm,),
        in_specs=[pl.BlockSpec((tm, D_MODEL), lambda i: (i, 0)),
                  pl.BlockSpec((D_MODEL, LANES), lambda i: (0, 0))],
        out_specs=[pl.BlockSpec((tm, LANES), lambda i: (i, 0)),
                   pl.BlockSpec((SUBLANES, LANES), lambda i: (0, 0))],
        out_shape=[jax.ShapeDtypeStruct((T, LANES), F32), jax.ShapeDtypeStruct((SUBLANES, LANES), F32)],
        scratch_shapes=[pltpu.VMEM((SUBLANES, LANES), F32)],
        compiler_params=_params(("arbitrary",)),
        name="moe_router",
    )(h2, wr)


def _expert_kernel(be_ref, nu_ref, x_ref, w1_ref, w3_ref, w2_ref, o_ref, acc_ref):
    i = pl.program_id(0)
    f = pl.program_id(1)

    @pl.when(f == 0)
    def _():
        acc_ref[...] = jnp.zeros_like(acc_ref)

    @pl.when(i < nu_ref[0])
    def _():
        x = x_ref[...]
        h1 = _dot(x, w1_ref[...])
        acc_ref[...] += _dot((h1 * _sigmoid(h1) * _dot(x, w3_ref[...])).astype(BF16), w2_ref[...])

    @pl.when(f == pl.num_programs(1) - 1)
    def _():
        o_ref[...] = acc_ref[...].astype(BF16)


def _experts(rows, block_expert, n_used, w1, w3, w2):
    n_rows = rows.shape[0]
    R, tf = MOE_ROWS, MOE_TF
    grid_spec = pltpu.PrefetchScalarGridSpec(
        num_scalar_prefetch=2,
        grid=(n_rows // R, FFN_EXPERT // tf),
        in_specs=[pl.BlockSpec((R, D_MODEL), lambda i, f, be, nu: (i, 0)),
                  pl.BlockSpec((None, D_MODEL, tf), lambda i, f, be, nu: (be[i], 0, f),
                               pipeline_mode=pl.Buffered(1)),
                  pl.BlockSpec((None, D_MODEL, tf), lambda i, f, be, nu: (be[i], 0, f),
                               pipeline_mode=pl.Buffered(1)),
                  pl.BlockSpec((None, tf, D_MODEL), lambda i, f, be, nu: (be[i], f, 0),
                               pipeline_mode=pl.Buffered(1))],
        out_specs=pl.BlockSpec((R, D_MODEL), lambda i, f, be, nu: (i, 0)),
        scratch_shapes=[pltpu.VMEM((R, D_MODEL), F32)],
    )
    return pl.pallas_call(
        _expert_kernel,
        grid_spec=grid_spec,
        out_shape=jax.ShapeDtypeStruct((n_rows, D_MODEL), BF16),
        compiler_params=_params(("parallel", "arbitrary")),
        name="moe_experts",
    )(block_expert, n_used, rows, w1, w3, w2)


def _combine_kernel(x_ref, y1_ref, y2_ref, info_ref, o_ref):
    info = info_ref[...]
    o_ref[...] = (x_ref[...] + info[:, 2:3] * y1_ref[...].astype(F32)
                  + info[:, 3:4] * y2_ref[...].astype(F32))


def _combine(x1, y1, y2, info):
    T = x1.shape[0]
    tm = COMBINE_TM
    row = pl.BlockSpec((tm, D_MODEL), lambda i: (i, 0))
    return pl.pallas_call(
        _combine_kernel,
        grid=(T // tm,),
        in_specs=[row, row, row, pl.BlockSpec((tm, LANES), lambda i: (i, 0))],
        out_specs=row,
        out_shape=jax.ShapeDtypeStruct((T, D_MODEL), F32),
        compiler_params=_params(("parallel",)),
        name="moe_combine",
    )(x1, y1, y2, info)


def _moe(h2, x1, wr, w1, w3, w2):
    T = h2.shape[0]
    R = MOE_ROWS
    n_blocks = -(-(T * TOP_K) // R) + N_EXPERTS
    info, cnt = _router(h2, wr)
    expert = info[:, 0:2].astype(jnp.int32)
    rank = info[:, 4:6].astype(jnp.int32)
    counts = cnt[0, :N_EXPERTS].astype(jnp.int32)
    blocks_per = (counts + R - 1) // R
    bend = jnp.cumsum(blocks_per)
    pstart = (bend - blocks_per) * R
    lanes = jnp.arange(N_EXPERTS, dtype=jnp.int32)
    dest = jnp.sum(jnp.where(expert[..., None] == lanes, pstart, 0), axis=-1) + rank
    n_used = bend[-1:]
    block_expert = jnp.minimum(
        jnp.searchsorted(bend, jnp.arange(n_blocks, dtype=jnp.int32), side='right'),
        N_EXPERTS - 1).astype(jnp.int32)
    n_pad = n_blocks * R - T * TOP_K
    pad_end = jnp.cumsum(blocks_per * R - counts)
    pad_expert = jnp.sum(jnp.arange(n_pad, dtype=jnp.int32)[:, None] >= pad_end[None, :], axis=1).astype(jnp.int32)
    keys = jnp.concatenate([expert.reshape(-1) * 2, pad_expert * 2 + 1])
    token = jnp.broadcast_to(jnp.arange(T, dtype=jnp.int32)[:, None], (T, TOP_K)).reshape(-1)
    vals = jnp.concatenate([token, jnp.zeros((n_pad,), jnp.int32)])
    _, tok_of_row = lax.sort((keys, vals), num_keys=1, is_stable=True)
    cuts = [n_blocks * c // MOE_GATHER_PIECES * R for c in range(MOE_GATHER_PIECES + 1)]
    rows = jnp.concatenate([h2.at[tok_of_row[a:b]].get(mode='promise_in_bounds')
                            for a, b in zip(cuts[:-1], cuts[1:])], axis=0)
    y = _experts(rows, block_expert, n_used, w1, w3, w2)
    y1 = y.at[dest[:, 0]].get(mode='promise_in_bounds')
    y2 = y.at[dest[:, 1]].get(mode='promise_in_bounds')
    return _combine(x1, y1, y2, info)


def _pad_rows(rows, n):
    rows = jnp.stack(rows, axis=-2)
    pad = [(0, 0)] * (rows.ndim - 2) + [(0, n - rows.shape[-2]), (0, 0)]
    return jnp.pad(rows, pad)


def _hgrn_params(lb, norm_g):
    lb = lb.reshape(2, HG_HEADS, HG_DK)
    rows = []
    for d in range(2):
        rows += [jnp.log(jnp.maximum(lb[d], LB_EPS)), jnp.log1p(-lb[d]), 1.0 - lb[d]]
    rows.append(norm_g.reshape(HG_HEADS, HG_DK))
    return _pad_rows(rows, SUBLANES)


def _rg_params(conv_w, conv_b, b_a, b_x, lam):
    ng = RG_WIDTH // LANES
    g = lambda t: t.reshape(ng, LANES)
    rows = [g(conv_w[j]) for j in range(RG_CONV_W)] + [g(conv_b)]
    rows += [g(b_a[0]), g(b_x[0]), g(b_a[1]), g(b_x[1]), g(lam[0]), g(lam[1])]
    return _pad_rows(rows, 2 * SUBLANES)


def _rg_gate_weights(w_a, w_x):
    ng = RG_WIDTH // LANES
    per = RG_BLOCKS // ng

    def dense(w):
        w = w.reshape(ng, per, RG_BLOCK_W, RG_BLOCK_W)
        eye = jnp.eye(per, dtype=w.dtype)
        return jnp.einsum('gpcd,pq->gpcqd', w, eye).reshape(ng, LANES, LANES)

    return jnp.concatenate([dense(w_a[0]), dense(w_x[0]), dense(w_a[1]), dense(w_x[1])], axis=-1).astype(BF16)


def _attn_params(q_g, k_g):
    H = ATT_HEADS_PER_GROUP
    groups = []
    for gi, (_, dil) in enumerate(ATT_GROUPS):
        heads = jnp.arange(gi * H + 1, (gi + 1) * H + 1, dtype=F32)
        slopes = 2.0 ** (-8.0 * heads / ATT_HEADS) * dil
        rows = [jnp.broadcast_to(q_g * (ATT_DH ** -0.5), (H, LANES)), jnp.broadcast_to(k_g, (H, LANES)),
                jnp.broadcast_to(slopes[:, None], (H, LANES))]
        groups.append(_pad_rows(rows, SUBLANES))
    return jnp.stack(groups, axis=1)


def _trunk(x3, p):
    B, S, _ = x3.shape
    x = x3.reshape(B * S, D_MODEL)
    sm = jax.nn.softmax(p['hg_lb_logits'].astype(F32), axis=1)
    lower_bounds = jnp.cumsum(sm, axis=1) - sm[:, :1]
    for l in range(DEPTH):
        proj = _inproj(x, p['norm_mix_g'][l][None], p['w_in'][l].astype(BF16))
        hg = _hgrn(proj, _hgrn_params(lower_bounds[:, l], p['hg_norm_g'][l]), B, S)
        rg = _rglru(proj,
                    _rg_params(p['rg_conv_w'][l], p['rg_conv_b'][l], p['rg_b_a'][l], p['rg_b_x'][l],
                               p['rg_lambda'][l]),
                    _rg_gate_weights(p['rg_w_a'][l], p['rg_w_x'][l]), B, S)
        att = _attention(proj, _attn_params(p['attn_q_g'][l], p['attn_k_g'][l]), B, S)
        x1, h2 = _merge(x, proj, hg, rg, att, p['w_branch'][l].astype(BF16), p['w_out'][l].astype(BF16),
                        p['norm_ffn_g'][l][None])
        j = l // 2
        if l % 2 == 0:
            x = _ffn_dense(h2, x1, p['ffn_w1'][j].astype(BF16), p['ffn_w3'][j].astype(BF16),
                           p['ffn_w2'][j].astype(BF16))
        else:
            wr = jnp.pad(p['moe_router'][j], ((0, 0), (0, LANES - N_EXPERTS))).astype(BF16)
            x = _moe(h2, x1, wr, p['moe_w1'][j].astype(BF16), p['moe_w3'][j].astype(BF16),
                     p['moe_w2'][j].astype(BF16))
    return x.reshape(B, S, D_MODEL)


def kernel(x_prompt, x_sample, norm_mix_g, w_in, hg_lb_logits, hg_norm_g, rg_conv_w, rg_conv_b, rg_w_a, rg_b_a,
           rg_w_x, rg_b_x, rg_lambda, attn_q_g, attn_k_g, w_branch, w_out, norm_ffn_g, ffn_w1, ffn_w3, ffn_w2,
           moe_router, moe_w1, moe_w3, moe_w2):
    p = dict(norm_mix_g=norm_mix_g, w_in=w_in, hg_lb_logits=hg_lb_logits, hg_norm_g=hg_norm_g,
             rg_conv_w=rg_conv_w, rg_conv_b=rg_conv_b, rg_w_a=rg_w_a, rg_b_a=rg_b_a, rg_w_x=rg_w_x,
             rg_b_x=rg_b_x, rg_lambda=rg_lambda, attn_q_g=attn_q_g, attn_k_g=attn_k_g,
             w_branch=w_branch, w_out=w_out, norm_ffn_g=norm_ffn_g, ffn_w1=ffn_w1, ffn_w3=ffn_w3,
             ffn_w2=ffn_w2, moe_router=moe_router, moe_w1=moe_w1, moe_w3=moe_w3, moe_w2=moe_w2)
    return (_trunk(x_prompt, p), _trunk(x_sample, p))
```

```python
import functools
import math

import jax
import jax.numpy as jnp
from jax import lax
from jax.experimental import pallas as pl
from jax.experimental.pallas import tpu as pltpu

F32 = jnp.float32
BF16 = jnp.bfloat16

LANES = 128
SUBLANES = 8
VMEM_LIMIT_BYTES = 56 * 1024 * 1024

D_MODEL = 1024
DEPTH = 2
BRANCH_W = 512
HG_HEADS = 4
HG_DK = 128
HG_CHUNK = 64
HG_HEADS_PER_STEP = 2
HG_LEVELS = 6
LB_EPS = 1e-20
RG_WIDTH = 512
RG_BLOCKS = 8
RG_BLOCK_W = RG_WIDTH // RG_BLOCKS
RG_CONV_W = 4
RG_C = 8.0
RG_TILE = 256
RG_HALO = 8
ATT_GROUPS = ((128, 1), (512, 4), (2048, 16))
ATT_HEADS_PER_GROUP = 4
ATT_HEADS = ATT_HEADS_PER_GROUP * len(ATT_GROUPS)
ATT_DH = 128
ATT_RADIUS = 64
ATT_QBLK = 128
ATT_UNROLL = 8
ATT_TILE = 256
FFN_DENSE = 2816
N_EXPERTS = 8
TOP_K = 2
FFN_EXPERT = 3584
NORM_EPS = 1e-6
NEG_INF = -1e30
LOG2_E = 1.4426950408889634

CB_HQ, CB_HF, CB_HB, CB_HV, CB_HG = 0, 4, 8, 12, 16
CB_RX, CB_RGATE = 20, 24
CB_AQ, CB_AK, CB_AV = 28, 40, 52
CB_GATES = 64
IN_COLS = 11264
N_CB = IN_COLS // LANES

INPROJ_TM, INPROJ_TN = 2048, 1024
MERGE_TM = 512
FFN_TM, FFN_TF = 512, 2816
MOE_ROWS, MOE_TF = 512, 1792
ROUTER_TM = 512
COMBINE_TM = 512


def _params(sem):
    return pltpu.CompilerParams(dimension_semantics=sem, vmem_limit_bytes=VMEM_LIMIT_BYTES)


def _dot(a, b):
    return jnp.dot(a, b, preferred_element_type=F32)


def _dot_nt(a, b):
    return lax.dot_general(a, b, (((1,), (1,)), ((), ())), preferred_element_type=F32)


def _sigmoid(x):
    return jax.nn.sigmoid(x)


def _lockstep(gens):
    results = [None] * len(gens)
    active = list(range(len(gens)))
    while active:
        for idx in list(active):
            try:
                next(gens[idx])
            except StopIteration as stop:
                results[idx] = stop.value
                active.remove(idx)
    return results


def _inproj_kernel(x_ref, g_ref, w_ref, o_ref, h_ref):
    @pl.when(pl.program_id(1) == 0)
    def _():
        x = x_ref[...]
        ms = jnp.mean(x * x, axis=-1, keepdims=True)
        h_ref[...] = (x * lax.rsqrt(ms + NORM_EPS) * g_ref[...]).astype(BF16)

    acc = _dot(h_ref[...], w_ref[...])
    for c in range(INPROJ_TN // LANES):
        o_ref[c] = acc[:, c * LANES:(c + 1) * LANES].astype(BF16)


def _inproj(x, g, w):
    T = x.shape[0]
    return pl.pallas_call(
        _inproj_kernel,
        grid=(T // INPROJ_TM, IN_COLS // INPROJ_TN),
        in_specs=[
            pl.BlockSpec((INPROJ_TM, D_MODEL), lambda i, j: (i, 0)),
            pl.BlockSpec((1, D_MODEL), lambda i, j: (0, 0)),
            pl.BlockSpec((D_MODEL, INPROJ_TN), lambda i, j: (0, j)),
        ],
        out_specs=pl.BlockSpec((INPROJ_TN // LANES, INPROJ_TM, LANES), lambda i, j: (j, i, 0)),
        out_shape=jax.ShapeDtypeStruct((N_CB, T, LANES), BF16),
        scratch_shapes=[pltpu.VMEM((INPROJ_TM, D_MODEL), BF16)],
        compiler_params=_params(("parallel", "arbitrary")),
        name="inproj",
    )(x, g, w)


def _hgrn_chain(q, vb, z, c0, c1, oml, code, sign_ref, states, slot, reverse):
    C = HG_CHUNK
    nv = C // SUBLANES
    e = jnp.exp(-jnp.abs(z))
    log_sig = jnp.minimum(z, 0.0) - jnp.log(1.0 + e)
    t = c1 + log_sig
    log_f = jnp.maximum(c0, t) + jnp.log(1.0 + jnp.exp(-jnp.abs(c0 - t)))
    k = oml * (jnp.where(z >= 0.0, e, 1.0) / (1.0 + e))
    sc_diag = _dot_nt(q.astype(BF16), k.astype(BF16))
    vt = vb.astype(F32).T.astype(BF16)
    yield
    x3 = log_f.reshape(nv, SUBLANES, LANES)
    sub3 = lax.broadcasted_iota(jnp.int32, (nv, SUBLANES, LANES), 1)
    d = 1
    while d < SUBLANES:
        if reverse:
            x3 = x3 + jnp.where(sub3 < SUBLANES - d, pltpu.roll(x3, SUBLANES - d, 1), 0.0)
        else:
            x3 = x3 + jnp.where(sub3 >= d, pltpu.roll(x3, d, 1), 0.0)
        d *= 2
    edge = 0 if reverse else SUBLANES - 1
    tot = jnp.broadcast_to(x3[:, edge:edge + 1, :], (nv, SUBLANES, LANES))
    groups = [None] * nv
    run = None
    for g in (range(nv - 1, -1, -1) if reverse else range(nv)):
        groups[g] = x3[g] if run is None else x3[g] + run
        run = tot[g] if run is None else run + tot[g]
    a3 = jnp.stack(groups) * LOG2_E
    a = a3.reshape(C, LANES)
    sub = lax.broadcasted_iota(jnp.int32, (C, LANES), 0) & (SUBLANES - 1)
    yield
    q_in = (q * jnp.exp2(a)).astype(BF16)
    a_last = a[0:1, :] if reverse else a[C - 1:C, :]
    k_end = (k * jnp.exp2(a_last - a)).astype(BF16)
    st = states[slot]
    o_inter = _dot_nt(q_in, st.astype(BF16))
    states[slot] = st * jnp.exp2(a_last) + _dot(vt, k_end)
    yield

    def pick(r):
        return jnp.broadcast_to(a3[:, r:r + 1, :], (nv, SUBLANES, LANES)).reshape(C, LANES)

    level_scores = []
    for b in range(HG_LEVELS):
        h = 1 << b
        m = h if reverse else h - 1
        if h == 1:
            if reverse:
                ref = jnp.where((sub & 1) == 0, pltpu.roll(a, C - 1, 0), a)
            else:
                ref = jnp.where((sub & 1) == 1, pltpu.roll(a, 1, 0), a)
        elif 2 * h < SUBLANES:
            ref = jnp.where(sub < 2 * h, pick(m), pick(m + 2 * h))
        elif 2 * h == SUBLANES:
            ref = pick(m)
        else:
            pieces = [jnp.broadcast_to(a[blk * 2 * h + m:blk * 2 * h + m + 1, :], (2 * h, LANES))
                      for blk in range(C // (2 * h))]
            ref = jnp.concatenate(pieces, axis=0) if len(pieces) > 1 else pieces[0]
        eb = jnp.exp2(((ref - a) if reverse else (a - ref)) * sign_ref[b])
        level_scores.append(_dot_nt((q * eb).astype(BF16), (k * eb).astype(BF16)))
        yield
    scores = jnp.where(code == HG_LEVELS, sc_diag, 0.0)
    for b in range(HG_LEVELS):
        scores = jnp.where(code == b, level_scores[b], scores)
    o = o_inter + _dot(scores.astype(BF16), vb)
    yield
    return o


def _hgrn_kernel(*refs, S):
    C = HG_CHUNK
    n = S // C
    nh = HG_HEADS_PER_STEP
    head_refs = [refs[5 * hh:5 * hh + 5] for hh in range(nh)]
    par_ref, o_ref, acc_ref, code_ref, sign_ref = refs[5 * nh:]

    ti = lax.broadcasted_iota(jnp.int32, (C, C), 0)
    si = lax.broadcasted_iota(jnp.int32, (C, C), 1)
    x = ti ^ si
    lvl = jnp.zeros((C, C), jnp.int32)
    for b in range(1, HG_LEVELS):
        lvl = lvl + jnp.where(x >= (1 << b), 1, 0)
    diag = jnp.where(ti == si, HG_LEVELS, -1)
    code_ref[0] = jnp.where(ti > si, lvl, diag)
    code_ref[1] = jnp.where(ti < si, lvl, diag)
    row = lax.broadcasted_iota(jnp.int32, (C, LANES), 0)
    for b in range(HG_LEVELS):
        sign_ref[b] = jnp.where((row & (1 << b)) != 0, 1.0, -1.0)

    def chain(ci, hh, d, states):
        q_ref, zf_ref, zb_ref, v_ref, _ = head_refs[hh]
        z_ref = zb_ref if d else zf_ref
        par = par_ref[hh]
        r0 = pl.multiple_of(ci * C, C)
        return _hgrn_chain(q_ref[pl.ds(r0, C), :].astype(F32), v_ref[pl.ds(r0, C), :],
                           z_ref[pl.ds(r0, C), :].astype(F32), par[3 * d:3 * d + 1], par[3 * d + 1:3 * d + 2],
                           par[3 * d + 2:3 * d + 3], code_ref[d], sign_ref, states, 2 * hh + d, d == 1)

    def make_body(accumulate):
        def body(i, carry):
            states = list(carry)
            jobs = []
            for hh in range(nh):
                jobs += [(2 * i, hh, 0), (2 * i + 1, hh, 0), (n - 1 - 2 * i, hh, 1), (n - 2 - 2 * i, hh, 1)]
            outs = _lockstep([chain(ci, hh, d, states) for ci, hh, d in jobs])
            for (ci, hh, _), o in zip(jobs, outs):
                r0 = pl.multiple_of(ci * C, C)
                if accumulate:
                    acc_ref[hh, pl.ds(r0, C), :] += o
                else:
                    acc_ref[hh, pl.ds(r0, C), :] = o
            return tuple(states)
        return body

    zero = jnp.zeros((LANES, HG_DK), F32)
    carry = lax.fori_loop(0, n // 4, make_body(False), (zero,) * (2 * nh))
    lax.fori_loop(n // 4, n // 2, make_body(True), carry)

    ft = 256

    def fin(i, _):
        r0 = pl.multiple_of(i * ft, ft)
        for hh in range(nh):
            o = acc_ref[hh, pl.ds(r0, ft), :]
            y = o * lax.rsqrt(jnp.mean(o * o, axis=-1, keepdims=True) + NORM_EPS) * par_ref[hh][6:7]
            g = head_refs[hh][4][pl.ds(r0, ft), :].astype(F32)
            o_ref[hh, pl.ds(r0, ft), :] = (y * (g * _sigmoid(g))).astype(BF16)
        return 0

    lax.fori_loop(0, S // ft, fin, 0)


def _hgrn(proj, par, B, S):
    nh = HG_HEADS_PER_STEP

    def spec(cb0, hh):
        return pl.BlockSpec((None, S, LANES), lambda b, h: (cb0 + nh * h + hh, b, 0))

    in_specs = []
    for hh in range(nh):
        in_specs += [spec(CB_HQ, hh), spec(CB_HF, hh), spec(CB_HB, hh), spec(CB_HV, hh), spec(CB_HG, hh)]
    in_specs.append(pl.BlockSpec((nh, SUBLANES, LANES), lambda b, h: (h, 0, 0)))
    return pl.pallas_call(
        functools.partial(_hgrn_kernel, S=S),
        grid=(B, HG_HEADS // nh),
        in_specs=in_specs,
        out_specs=pl.BlockSpec((nh, S, LANES), lambda b, h: (h, b, 0)),
        out_shape=jax.ShapeDtypeStruct((HG_HEADS, B * S, LANES), BF16),
        scratch_shapes=[pltpu.VMEM((nh, S, LANES), F32), pltpu.VMEM((2, HG_CHUNK, HG_CHUNK), jnp.int32),
                        pltpu.VMEM((HG_LEVELS, HG_CHUNK, LANES), F32)],
        compiler_params=_params(("parallel", "parallel")),
        name="hgrn2",
    )(*([proj] * (5 * nh)), par)


def _linear_scan_tile(a, u, carry, reverse):
    n = a.shape[0]
    nv = n // SUBLANES
    a3 = a.reshape(nv, SUBLANES, LANES)
    u3 = u.reshape(nv, SUBLANES, LANES)
    sub = lax.broadcasted_iota(jnp.int32, (nv, SUBLANES, LANES), 1)
    d = 1
    while d < SUBLANES:
        if reverse:
            keep = sub < SUBLANES - d
            shift = SUBLANES - d
        else:
            keep = sub >= d
            shift = d
        a_s = jnp.where(keep, pltpu.roll(a3, shift, 1), 1.0)
        u_s = jnp.where(keep, pltpu.roll(u3, shift, 1), 0.0)
        u3 = a3 * u_s + u3
        a3 = a3 * a_s
        d *= 2
        yield
    edge = 0 if reverse else SUBLANES - 1
    a_tot = jnp.broadcast_to(a3[:, edge:edge + 1, :], a3.shape)
    u_tot = jnp.broadcast_to(u3[:, edge:edge + 1, :], u3.shape)
    groups = [None] * nv
    for g in (range(nv - 1, -1, -1) if reverse else range(nv)):
        groups[g] = u3[g] + a3[g] * carry
        carry = u_tot[g] + a_tot[g] * carry
        if g % 4 == 0:
            yield
    return jnp.stack(groups).reshape(n, LANES), carry


def _rg_kernel(x_ref, gate_ref, par_ref, w_ref, o_ref, xpad_ref, xc_ref, h_ref, *, S):
    TS = RG_TILE
    n = S // TS
    par = par_ref[...]
    conv_b = par[4:5]

    xpad_ref[pl.ds(0, RG_HALO), :] = jnp.zeros((RG_HALO, LANES), F32)
    xpad_ref[pl.ds(S + RG_HALO, RG_HALO), :] = jnp.zeros((RG_HALO, LANES), F32)

    def copy(i, _):
        r0 = pl.multiple_of(i * TS, TS)
        xpad_ref[pl.ds(r0 + RG_HALO, TS), :] = x_ref[pl.ds(r0, TS), :].astype(F32)
        return 0

    lax.fori_loop(0, n, copy, 0)

    def conv(i, _):
        r0 = pl.multiple_of(i * TS, TS)
        xc = conv_b
        for j in range(RG_CONV_W):
            off = RG_HALO - RG_CONV_W // 2 + j
            xc = xc + par[j:j + 1] * xpad_ref[pl.ds(r0 + off, TS), :]
        xc_ref[pl.ds(r0, TS), :] = xc
        return 0

    lax.fori_loop(0, n, conv, 0)

    def tile(i, carries, d, final):
        r0 = pl.multiple_of(i * TS, TS)
        xc = xc_ref[pl.ds(r0, TS), :]
        gts = _dot(xc.astype(BF16), w_ref[:, d * 2 * LANES:(d + 1) * 2 * LANES])
        yield
        r = _sigmoid(gts[:, :LANES] + par[5 + 2 * d:6 + 2 * d])
        ig = _sigmoid(gts[:, LANES:] + par[6 + 2 * d:7 + 2 * d])
        lam = par[9 + d:10 + d]
        softplus_neg_lam = jnp.maximum(-lam, 0.0) + jnp.log1p(jnp.exp(-jnp.abs(lam)))
        log_a = (-RG_C) * r * softplus_neg_lam
        a = jnp.exp(log_a)
        y = 1.0 - a * a
        u = jnp.where(y > 0.0, y * lax.rsqrt(y), 0.0) * (ig * xc)
        yield
        h, carries[d] = yield from _linear_scan_tile(a, u, carries[d], reverse=(d == 1))
        yield
        if final:
            gate = gate_ref[pl.ds(r0, TS), :].astype(F32)
            o_ref[pl.ds(r0, TS), :] = (jax.nn.gelu(gate) * (h_ref[pl.ds(r0, TS), :] + h)).astype(BF16)
        else:
            h_ref[pl.ds(r0, TS), :] = h

    def make_body(final):
        def body(i, carry):
            carries = list(carry)
            _lockstep([tile(i, carries, 0, final), tile(n - 1 - i, carries, 1, final)])
            return tuple(carries)
        return body

    zero = jnp.zeros((SUBLANES, LANES), F32)
    carry = lax.fori_loop(0, n // 2, make_body(False), (zero, zero))
    lax.fori_loop(n // 2, n, make_body(True), carry)


def _rglru(proj, par, w4, B, S):
    ng = RG_WIDTH // LANES

    def spec(cb0):
        return pl.BlockSpec((None, S, LANES), lambda b, g: (cb0 + g, b, 0))

    return pl.pallas_call(
        functools.partial(_rg_kernel, S=S),
        grid=(B, ng),
        in_specs=[spec(CB_RX), spec(CB_RGATE),
                  pl.BlockSpec((None, 2 * SUBLANES, LANES), lambda b, g: (g, 0, 0)),
                  pl.BlockSpec((None, LANES, 4 * LANES), lambda b, g: (g, 0, 0))],
        out_specs=pl.BlockSpec((None, S, LANES), lambda b, g: (g, b, 0)),
        out_shape=jax.ShapeDtypeStruct((ng, B * S, LANES), BF16),
        scratch_shapes=[pltpu.VMEM((S + 2 * RG_HALO, LANES), F32), pltpu.VMEM((S, LANES), F32),
                        pltpu.VMEM((S, LANES), F32)],
        compiler_params=_params(("parallel", "parallel")),
        name="rglru",
    )(proj, proj, par, w4)


def _attn_kernel(q0_ref, k0_ref, v0_ref, q1_ref, k1_ref, v1_ref, q2_ref, k2_ref, v2_ref, par_ref, o_ref,
                 perm_ref, qd_ref, kd_ref, vd_ref, bias_ref, og_ref, lg_ref, *, S):
    refs = ((q0_ref, k0_ref, v0_ref), (q1_ref, k1_ref, v1_ref), (q2_ref, k2_ref, v2_ref))
    ct = ATT_TILE

    for g, (_, dil) in enumerate(ATT_GROUPS):
        q_ref, k_ref, v_ref = refs[g]
        par = par_ref[g]
        qg, kg, slope = par[0:1], par[1:2], par[2:3, 0:1]
        L = S // dil
        Q = min(ATT_QBLK, L)
        KW = min(Q + 2 * ATT_RADIUS, L)
        nq = L // Q
        nq_shift = int(math.log2(nq))

        for var in range(3):
            qi = var * ATT_RADIUS + lax.broadcasted_iota(jnp.int32, (Q, KW), 0)
            rel = jnp.abs(qi - lax.broadcasted_iota(jnp.int32, (Q, KW), 1))
            bias_ref[var, :Q, :KW] = jnp.where(rel <= ATT_RADIUS, -slope * rel.astype(F32), NEG_INF)

        per = ct // dil
        if dil > 1:
            pi = lax.broadcasted_iota(jnp.int32, (ct, ct), 0)
            pj = lax.broadcasted_iota(jnp.int32, (ct, ct), 1)
            src_tok = (pi & (per - 1)) * dil + lax.shift_right_logical(pi, int(math.log2(per)))
            perm_ref[...] = jnp.where(pj == src_tok, 1.0, 0.0).astype(BF16)

        def prep_tile(i, q_ref=q_ref, k_ref=k_ref, v_ref=v_ref, qg=qg, kg=kg, dil=dil, per=per, L=L):
            r0 = pl.multiple_of(i * ct, ct)
            xs = (q_ref[pl.ds(r0, ct), :].astype(F32), k_ref[pl.ds(r0, ct), :].astype(F32))
            sums = [jnp.sum(x * x, axis=-1, keepdims=True) for x in xs]
            yield
            tiles = [(x * lax.rsqrt(ss * (1.0 / LANES) + NORM_EPS) * gain).astype(BF16)
                     for x, ss, gain in zip(xs, sums, (qg, kg))]
            tiles.append(v_ref[pl.ds(r0, ct), :])
            if dil > 1:
                y = _dot(perm_ref[...], jnp.concatenate(tiles, axis=1))
                tiles = [y[:, c * LANES:(c + 1) * LANES] for c in range(3)]
                yield
            for dst_ref, y in zip((qd_ref, kd_ref, vd_ref), tiles):
                if dil == 1:
                    dst_ref[pl.ds(r0, ct), :] = y
                else:
                    y = y.astype(BF16)
                    for r in range(dil):
                        dst_ref[pl.ds(pl.multiple_of(r * L + i * per, 16), per), :] = y[r * per:(r + 1) * per, :]

        prep_unroll = min(4, S // ct)

        def prep(i, _, prep_tile=prep_tile, prep_unroll=prep_unroll):
            _lockstep([prep_tile(i * prep_unroll + u) for u in range(prep_unroll)])
            return 0

        lax.fori_loop(0, S // ct // prep_unroll, prep, 0)

        def rows(start, size, dil=dil):
            if dil == 1:
                return pl.ds(start, size)
            return pl.ds(start, size, stride=dil)

        unroll = min(ATT_UNROLL, dil * nq)

        def qblock(it, g=g, dil=dil, L=L, Q=Q, KW=KW, nq=nq, nq_shift=nq_shift, rows=rows):
            r = lax.shift_right_logical(it, nq_shift)
            m0 = (it & (nq - 1)) * Q
            ks = jnp.clip(m0 - ATT_RADIUS, 0, L - KW)
            var = lax.shift_right_logical(m0 - ks, int(math.log2(ATT_RADIUS)))
            base = r * L
            qn = qd_ref[pl.ds(pl.multiple_of(base + m0, 16), Q), :]
            kk = kd_ref[pl.ds(pl.multiple_of(base + ks, 16), KW), :]
            s = _dot_nt(qn, kk)
            yield
            s = s + bias_ref[var, :Q, :KW]
            m = jnp.max(s, axis=-1, keepdims=True)
            yield
            pb = jnp.exp(s - m).astype(BF16)
            vv = vd_ref[pl.ds(pl.multiple_of(base + ks, 16), KW), :]
            ol = _dot(pb, jnp.concatenate([vv, jnp.ones((KW, LANES), BF16)], axis=1))
            yield
            l = ol[:, LANES:]
            og_ref[g, rows(m0 * dil + r, Q), :] = ol[:, :LANES] / l
            lg_ref[g, rows(m0 * dil + r, Q), :] = m + jnp.log(l)

        def qblocks(i, _, unroll=unroll, qblock=qblock):
            _lockstep([qblock(i * unroll + u) for u in range(unroll)])
            return 0

        lax.fori_loop(0, (dil * nq) // unroll, qblocks, 0)

    def merge(i, _):
        r0 = pl.multiple_of(i * ct, ct)
        l0, l1, l2 = (lg_ref[g, pl.ds(r0, ct), :] for g in range(3))
        m = jnp.maximum(jnp.maximum(l0, l1), l2)
        e0, e1, e2 = jnp.exp(l0 - m), jnp.exp(l1 - m), jnp.exp(l2 - m)
        o = (e0 * og_ref[0, pl.ds(r0, ct), :] + e1 * og_ref[1, pl.ds(r0, ct), :]
             + e2 * og_ref[2, pl.ds(r0, ct), :]) / (e0 + e1 + e2)
        o_ref[pl.ds(r0, ct), :] = o.astype(BF16)
        return 0

    lax.fori_loop(0, S // ct, merge, 0)


def _attention(proj, par, B, S):
    H = ATT_HEADS_PER_GROUP
    ngroups = len(ATT_GROUPS)

    def spec(cb0, g):
        return pl.BlockSpec((None, S, LANES), lambda b, j: (cb0 + g * H + j, b, 0))

    in_specs = []
    for g in range(ngroups):
        in_specs += [spec(CB_AQ, g), spec(CB_AK, g), spec(CB_AV, g)]
    in_specs.append(pl.BlockSpec((None, ngroups, SUBLANES, LANES), lambda b, j: (j, 0, 0, 0)))
    return pl.pallas_call(
        functools.partial(_attn_kernel, S=S),
        grid=(B, H),
        in_specs=in_specs,
        out_specs=pl.BlockSpec((None, S, LANES), lambda b, j: (j, b, 0)),
        out_shape=jax.ShapeDtypeStruct((H, B * S, LANES), BF16),
        scratch_shapes=[pltpu.VMEM((ATT_TILE, ATT_TILE), BF16),
                        pltpu.VMEM((S, LANES), BF16), pltpu.VMEM((S, LANES), BF16), pltpu.VMEM((S, LANES), BF16),
                        pltpu.VMEM((3, ATT_QBLK, ATT_QBLK + 2 * ATT_RADIUS), F32),
                        pltpu.VMEM((ngroups, S, LANES), F32), pltpu.VMEM((ngroups, S, LANES), F32)],
        compiler_params=_params(("parallel", "parallel")),
        name="attention",
    )(*([proj] * (3 * ngroups)), par)


def _cat_heads(ref):
    return jnp.concatenate([ref[c] for c in range(ref.shape[0])], axis=-1)


def _merge_kernel(x_ref, hg_ref, rg_ref, att_ref, ga_ref, gb_ref, gc_ref, wb_ref, wo_ref, ng_ref,
                  x1_ref, h2_ref):
    merged = _sigmoid(_cat_heads(ga_ref).astype(F32)) * _dot(_cat_heads(hg_ref), wb_ref[0])
    merged += _sigmoid(_cat_heads(gb_ref).astype(F32)) * _dot(_cat_heads(rg_ref), wb_ref[1])
    merged += _sigmoid(_cat_heads(gc_ref).astype(F32)) * _dot(_cat_heads(att_ref), wb_ref[2])
    x1 = x_ref[...] + _dot(merged.astype(BF16), wo_ref[...])
    x1_ref[...] = x1
    ms = jnp.mean(x1 * x1, axis=-1, keepdims=True)
    h2_ref[...] = (x1 * lax.rsqrt(ms + NORM_EPS) * ng_ref[...]).astype(BF16)


def _merge(x, proj, hg, rg, att, wb, wo, ng):
    T = x.shape[0]
    tm = MERGE_TM
    nb = BRANCH_W // LANES
    ngate = D_MODEL // LANES

    branch = pl.BlockSpec((nb, tm, LANES), lambda i: (0, i, 0))

    def gate(n):
        return pl.BlockSpec((ngate, tm, LANES), lambda i: (CB_GATES // ngate + n, i, 0))

    row = pl.BlockSpec((tm, D_MODEL), lambda i: (i, 0))
    return pl.pallas_call(
        _merge_kernel,
        grid=(T // tm,),
        in_specs=[row, branch, branch, branch, gate(0), gate(1), gate(2),
                  pl.BlockSpec((3, BRANCH_W, D_MODEL), lambda i: (0, 0, 0)),
                  pl.BlockSpec((D_MODEL, D_MODEL), lambda i: (0, 0)),
                  pl.BlockSpec((1, D_MODEL), lambda i: (0, 0))],
        out_specs=[row, row],
        out_shape=[jax.ShapeDtypeStruct((T, D_MODEL), F32), jax.ShapeDtypeStruct((T, D_MODEL), BF16)],
        compiler_params=_params(("parallel",)),
        name="merge",
    )(x, hg, rg, att, proj, proj, proj, wb, wo, ng)


def _ffn_kernel(h_ref, x_ref, w1_ref, w3_ref, w2_ref, o_ref, acc_ref):
    f = pl.program_id(1)
    h = h_ref[...]
    h1 = _dot(h, w1_ref[...])
    y = _dot((h1 * _sigmoid(h1) * _dot(h, w3_ref[...])).astype(BF16), w2_ref[...])

    @pl.when(f == 0)
    def _():
        acc_ref[...] = x_ref[...] + y

    @pl.when(f > 0)
    def _():
        acc_ref[...] += y

    @pl.when(f == pl.num_programs(1) - 1)
    def _():
        o_ref[...] = acc_ref[...]


def _ffn_dense(h2, x1, w1, w3, w2):
    T = h2.shape[0]
    tm, tf = FFN_TM, FFN_TF
    return pl.pallas_call(
        _ffn_kernel,
        grid=(T // tm, FFN_DENSE // tf),
        in_specs=[pl.BlockSpec((tm, D_MODEL), lambda i, f: (i, 0)),
                  pl.BlockSpec((tm, D_MODEL), lambda i, f: (i, 0)),
                  pl.BlockSpec((D_MODEL, tf), lambda i, f: (0, f), pipeline_mode=pl.Buffered(1)),
                  pl.BlockSpec((D_MODEL, tf), lambda i, f: (0, f), pipeline_mode=pl.Buffered(1)),
                  pl.BlockSpec((tf, D_MODEL), lambda i, f: (f, 0), pipeline_mode=pl.Buffered(1))],
        out_specs=pl.BlockSpec((tm, D_MODEL), lambda i, f: (i, 0)),
        out_shape=jax.ShapeDtypeStruct((T, D_MODEL), F32),
        scratch_shapes=[pltpu.VMEM((tm, D_MODEL), F32)],
        compiler_params=_params(("parallel", "arbitrary")),
        name="ffn_dense",
    )(h2, x1, w1, w3, w2)


def _router_kernel(h_ref, wr_ref, o_ref, cnt_ref, base_ref):
    tm = ROUTER_TM

    @pl.when(pl.program_id(0) == 0)
    def _():
        base_ref[...] = jnp.zeros_like(base_ref)

    lane = lax.broadcasted_iota(jnp.int32, (tm, LANES), 1)
    logits = jnp.where(lane < N_EXPERTS, _dot(h_ref[...], wr_ref[...]), -jnp.inf)
    m1 = jnp.max(logits, axis=-1, keepdims=True)
    i1 = jnp.min(jnp.where(logits == m1, lane, LANES), axis=-1, keepdims=True)
    rest = jnp.where(lane == i1, -jnp.inf, logits)
    m2 = jnp.max(rest, axis=-1, keepdims=True)
    i2 = jnp.min(jnp.where(rest == m2, lane, LANES), axis=-1, keepdims=True)
    e21 = jnp.exp(m2 - m1)
    g1 = 1.0 / (1.0 + e21)
    g2 = e21 / (1.0 + e21)

    onehot = jnp.where(lane == i1, 1.0, jnp.where(lane == i2, 1.0, 0.0))
    ti = lax.broadcasted_iota(jnp.int32, (tm, tm), 0)
    si = lax.broadcasted_iota(jnp.int32, (tm, tm), 1)
    before = jnp.where(si < ti, 1.0, 0.0).astype(BF16)
    pos = _dot(before, onehot.astype(BF16)) + base_ref[0:1, :]
    r1 = jnp.sum(jnp.where(lane == i1, pos, 0.0), axis=-1, keepdims=True)
    r2 = jnp.sum(jnp.where(lane == i2, pos, 0.0), axis=-1, keepdims=True)
    total = base_ref[0:1, :] + jnp.sum(onehot, axis=0, keepdims=True)
    base_ref[...] = jnp.broadcast_to(total, base_ref.shape)
    cnt_ref[...] = jnp.broadcast_to(total, cnt_ref.shape)

    out = jnp.where(lane == 0, i1.astype(F32), 0.0)
    out = jnp.where(lane == 1, i2.astype(F32), out)
    out = jnp.where(lane == 2, g1, out)
    out = jnp.where(lane == 3, g2, out)
    out = jnp.where(lane == 4, r1, out)
    out = jnp.where(lane == 5, r2, out)
    o_ref[...] = out


def _router(h2, wr):
    T = h2.shape[0]
    tm = ROUTER_TM
    return pl.pallas_call(
        _router_kernel,
        grid=(T // tm,),
        in_specs=[pl.BlockSpec((tm, D_MODEL), lambda i: (i, 0)),
                  pl.BlockSpec((D_MODEL, LANES), lambda i: (0, 0))],
        out_specs=[pl.BlockSpec((tm, LANES), lambda i: (i, 0)),
                   pl.BlockSpec((SUBLANES, LANES), lambda i: (0, 0))],
        out_shape=[jax.ShapeDtypeStruct((T, LANES), F32), jax.ShapeDtypeStruct((SUBLANES, LANES), F32)],
        scratch_shapes=[pltpu.VMEM((SUBLANES, LANES), F32)],
        compiler_params=_params(("arbitrary",)),
        name="moe_router",
    )(h2, wr)


def _expert_kernel(be_ref, nu_ref, x_ref, w1_ref, w3_ref, w2_ref, o_ref, acc_ref):
    i = pl.program_id(0)
    f = pl.program_id(1)

    @pl.when(f == 0)
    def _():
        acc_ref[...] = jnp.zeros_like(acc_ref)

    @pl.when(i < nu_ref[0])
    def _():
        x = x_ref[...]
        h1 = _dot(x, w1_ref[...])
        acc_ref[...] += _dot((h1 * _sigmoid(h1) * _dot(x, w3_ref[...])).astype(BF16), w2_ref[...])

    @pl.when(f == pl.num_programs(1) - 1)
    def _():
        o_ref[...] = acc_ref[...].astype(BF16)


def _experts(rows, block_expert, n_used, w1, w3, w2):
    n_rows = rows.shape[0]
    R, tf = MOE_ROWS, MOE_TF
    grid_spec = pltpu.PrefetchScalarGridSpec(
        num_scalar_prefetch=2,
        grid=(n_rows // R, FFN_EXPERT // tf),
        in_specs=[pl.BlockSpec((R, D_MODEL), lambda i, f, be, nu: (i, 0)),
                  pl.BlockSpec((None, D_MODEL, tf), lambda i, f, be, nu: (be[i], 0, f)),
                  pl.BlockSpec((None, D_MODEL, tf), lambda i, f, be, nu: (be[i], 0, f)),
                  pl.BlockSpec((None, tf, D_MODEL), lambda i, f, be, nu: (be[i], f, 0))],
        out_specs=pl.BlockSpec((R, D_MODEL), lambda i, f, be, nu: (i, 0)),
        scratch_shapes=[pltpu.VMEM((R, D_MODEL), F32)],
    )
    return pl.pallas_call(
        _expert_kernel,
        grid_spec=grid_spec,
        out_shape=jax.ShapeDtypeStruct((n_rows, D_MODEL), BF16),
        compiler_params=_params(("parallel", "arbitrary")),
        name="moe_experts",
    )(block_expert, n_used, rows, w1, w3, w2)


def _combine_kernel(x_ref, y1_ref, y2_ref, info_ref, o_ref):
    info = info_ref[...]
    o_ref[...] = (x_ref[...] + info[:, 2:3] * y1_ref[...].astype(F32)
                  + info[:, 3:4] * y2_ref[...].astype(F32))


def _combine(x1, y1, y2, info):
    T = x1.shape[0]
    tm = COMBINE_TM
    row = pl.BlockSpec((tm, D_MODEL), lambda i: (i, 0))
    return pl.pallas_call(
        _combine_kernel,
        grid=(T // tm,),
        in_specs=[row, row, row, pl.BlockSpec((tm, LANES), lambda i: (i, 0))],
        out_specs=row,
        out_shape=jax.ShapeDtypeStruct((T, D_MODEL), F32),
        compiler_params=_params(("parallel",)),
        name="moe_combine",
    )(x1, y1, y2, info)


def _moe(h2, x1, wr, w1, w3, w2):
    T = h2.shape[0]
    R = MOE_ROWS
    n_blocks = -(-(T * TOP_K) // R) + N_EXPERTS
    info, cnt = _router(h2, wr)
    expert = info[:, 0:2].astype(jnp.int32)
    rank = info[:, 4:6].astype(jnp.int32)
    counts = cnt[0, :N_EXPERTS].astype(jnp.int32)
    blocks_per = (counts + R - 1) // R
    bend = jnp.cumsum(blocks_per)
    pstart = (bend - blocks_per) * R
    lanes = jnp.arange(N_EXPERTS, dtype=jnp.int32)
    dest = jnp.sum(jnp.where(expert[..., None] == lanes, pstart, 0), axis=-1) + rank
    n_used = bend[-1:]
    block_expert = jnp.minimum(
        jnp.searchsorted(bend, jnp.arange(n_blocks, dtype=jnp.int32), side='right'),
        N_EXPERTS - 1).astype(jnp.int32)
    n_pad = n_blocks * R - T * TOP_K
    pad_end = jnp.cumsum(blocks_per * R - counts)
    pad_expert = jnp.sum(jnp.arange(n_pad, dtype=jnp.int32)[:, None] >= pad_end[None, :], axis=1).astype(jnp.int32)
    keys = jnp.concatenate([expert.reshape(-1) * 2, pad_expert * 2 + 1])
    token = jnp.broadcast_to(jnp.arange(T, dtype=jnp.int32)[:, None], (T, TOP_K)).reshape(-1)
    vals = jnp.concatenate([token, jnp.zeros((n_pad,), jnp.int32)])
    _, tok_of_row = lax.sort((keys, vals), num_keys=1, is_stable=True)
    rows = h2.at[tok_of_row].get(mode='promise_in_bounds')
    y = _experts(rows, block_expert, n_used, w1, w3, w2)
    y1 = y.at[dest[:, 0]].get(mode='promise_in_bounds')
    y2 = y.at[dest[:, 1]].get(mode='promise_in_bounds')
    return _combine(x1, y1, y2, info)


def _pad_rows(rows, n):
    rows = jnp.stack(rows, axis=-2)
    pad = [(0, 0)] * (rows.ndim - 2) + [(0, n - rows.shape[-2]), (0, 0)]
    return jnp.pad(rows, pad)


def _hgrn_params(lb, norm_g):
    lb = lb.reshape(2, HG_HEADS, HG_DK)
    rows = []
    for d in range(2):
        rows += [jnp.log(jnp.maximum(lb[d], LB_EPS)), jnp.log1p(-lb[d]), 1.0 - lb[d]]
    rows.append(norm_g.reshape(HG_HEADS, HG_DK))
    return _pad_rows(rows, SUBLANES)


def _rg_params(conv_w, conv_b, b_a, b_x, lam):
    ng = RG_WIDTH // LANES
    g = lambda t: t.reshape(ng, LANES)
    rows = [g(conv_w[j]) for j in range(RG_CONV_W)] + [g(conv_b)]
    rows += [g(b_a[0]), g(b_x[0]), g(b_a[1]), g(b_x[1]), g(lam[0]), g(lam[1])]
    return _pad_rows(rows, 2 * SUBLANES)


def _rg_gate_weights(w_a, w_x):
    ng = RG_WIDTH // LANES
    per = RG_BLOCKS // ng

    def dense(w):
        w = w.reshape(ng, per, RG_BLOCK_W, RG_BLOCK_W)
        eye = jnp.eye(per, dtype=w.dtype)
        return jnp.einsum('gpcd,pq->gpcqd', w, eye).reshape(ng, LANES, LANES)

    return jnp.concatenate([dense(w_a[0]), dense(w_x[0]), dense(w_a[1]), dense(w_x[1])], axis=-1).astype(BF16)


def _attn_params(q_g, k_g):
    H = ATT_HEADS_PER_GROUP
    groups = []
    for gi, (_, dil) in enumerate(ATT_GROUPS):
        heads = jnp.arange(gi * H + 1, (gi + 1) * H + 1, dtype=F32)
        slopes = 2.0 ** (-8.0 * heads / ATT_HEADS) * dil
        rows = [jnp.broadcast_to(q_g * (ATT_DH ** -0.5), (H, LANES)), jnp.broadcast_to(k_g, (H, LANES)),
                jnp.broadcast_to(slopes[:, None], (H, LANES))]
        groups.append(_pad_rows(rows, SUBLANES))
    return jnp.stack(groups, axis=1)


def _trunk(x3, p):
    B, S, _ = x3.shape
    x = x3.reshape(B * S, D_MODEL)
    sm = jax.nn.softmax(p['hg_lb_logits'].astype(F32), axis=1)
    lower_bounds = jnp.cumsum(sm, axis=1) - sm[:, :1]
    for l in range(DEPTH):
        proj = _inproj(x, p['norm_mix_g'][l][None], p['w_in'][l].astype(BF16))
        hg = _hgrn(proj, _hgrn_params(lower_bounds[:, l], p['hg_norm_g'][l]), B, S)
        rg = _rglru(proj,
                    _rg_params(p['rg_conv_w'][l], p['rg_conv_b'][l], p['rg_b_a'][l], p['rg_b_x'][l],
                               p['rg_lambda'][l]),
                    _rg_gate_weights(p['rg_w_a'][l], p['rg_w_x'][l]), B, S)
        att = _attention(proj, _attn_params(p['attn_q_g'][l], p['attn_k_g'][l]), B, S)
        x1, h2 = _merge(x, proj, hg, rg, att, p['w_branch'][l].astype(BF16), p['w_out'][l].astype(BF16),
                        p['norm_ffn_g'][l][None])
        j = l // 2
        if l % 2 == 0:
            x = _ffn_dense(h2, x1, p['ffn_w1'][j].astype(BF16), p['ffn_w3'][j].astype(BF16),
                           p['ffn_w2'][j].astype(BF16))
        else:
            wr = jnp.pad(p['moe_router'][j], ((0, 0), (0, LANES - N_EXPERTS))).astype(BF16)
            x = _moe(h2, x1, wr, p['moe_w1'][j].astype(BF16), p['moe_w3'][j].astype(BF16),
                     p['moe_w2'][j].astype(BF16))
    return x.reshape(B, S, D_MODEL)


def kernel(x_prompt, x_sample, norm_mix_g, w_in, hg_lb_logits, hg_norm_g, rg_conv_w, rg_conv_b, rg_w_a, rg_b_a,
           rg_w_x, rg_b_x, rg_lambda, attn_q_g, attn_k_g, w_branch, w_out, norm_ffn_g, ffn_w1, ffn_w3, ffn_w2,
           moe_router, moe_w1, moe_w3, moe_w2):
    p = dict(norm_mix_g=norm_mix_g, w_in=w_in, hg_lb_logits=hg_lb_logits, hg_norm_g=hg_norm_g,
             rg_conv_w=rg_conv_w, rg_conv_b=rg_conv_b, rg_w_a=rg_w_a, rg_b_a=rg_b_a, rg_w_x=rg_w_x,
             rg_b_x=rg_b_x, rg_lambda=rg_lambda, attn_q_g=attn_q_g, attn_k_g=attn_k_g,
             w_branch=w_branch, w_out=w_out, norm_ffn_g=norm_ffn_g, ffn_w1=ffn_w1, ffn_w3=ffn_w3,
             ffn_w2=ffn_w2, moe_router=moe_router, moe_w1=moe_w1, moe_w3=moe_w3, moe_w2=moe_w2)
    return (_trunk(x_prompt, p), _trunk(x_sample, p))
```

```python
import functools
import math

import jax
import jax.numpy as jnp
from jax import lax
from jax.experimental import pallas as pl
from jax.experimental.pallas import tpu as pltpu

F32 = jnp.float32
BF16 = jnp.bfloat16

LANES = 128
SUBLANES = 8
VMEM_LIMIT_BYTES = 56 * 1024 * 1024

D_MODEL = 1024
DEPTH = 2
BRANCH_W = 512
HG_HEADS = 4
HG_DK = 128
HG_CHUNK = 64
HG_HEADS_PER_STEP = 2
HG_LEVELS = 6
LB_EPS = 1e-20
RG_WIDTH = 512
RG_BLOCKS = 8
RG_BLOCK_W = RG_WIDTH // RG_BLOCKS
RG_CONV_W = 4
RG_C = 8.0
RG_TILE = 1024
RG_HALO = 8
ATT_GROUPS = ((128, 1), (512, 4), (2048, 16))
ATT_HEADS_PER_GROUP = 4
ATT_HEADS = ATT_HEADS_PER_GROUP * len(ATT_GROUPS)
ATT_DH = 128
ATT_RADIUS = 64
ATT_QBLK = 128
ATT_UNROLL = 16
ATT_TILE = 256
FFN_DENSE = 2816
N_EXPERTS = 8
TOP_K = 2
FFN_EXPERT = 3584
NORM_EPS = 1e-6
NEG_INF = -1e30
LOG2_E = 1.4426950408889634

CB_HQ, CB_HF, CB_HB, CB_HV, CB_HG = 0, 4, 8, 12, 16
CB_RX, CB_RGATE = 20, 24
CB_AQ, CB_AK, CB_AV = 28, 40, 52
CB_GATES = 64
IN_COLS = 11264
N_CB = IN_COLS // LANES

INPROJ_TM, INPROJ_TN = 2048, 1024
MERGE_TM = 512
FFN_TM, FFN_TF = 512, 2816
MOE_ROWS, MOE_TF = 512, 1792
ROUTER_TM = 512
COMBINE_TM = 512


def _params(sem):
    return pltpu.CompilerParams(dimension_semantics=sem, vmem_limit_bytes=VMEM_LIMIT_BYTES)


def _dot(a, b):
    return jnp.dot(a, b, preferred_element_type=F32)


def _dot_nt(a, b):
    return lax.dot_general(a, b, (((1,), (1,)), ((), ())), preferred_element_type=F32)


def _sigmoid(x):
    return jax.nn.sigmoid(x)


def _lockstep(gens):
    results = [None] * len(gens)
    active = list(range(len(gens)))
    while active:
        for idx in list(active):
            try:
                next(gens[idx])
            except StopIteration as stop:
                results[idx] = stop.value
                active.remove(idx)
    return results


def _inproj_kernel(x_ref, g_ref, w_ref, o_ref, h_ref):
    @pl.when(pl.program_id(1) == 0)
    def _():
        x = x_ref[...]
        ms = jnp.mean(x * x, axis=-1, keepdims=True)
        h_ref[...] = (x * lax.rsqrt(ms + NORM_EPS) * g_ref[...]).astype(BF16)

    acc = _dot(h_ref[...], w_ref[...])
    for c in range(INPROJ_TN // LANES):
        o_ref[c] = acc[:, c * LANES:(c + 1) * LANES].astype(BF16)


def _inproj(x, g, w):
    T = x.shape[0]
    return pl.pallas_call(
        _inproj_kernel,
        grid=(T // INPROJ_TM, IN_COLS // INPROJ_TN),
        in_specs=[
            pl.BlockSpec((INPROJ_TM, D_MODEL), lambda i, j: (i, 0)),
            pl.BlockSpec((1, D_MODEL), lambda i, j: (0, 0)),
            pl.BlockSpec((D_MODEL, INPROJ_TN), lambda i, j: (0, j)),
        ],
        out_specs=pl.BlockSpec((INPROJ_TN // LANES, INPROJ_TM, LANES), lambda i, j: (j, i, 0)),
        out_shape=jax.ShapeDtypeStruct((N_CB, T, LANES), BF16),
        scratch_shapes=[pltpu.VMEM((INPROJ_TM, D_MODEL), BF16)],
        compiler_params=_params(("parallel", "arbitrary")),
        name="inproj",
    )(x, g, w)


def _hgrn_chain(q, vb, z, c0, c1, oml, code, sign_ref, states, slot, reverse):
    C = HG_CHUNK
    nv = C // SUBLANES
    e = jnp.exp(-jnp.abs(z))
    log_sig = jnp.minimum(z, 0.0) - jnp.log(1.0 + e)
    t = c1 + log_sig
    log_f = jnp.maximum(c0, t) + jnp.log(1.0 + jnp.exp(-jnp.abs(c0 - t)))
    k = oml * (jnp.where(z >= 0.0, e, 1.0) / (1.0 + e))
    sc_diag = _dot_nt(q.astype(BF16), k.astype(BF16))
    vt = vb.astype(F32).T.astype(BF16)
    yield
    x3 = log_f.reshape(nv, SUBLANES, LANES)
    sub3 = lax.broadcasted_iota(jnp.int32, (nv, SUBLANES, LANES), 1)
    d = 1
    while d < SUBLANES:
        if reverse:
            x3 = x3 + jnp.where(sub3 < SUBLANES - d, pltpu.roll(x3, SUBLANES - d, 1), 0.0)
        else:
            x3 = x3 + jnp.where(sub3 >= d, pltpu.roll(x3, d, 1), 0.0)
        d *= 2
    edge = 0 if reverse else SUBLANES - 1
    tot = jnp.broadcast_to(x3[:, edge:edge + 1, :], (nv, SUBLANES, LANES))
    groups = [None] * nv
    run = None
    for g in (range(nv - 1, -1, -1) if reverse else range(nv)):
        groups[g] = x3[g] if run is None else x3[g] + run
        run = tot[g] if run is None else run + tot[g]
    a3 = jnp.stack(groups) * LOG2_E
    a = a3.reshape(C, LANES)
    sub = lax.broadcasted_iota(jnp.int32, (C, LANES), 0) & (SUBLANES - 1)
    yield
    q_in = (q * jnp.exp2(a)).astype(BF16)
    a_last = a[0:1, :] if reverse else a[C - 1:C, :]
    k_end = (k * jnp.exp2(a_last - a)).astype(BF16)
    st = states[slot]
    o_inter = _dot_nt(q_in, st.astype(BF16))
    states[slot] = st * jnp.exp2(a_last) + _dot(vt, k_end)
    yield

    def pick(r):
        return jnp.broadcast_to(a3[:, r:r + 1, :], (nv, SUBLANES, LANES)).reshape(C, LANES)

    level_scores = []
    for b in range(HG_LEVELS):
        h = 1 << b
        m = h if reverse else h - 1
        if h == 1:
            if reverse:
                ref = jnp.where((sub & 1) == 0, pltpu.roll(a, C - 1, 0), a)
            else:
                ref = jnp.where((sub & 1) == 1, pltpu.roll(a, 1, 0), a)
        elif 2 * h < SUBLANES:
            ref = jnp.where(sub < 2 * h, pick(m), pick(m + 2 * h))
        elif 2 * h == SUBLANES:
            ref = pick(m)
        else:
            pieces = [jnp.broadcast_to(a[blk * 2 * h + m:blk * 2 * h + m + 1, :], (2 * h, LANES))
                      for blk in range(C // (2 * h))]
            ref = jnp.concatenate(pieces, axis=0) if len(pieces) > 1 else pieces[0]
        eb = jnp.exp2(((ref - a) if reverse else (a - ref)) * sign_ref[b])
        level_scores.append(_dot_nt((q * eb).astype(BF16), (k * eb).astype(BF16)))
        yield
    scores = jnp.where(code == HG_LEVELS, sc_diag, 0.0)
    for b in range(HG_LEVELS):
        scores = jnp.where(code == b, level_scores[b], scores)
    o = o_inter + _dot(scores.astype(BF16), vb)
    yield
    return o


def _hgrn_kernel(*refs, S):
    C = HG_CHUNK
    n = S // C
    nh = HG_HEADS_PER_STEP
    head_refs = [refs[5 * hh:5 * hh + 5] for hh in range(nh)]
    par_ref, o_ref, acc_ref, code_ref, sign_ref = refs[5 * nh:]

    ti = lax.broadcasted_iota(jnp.int32, (C, C), 0)
    si = lax.broadcasted_iota(jnp.int32, (C, C), 1)
    x = ti ^ si
    lvl = jnp.zeros((C, C), jnp.int32)
    for b in range(1, HG_LEVELS):
        lvl = lvl + jnp.where(x >= (1 << b), 1, 0)
    diag = jnp.where(ti == si, HG_LEVELS, -1)
    code_ref[0] = jnp.where(ti > si, lvl, diag)
    code_ref[1] = jnp.where(ti < si, lvl, diag)
    row = lax.broadcasted_iota(jnp.int32, (C, LANES), 0)
    for b in range(HG_LEVELS):
        sign_ref[b] = jnp.where((row & (1 << b)) != 0, 1.0, -1.0)

    def chain(ci, hh, d, states):
        q_ref, zf_ref, zb_ref, v_ref, _ = head_refs[hh]
        z_ref = zb_ref if d else zf_ref
        par = par_ref[hh]
        r0 = pl.multiple_of(ci * C, C)
        return _hgrn_chain(q_ref[pl.ds(r0, C), :].astype(F32), v_ref[pl.ds(r0, C), :],
                           z_ref[pl.ds(r0, C), :].astype(F32), par[3 * d:3 * d + 1], par[3 * d + 1:3 * d + 2],
                           par[3 * d + 2:3 * d + 3], code_ref[d], sign_ref, states, 2 * hh + d, d == 1)

    def make_body(accumulate):
        def body(i, carry):
            states = list(carry)
            jobs = []
            for hh in range(nh):
                jobs += [(2 * i, hh, 0), (2 * i + 1, hh, 0), (n - 1 - 2 * i, hh, 1), (n - 2 - 2 * i, hh, 1)]
            outs = _lockstep([chain(ci, hh, d, states) for ci, hh, d in jobs])
            for (ci, hh, _), o in zip(jobs, outs):
                r0 = pl.multiple_of(ci * C, C)
                if accumulate:
                    acc_ref[hh, pl.ds(r0, C), :] += o
                else:
                    acc_ref[hh, pl.ds(r0, C), :] = o
            return tuple(states)
        return body

    zero = jnp.zeros((LANES, HG_DK), F32)
    carry = lax.fori_loop(0, n // 4, make_body(False), (zero,) * (2 * nh))
    lax.fori_loop(n // 4, n // 2, make_body(True), carry)

    ft = 256

    def fin(i, _):
        r0 = pl.multiple_of(i * ft, ft)
        for hh in range(nh):
            o = acc_ref[hh, pl.ds(r0, ft), :]
            y = o * lax.rsqrt(jnp.mean(o * o, axis=-1, keepdims=True) + NORM_EPS) * par_ref[hh][6:7]
            g = head_refs[hh][4][pl.ds(r0, ft), :].astype(F32)
            o_ref[hh, pl.ds(r0, ft), :] = (y * (g * _sigmoid(g))).astype(BF16)
        return 0

    lax.fori_loop(0, S // ft, fin, 0)


def _hgrn(proj, par, B, S):
    nh = HG_HEADS_PER_STEP

    def spec(cb0, hh):
        return pl.BlockSpec((None, S, LANES), lambda b, h: (cb0 + nh * h + hh, b, 0))

    in_specs = []
    for hh in range(nh):
        in_specs += [spec(CB_HQ, hh), spec(CB_HF, hh), spec(CB_HB, hh), spec(CB_HV, hh), spec(CB_HG, hh)]
    in_specs.append(pl.BlockSpec((nh, SUBLANES, LANES), lambda b, h: (h, 0, 0)))
    return pl.pallas_call(
        functools.partial(_hgrn_kernel, S=S),
        grid=(B, HG_HEADS // nh),
        in_specs=in_specs,
        out_specs=pl.BlockSpec((nh, S, LANES), lambda b, h: (h, b, 0)),
        out_shape=jax.ShapeDtypeStruct((HG_HEADS, B * S, LANES), BF16),
        scratch_shapes=[pltpu.VMEM((nh, S, LANES), F32), pltpu.VMEM((2, HG_CHUNK, HG_CHUNK), jnp.int32),
                        pltpu.VMEM((HG_LEVELS, HG_CHUNK, LANES), F32)],
        compiler_params=_params(("parallel", "parallel")),
        name="hgrn2",
    )(*([proj] * (5 * nh)), par)


def _linear_scan_tile(a, u, carry, reverse):
    n = a.shape[0]
    nv = n // SUBLANES
    a3 = a.reshape(nv, SUBLANES, LANES)
    u3 = u.reshape(nv, SUBLANES, LANES)
    sub = lax.broadcasted_iota(jnp.int32, (nv, SUBLANES, LANES), 1)
    d = 1
    while d < SUBLANES:
        if reverse:
            keep = sub < SUBLANES - d
            shift = SUBLANES - d
        else:
            keep = sub >= d
            shift = d
        a_s = jnp.where(keep, pltpu.roll(a3, shift, 1), 1.0)
        u_s = jnp.where(keep, pltpu.roll(u3, shift, 1), 0.0)
        u3 = a3 * u_s + u3
        a3 = a3 * a_s
        d *= 2
        yield
    edge = 0 if reverse else SUBLANES - 1
    a_tot = jnp.broadcast_to(a3[:, edge:edge + 1, :], a3.shape)
    u_tot = jnp.broadcast_to(u3[:, edge:edge + 1, :], u3.shape)
    groups = [None] * nv
    for g in (range(nv - 1, -1, -1) if reverse else range(nv)):
        groups[g] = u3[g] + a3[g] * carry
        carry = u_tot[g] + a_tot[g] * carry
        if g % 4 == 0:
            yield
    return jnp.stack(groups).reshape(n, LANES), carry


def _rg_kernel(x_ref, gate_ref, par_ref, w_ref, o_ref, xpad_ref, xc_ref, h_ref, *, S):
    TS = RG_TILE
    n = S // TS
    par = par_ref[...]
    conv_b = par[4:5]

    xpad_ref[pl.ds(0, RG_HALO), :] = jnp.zeros((RG_HALO, LANES), F32)
    xpad_ref[pl.ds(S + RG_HALO, RG_HALO), :] = jnp.zeros((RG_HALO, LANES), F32)

    def copy(i, _):
        r0 = pl.multiple_of(i * TS, TS)
        xpad_ref[pl.ds(r0 + RG_HALO, TS), :] = x_ref[pl.ds(r0, TS), :].astype(F32)
        return 0

    lax.fori_loop(0, n, copy, 0)

    def conv(i, _):
        r0 = pl.multiple_of(i * TS, TS)
        xc = conv_b
        for j in range(RG_CONV_W):
            off = RG_HALO - RG_CONV_W // 2 + j
            xc = xc + par[j:j + 1] * xpad_ref[pl.ds(r0 + off, TS), :]
        xc_ref[pl.ds(r0, TS), :] = xc
        return 0

    lax.fori_loop(0, n, conv, 0)

    def tile(i, carries, d, final):
        r0 = pl.multiple_of(i * TS, TS)
        xc = xc_ref[pl.ds(r0, TS), :]
        gts = _dot(xc.astype(BF16), w_ref[:, d * 2 * LANES:(d + 1) * 2 * LANES])
        yield
        r = _sigmoid(gts[:, :LANES] + par[5 + 2 * d:6 + 2 * d])
        ig = _sigmoid(gts[:, LANES:] + par[6 + 2 * d:7 + 2 * d])
        lam = par[9 + d:10 + d]
        softplus_neg_lam = jnp.maximum(-lam, 0.0) + jnp.log1p(jnp.exp(-jnp.abs(lam)))
        log_a = (-RG_C) * r * softplus_neg_lam
        a = jnp.exp(log_a)
        y = 1.0 - a * a
        u = jnp.where(y > 0.0, y * lax.rsqrt(y), 0.0) * (ig * xc)
        yield
        h, carries[d] = yield from _linear_scan_tile(a, u, carries[d], reverse=(d == 1))
        yield
        if final:
            gate = gate_ref[pl.ds(r0, TS), :].astype(F32)
            o_ref[pl.ds(r0, TS), :] = (jax.nn.gelu(gate) * (h_ref[pl.ds(r0, TS), :] + h)).astype(BF16)
        else:
            h_ref[pl.ds(r0, TS), :] = h

    def make_body(final):
        def body(i, carry):
            carries = list(carry)
            _lockstep([tile(i, carries, 0, final), tile(n - 1 - i, carries, 1, final)])
            return tuple(carries)
        return body

    zero = jnp.zeros((SUBLANES, LANES), F32)
    carry = lax.fori_loop(0, n // 2, make_body(False), (zero, zero))
    lax.fori_loop(n // 2, n, make_body(True), carry)


def _rglru(proj, par, w4, B, S):
    ng = RG_WIDTH // LANES

    def spec(cb0):
        return pl.BlockSpec((None, S, LANES), lambda b, g: (cb0 + g, b, 0))

    return pl.pallas_call(
        functools.partial(_rg_kernel, S=S),
        grid=(B, ng),
        in_specs=[spec(CB_RX), spec(CB_RGATE),
                  pl.BlockSpec((None, 2 * SUBLANES, LANES), lambda b, g: (g, 0, 0)),
                  pl.BlockSpec((None, LANES, 4 * LANES), lambda b, g: (g, 0, 0))],
        out_specs=pl.BlockSpec((None, S, LANES), lambda b, g: (g, b, 0)),
        out_shape=jax.ShapeDtypeStruct((ng, B * S, LANES), BF16),
        scratch_shapes=[pltpu.VMEM((S + 2 * RG_HALO, LANES), F32), pltpu.VMEM((S, LANES), F32),
                        pltpu.VMEM((S, LANES), F32)],
        compiler_params=_params(("parallel", "parallel")),
        name="rglru",
    )(proj, proj, par, w4)


def _attn_kernel(q0_ref, k0_ref, v0_ref, q1_ref, k1_ref, v1_ref, q2_ref, k2_ref, v2_ref, par_ref, o_ref,
                 perm_ref, qd_ref, kd_ref, vd_ref, bias_ref, og_ref, lg_ref, *, S):
    refs = ((q0_ref, k0_ref, v0_ref), (q1_ref, k1_ref, v1_ref), (q2_ref, k2_ref, v2_ref))
    ct = ATT_TILE

    for g, (_, dil) in enumerate(ATT_GROUPS):
        q_ref, k_ref, v_ref = refs[g]
        par = par_ref[g]
        qg, kg, slope = par[0:1], par[1:2], par[2:3, 0:1]
        L = S // dil
        Q = min(ATT_QBLK, L)
        KW = min(Q + 2 * ATT_RADIUS, L)
        nq = L // Q
        nq_shift = int(math.log2(nq))

        for var in range(3):
            qi = var * ATT_RADIUS + lax.broadcasted_iota(jnp.int32, (Q, KW), 0)
            rel = jnp.abs(qi - lax.broadcasted_iota(jnp.int32, (Q, KW), 1))
            bias_ref[var, :Q, :KW] = jnp.where(rel <= ATT_RADIUS, -slope * rel.astype(F32), NEG_INF)

        per = ct // dil
        if dil > 1:
            pi = lax.broadcasted_iota(jnp.int32, (ct, ct), 0)
            pj = lax.broadcasted_iota(jnp.int32, (ct, ct), 1)
            src_tok = (pi & (per - 1)) * dil + lax.shift_right_logical(pi, int(math.log2(per)))
            perm_ref[...] = jnp.where(pj == src_tok, 1.0, 0.0).astype(BF16)

        def prep_tile(i, q_ref=q_ref, k_ref=k_ref, v_ref=v_ref, qg=qg, kg=kg, dil=dil, per=per, L=L):
            r0 = pl.multiple_of(i * ct, ct)
            xs = (q_ref[pl.ds(r0, ct), :].astype(F32), k_ref[pl.ds(r0, ct), :].astype(F32))
            sums = [jnp.sum(x * x, axis=-1, keepdims=True) for x in xs]
            yield
            tiles = [(x * lax.rsqrt(ss * (1.0 / LANES) + NORM_EPS) * gain).astype(BF16)
                     for x, ss, gain in zip(xs, sums, (qg, kg))]
            tiles.append(v_ref[pl.ds(r0, ct), :])
            if dil > 1:
                y = _dot(perm_ref[...], jnp.concatenate(tiles, axis=1))
                tiles = [y[:, c * LANES:(c + 1) * LANES] for c in range(3)]
                yield
            for dst_ref, y in zip((qd_ref, kd_ref, vd_ref), tiles):
                if dil == 1:
                    dst_ref[pl.ds(r0, ct), :] = y
                else:
                    y = y.astype(BF16)
                    for r in range(dil):
                        dst_ref[pl.ds(pl.multiple_of(r * L + i * per, 16), per), :] = y[r * per:(r + 1) * per, :]

        prep_unroll = min(4, S // ct)

        def prep(i, _, prep_tile=prep_tile, prep_unroll=prep_unroll):
            _lockstep([prep_tile(i * prep_unroll + u) for u in range(prep_unroll)])
            return 0

        lax.fori_loop(0, S // ct // prep_unroll, prep, 0)

        def rows(start, size, dil=dil):
            if dil == 1:
                return pl.ds(start, size)
            return pl.ds(start, size, stride=dil)

        unroll = min(ATT_UNROLL, dil * nq)

        def qblock(it, g=g, dil=dil, L=L, Q=Q, KW=KW, nq=nq, nq_shift=nq_shift, rows=rows):
            r = lax.shift_right_logical(it, nq_shift)
            m0 = (it & (nq - 1)) * Q
            ks = jnp.clip(m0 - ATT_RADIUS, 0, L - KW)
            var = lax.shift_right_logical(m0 - ks, int(math.log2(ATT_RADIUS)))
            base = r * L
            qn = qd_ref[pl.ds(pl.multiple_of(base + m0, 16), Q), :]
            kk = kd_ref[pl.ds(pl.multiple_of(base + ks, 16), KW), :]
            s = _dot_nt(qn, kk)
            yield
            s = s + bias_ref[var, :Q, :KW]
            m = jnp.max(s, axis=-1, keepdims=True)
            yield
            pb = jnp.exp(s - m).astype(BF16)
            vv = vd_ref[pl.ds(pl.multiple_of(base + ks, 16), KW), :]
            ol = _dot(pb, jnp.concatenate([vv, jnp.ones((KW, LANES), BF16)], axis=1))
            yield
            l = ol[:, LANES:]
            og_ref[g, rows(m0 * dil + r, Q), :] = ol[:, :LANES] / l
            lg_ref[g, rows(m0 * dil + r, Q), :] = m + jnp.log(l)

        def qblocks(i, _, unroll=unroll, qblock=qblock):
            _lockstep([qblock(i * unroll + u) for u in range(unroll)])
            return 0

        lax.fori_loop(0, (dil * nq) // unroll, qblocks, 0)

    def merge(i, _):
        r0 = pl.multiple_of(i * ct, ct)
        l0, l1, l2 = (lg_ref[g, pl.ds(r0, ct), :] for g in range(3))
        m = jnp.maximum(jnp.maximum(l0, l1), l2)
        e0, e1, e2 = jnp.exp(l0 - m), jnp.exp(l1 - m), jnp.exp(l2 - m)
        o = (e0 * og_ref[0, pl.ds(r0, ct), :] + e1 * og_ref[1, pl.ds(r0, ct), :]
             + e2 * og_ref[2, pl.ds(r0, ct), :]) / (e0 + e1 + e2)
        o_ref[pl.ds(r0, ct), :] = o.astype(BF16)
        return 0

    lax.fori_loop(0, S // ct, merge, 0)


def _attention(proj, par, B, S):
    H = ATT_HEADS_PER_GROUP
    ngroups = len(ATT_GROUPS)

    def spec(cb0, g):
        return pl.BlockSpec((None, S, LANES), lambda b, j: (cb0 + g * H + j, b, 0))

    in_specs = []
    for g in range(ngroups):
        in_specs += [spec(CB_AQ, g), spec(CB_AK, g), spec(CB_AV, g)]
    in_specs.append(pl.BlockSpec((None, ngroups, SUBLANES, LANES), lambda b, j: (j, 0, 0, 0)))
    return pl.pallas_call(
        functools.partial(_attn_kernel, S=S),
        grid=(B, H),
        in_specs=in_specs,
        out_specs=pl.BlockSpec((None, S, LANES), lambda b, j: (j, b, 0)),
        out_shape=jax.ShapeDtypeStruct((H, B * S, LANES), BF16),
        scratch_shapes=[pltpu.VMEM((ATT_TILE, ATT_TILE), BF16),
                        pltpu.VMEM((S, LANES), BF16), pltpu.VMEM((S, LANES), BF16), pltpu.VMEM((S, LANES), BF16),
                        pltpu.VMEM((3, ATT_QBLK, ATT_QBLK + 2 * ATT_RADIUS), F32),
                        pltpu.VMEM((ngroups, S, LANES), F32), pltpu.VMEM((ngroups, S, LANES), F32)],
        compiler_params=_params(("parallel", "parallel")),
        name="attention",
    )(*([proj] * (3 * ngroups)), par)


def _cat_heads(ref):
    return jnp.concatenate([ref[c] for c in range(ref.shape[0])], axis=-1)


def _merge_kernel(x_ref, hg_ref, rg_ref, att_ref, ga_ref, gb_ref, gc_ref, wb_ref, wo_ref, ng_ref,
                  x1_ref, h2_ref):
    merged = _sigmoid(_cat_heads(ga_ref).astype(F32)) * _dot(_cat_heads(hg_ref), wb_ref[0])
    merged += _sigmoid(_cat_heads(gb_ref).astype(F32)) * _dot(_cat_heads(rg_ref), wb_ref[1])
    merged += _sigmoid(_cat_heads(gc_ref).astype(F32)) * _dot(_cat_heads(att_ref), wb_ref[2])
    x1 = x_ref[...] + _dot(merged.astype(BF16), wo_ref[...])
    x1_ref[...] = x1
    ms = jnp.mean(x1 * x1, axis=-1, keepdims=True)
    h2_ref[...] = (x1 * lax.rsqrt(ms + NORM_EPS) * ng_ref[...]).astype(BF16)


def _merge(x, proj, hg, rg, att, wb, wo, ng):
    T = x.shape[0]
    tm = MERGE_TM
    nb = BRANCH_W // LANES
    ngate = D_MODEL // LANES

    branch = pl.BlockSpec((nb, tm, LANES), lambda i: (0, i, 0))

    def gate(n):
        return pl.BlockSpec((ngate, tm, LANES), lambda i: (CB_GATES // ngate + n, i, 0))

    row = pl.BlockSpec((tm, D_MODEL), lambda i: (i, 0))
    return pl.pallas_call(
        _merge_kernel,
        grid=(T // tm,),
        in_specs=[row, branch, branch, branch, gate(0), gate(1), gate(2),
                  pl.BlockSpec((3, BRANCH_W, D_MODEL), lambda i: (0, 0, 0)),
                  pl.BlockSpec((D_MODEL, D_MODEL), lambda i: (0, 0)),
                  pl.BlockSpec((1, D_MODEL), lambda i: (0, 0))],
        out_specs=[row, row],
        out_shape=[jax.ShapeDtypeStruct((T, D_MODEL), F32), jax.ShapeDtypeStruct((T, D_MODEL), BF16)],
        compiler_params=_params(("parallel",)),
        name="merge",
    )(x, hg, rg, att, proj, proj, proj, wb, wo, ng)


def _ffn_kernel(h_ref, x_ref, w1_ref, w3_ref, w2_ref, o_ref, acc_ref):
    f = pl.program_id(1)
    h = h_ref[...]
    h1 = _dot(h, w1_ref[...])
    y = _dot((h1 * _sigmoid(h1) * _dot(h, w3_ref[...])).astype(BF16), w2_ref[...])

    @pl.when(f == 0)
    def _():
        acc_ref[...] = x_ref[...] + y

    @pl.when(f > 0)
    def _():
        acc_ref[...] += y

    @pl.when(f == pl.num_programs(1) - 1)
    def _():
        o_ref[...] = acc_ref[...]


def _ffn_dense(h2, x1, w1, w3, w2):
    T = h2.shape[0]
    tm, tf = FFN_TM, FFN_TF
    return pl.pallas_call(
        _ffn_kernel,
        grid=(T // tm, FFN_DENSE // tf),
        in_specs=[pl.BlockSpec((tm, D_MODEL), lambda i, f: (i, 0)),
                  pl.BlockSpec((tm, D_MODEL), lambda i, f: (i, 0)),
                  pl.BlockSpec((D_MODEL, tf), lambda i, f: (0, f), pipeline_mode=pl.Buffered(1)),
                  pl.BlockSpec((D_MODEL, tf), lambda i, f: (0, f), pipeline_mode=pl.Buffered(1)),
                  pl.BlockSpec((tf, D_MODEL), lambda i, f: (f, 0), pipeline_mode=pl.Buffered(1))],
        out_specs=pl.BlockSpec((tm, D_MODEL), lambda i, f: (i, 0)),
        out_shape=jax.ShapeDtypeStruct((T, D_MODEL), F32),
        scratch_shapes=[pltpu.VMEM((tm, D_MODEL), F32)],
        compiler_params=_params(("parallel", "arbitrary")),
        name="ffn_dense",
    )(h2, x1, w1, w3, w2)


def _router_kernel(h_ref, wr_ref, o_ref, cnt_ref, base_ref):
    tm = ROUTER_TM

    @pl.when(pl.program_id(0) == 0)
    def _():
        base_ref[...] = jnp.zeros_like(base_ref)

    lane = lax.broadcasted_iota(jnp.int32, (tm, LANES), 1)
    logits = jnp.where(lane < N_EXPERTS, _dot(h_ref[...], wr_ref[...]), -jnp.inf)
    m1 = jnp.max(logits, axis=-1, keepdims=True)
    i1 = jnp.min(jnp.where(logits == m1, lane, LANES), axis=-1, keepdims=True)
    rest = jnp.where(lane == i1, -jnp.inf, logits)
    m2 = jnp.max(rest, axis=-1, keepdims=True)
    i2 = jnp.min(jnp.where(rest == m2, lane, LANES), axis=-1, keepdims=True)
    e21 = jnp.exp(m2 - m1)
    g1 = 1.0 / (1.0 + e21)
    g2 = e21 / (1.0 + e21)

    onehot = jnp.where(lane == i1, 1.0, jnp.where(lane == i2, 1.0, 0.0))
    ti = lax.broadcasted_iota(jnp.int32, (tm, tm), 0)
    si = lax.broadcasted_iota(jnp.int32, (tm, tm), 1)
    before = jnp.where(si < ti, 1.0, 0.0).astype(BF16)
    pos = _dot(before, onehot.astype(BF16)) + base_ref[0:1, :]
    r1 = jnp.sum(jnp.where(lane == i1, pos, 0.0), axis=-1, keepdims=True)
    r2 = jnp.sum(jnp.where(lane == i2, pos, 0.0), axis=-1, keepdims=True)
    total = base_ref[0:1, :] + jnp.sum(onehot, axis=0, keepdims=True)
    base_ref[...] = jnp.broadcast_to(total, base_ref.shape)
    cnt_ref[...] = jnp.broadcast_to(total, cnt_ref.shape)

    out = jnp.where(lane == 0, i1.astype(F32), 0.0)
    out = jnp.where(lane == 1, i2.astype(F32), out)
    out = jnp.where(lane == 2, g1, out)
    out = jnp.where(lane == 3, g2, out)
    out = jnp.where(lane == 4, r1, out)
    out = jnp.where(lane == 5, r2, out)
    o_ref[...] = out


def _router(h2, wr):
    T = h2.shape[0]
    tm = ROUTER_TM
    return pl.pallas_call(
        _router_kernel,
        grid=(T // tm,),
        in_specs=[pl.BlockSpec((tm, D_MODEL), lambda i: (i, 0)),
                  pl.BlockSpec((D_MODEL, LANES), lambda i: (0, 0))],
        out_specs=[pl.BlockSpec((tm, LANES), lambda i: (i, 0)),
                   pl.BlockSpec((SUBLANES, LANES), lambda i: (0, 0))],
        out_shape=[jax.ShapeDtypeStruct((T, LANES), F32), jax.ShapeDtypeStruct((SUBLANES, LANES), F32)],
        scratch_shapes=[pltpu.VMEM((SUBLANES, LANES), F32)],
        compiler_params=_params(("arbitrary",)),
        name="moe_router",
    )(h2, wr)


def _expert_kernel(be_ref, nu_ref, x_ref, w1_ref, w3_ref, w2_ref, o_ref, acc_ref):
    i = pl.program_id(0)
    f = pl.program_id(1)

    @pl.when(f == 0)
    def _():
        acc_ref[...] = jnp.zeros_like(acc_ref)

    @pl.when(i < nu_ref[0])
    def _():
        x = x_ref[...]
        h1 = _dot(x, w1_ref[...])
        acc_ref[...] += _dot((h1 * _sigmoid(h1) * _dot(x, w3_ref[...])).astype(BF16), w2_ref[...])

    @pl.when(f == pl.num_programs(1) - 1)
    def _():
        o_ref[...] = acc_ref[...].astype(BF16)


def _experts(rows, block_expert, n_used, w1, w3, w2):
    n_rows = rows.shape[0]
    R, tf = MOE_ROWS, MOE_TF
    grid_spec = pltpu.PrefetchScalarGridSpec(
        num_scalar_prefetch=2,
        grid=(n_rows // R, FFN_EXPERT // tf),
        in_specs=[pl.BlockSpec((R, D_MODEL), lambda i, f, be, nu: (i, 0)),
                  pl.BlockSpec((None, D_MODEL, tf), lambda i, f, be, nu: (be[i], 0, f)),
                  pl.BlockSpec((None, D_MODEL, tf), lambda i, f, be, nu: (be[i], 0, f)),
                  pl.BlockSpec((None, tf, D_MODEL), lambda i, f, be, nu: (be[i], f, 0))],
        out_specs=pl.BlockSpec((R, D_MODEL), lambda i, f, be, nu: (i, 0)),
        scratch_shapes=[pltpu.VMEM((R, D_MODEL), F32)],
    )
    return pl.pallas_call(
        _expert_kernel,
        grid_spec=grid_spec,
        out_shape=jax.ShapeDtypeStruct((n_rows, D_MODEL), BF16),
        compiler_params=_params(("parallel", "arbitrary")),
        name="moe_experts",
    )(block_expert, n_used, rows, w1, w3, w2)


def _combine_kernel(x_ref, y1_ref, y2_ref, info_ref, o_ref):
    info = info_ref[...]
    o_ref[...] = (x_ref[...] + info[:, 2:3] * y1_ref[...].astype(F32)
                  + info[:, 3:4] * y2_ref[...].astype(F32))


def _combine(x1, y1, y2, info):
    T = x1.shape[0]
    tm = COMBINE_TM
    row = pl.BlockSpec((tm, D_MODEL), lambda i: (i, 0))
    return pl.pallas_call(
        _combine_kernel,
        grid=(T // tm,),
        in_specs=[row, row, row, pl.BlockSpec((tm, LANES), lambda i: (i, 0))],
        out_specs=row,
        out_shape=jax.ShapeDtypeStruct((T, D_MODEL), F32),
        compiler_params=_params(("parallel",)),
        name="moe_combine",
    )(x1, y1, y2, info)


def _moe(h2, x1, wr, w1, w3, w2):
    T = h2.shape[0]
    R = MOE_ROWS
    n_blocks = -(-(T * TOP_K) // R) + N_EXPERTS
    info, cnt = _router(h2, wr)
    expert = info[:, 0:2].astype(jnp.int32)
    rank = info[:, 4:6].astype(jnp.int32)
    counts = cnt[0, :N_EXPERTS].astype(jnp.int32)
    blocks_per = (counts + R - 1) // R
    bend = jnp.cumsum(blocks_per)
    pstart = (bend - blocks_per) * R
    lanes = jnp.arange(N_EXPERTS, dtype=jnp.int32)
    dest = jnp.sum(jnp.where(expert[..., None] == lanes, pstart, 0), axis=-1) + rank
    n_used = bend[-1:]
    block_expert = jnp.minimum(
        jnp.searchsorted(bend, jnp.arange(n_blocks, dtype=jnp.int32), side='right'),
        N_EXPERTS - 1).astype(jnp.int32)
    n_pad = n_blocks * R - T * TOP_K
    pad_end = jnp.cumsum(blocks_per * R - counts)
    pad_expert = jnp.sum(jnp.arange(n_pad, dtype=jnp.int32)[:, None] >= pad_end[None, :], axis=1).astype(jnp.int32)
    keys = jnp.concatenate([expert.reshape(-1) * 2, pad_expert * 2 + 1])
    token = jnp.broadcast_to(jnp.arange(T, dtype=jnp.int32)[:, None], (T, TOP_K)).reshape(-1)
    vals = jnp.concatenate([token, jnp.zeros((n_pad,), jnp.int32)])
    _, tok_of_row = lax.sort((keys, vals), num_keys=1, is_stable=True)
    rows = h2.at[tok_of_row].get(mode='promise_in_bounds')
    y = _experts(rows, block_expert, n_used, w1, w3, w2)
    y1 = y.at[dest[:, 0]].get(mode='promise_in_bounds')
    y2 = y.at[dest[:, 1]].get(mode='promise_in_bounds')
    return _combine(x1, y1, y2, info)


def _pad_rows(rows, n):
    rows = jnp.stack(rows, axis=-2)
    pad = [(0, 0)] * (rows.ndim - 2) + [(0, n - rows.shape[-2]), (0, 0)]
    return jnp.pad(rows, pad)


def _hgrn_params(lb, norm_g):
    lb = lb.reshape(2, HG_HEADS, HG_DK)
    rows = []
    for d in range(2):
        rows += [jnp.log(jnp.maximum(lb[d], LB_EPS)), jnp.log1p(-lb[d]), 1.0 - lb[d]]
    rows.append(norm_g.reshape(HG_HEADS, HG_DK))
    return _pad_rows(rows, SUBLANES)


def _rg_params(conv_w, conv_b, b_a, b_x, lam):
    ng = RG_WIDTH // LANES
    g = lambda t: t.reshape(ng, LANES)
    rows = [g(conv_w[j]) for j in range(RG_CONV_W)] + [g(conv_b)]
    rows += [g(b_a[0]), g(b_x[0]), g(b_a[1]), g(b_x[1]), g(lam[0]), g(lam[1])]
    return _pad_rows(rows, 2 * SUBLANES)


def _rg_gate_weights(w_a, w_x):
    ng = RG_WIDTH // LANES
    per = RG_BLOCKS // ng

    def dense(w):
        w = w.reshape(ng, per, RG_BLOCK_W, RG_BLOCK_W)
        eye = jnp.eye(per, dtype=w.dtype)
        return jnp.einsum('gpcd,pq->gpcqd', w, eye).reshape(ng, LANES, LANES)

    return jnp.concatenate([dense(w_a[0]), dense(w_x[0]), dense(w_a[1]), dense(w_x[1])], axis=-1).astype(BF16)


def _attn_params(q_g, k_g):
    H = ATT_HEADS_PER_GROUP
    groups = []
    for gi, (_, dil) in enumerate(ATT_GROUPS):
        heads = jnp.arange(gi * H + 1, (gi + 1) * H + 1, dtype=F32)
        slopes = 2.0 ** (-8.0 * heads / ATT_HEADS) * dil
        rows = [jnp.broadcast_to(q_g * (ATT_DH ** -0.5), (H, LANES)), jnp.broadcast_to(k_g, (H, LANES)),
                jnp.broadcast_to(slopes[:, None], (H, LANES))]
        groups.append(_pad_rows(rows, SUBLANES))
    return jnp.stack(groups, axis=1)


def _trunk(x3, p):
    B, S, _ = x3.shape
    x = x3.reshape(B * S, D_MODEL)
    sm = jax.nn.softmax(p['hg_lb_logits'].astype(F32), axis=1)
    lower_bounds = jnp.cumsum(sm, axis=1) - sm[:, :1]
    for l in range(DEPTH):
        proj = _inproj(x, p['norm_mix_g'][l][None], p['w_in'][l].astype(BF16))
        hg = _hgrn(proj, _hgrn_params(lower_bounds[:, l], p['hg_norm_g'][l]), B, S)
        rg = _rglru(proj,
                    _rg_params(p['rg_conv_w'][l], p['rg_conv_b'][l], p['rg_b_a'][l], p['rg_b_x'][l],
                               p['rg_lambda'][l]),
                    _rg_gate_weights(p['rg_w_a'][l], p['rg_w_x'][l]), B, S)
        att = _attention(proj, _attn_params(p['attn_q_g'][l], p['attn_k_g'][l]), B, S)
        x1, h2 = _merge(x, proj, hg, rg, att, p['w_branch'][l].astype(BF16), p['w_out'][l].astype(BF16),
                        p['norm_ffn_g'][l][None])
        j = l // 2
        if l % 2 == 0:
            x = _ffn_dense(h2, x1, p['ffn_w1'][j].astype(BF16), p['ffn_w3'][j].astype(BF16),
                           p['ffn_w2'][j].astype(BF16))
        else:
            wr = jnp.pad(p['moe_router'][j], ((0, 0), (0, LANES - N_EXPERTS))).astype(BF16)
            x = _moe(h2, x1, wr, p['moe_w1'][j].astype(BF16), p['moe_w3'][j].astype(BF16),
                     p['moe_w2'][j].astype(BF16))
    return x.reshape(B, S, D_MODEL)


def kernel(x_prompt, x_sample, norm_mix_g, w_in, hg_lb_logits, hg_norm_g, rg_conv_w, rg_conv_b, rg_w_a, rg_b_a,
           rg_w_x, rg_b_x, rg_lambda, attn_q_g, attn_k_g, w_branch, w_out, norm_ffn_g, ffn_w1, ffn_w3, ffn_w2,
           moe_router, moe_w1, moe_w3, moe_w2):
    p = dict(norm_mix_g=norm_mix_g, w_in=w_in, hg_lb_logits=hg_lb_logits, hg_norm_g=hg_norm_g,
             rg_conv_w=rg_conv_w, rg_conv_b=rg_conv_b, rg_w_a=rg_w_a, rg_b_a=rg_b_a, rg_w_x=rg_w_x,
             rg_b_x=rg_b_x, rg_lambda=rg_lambda, attn_q_g=attn_q_g, attn_k_g=attn_k_g,
             w_branch=w_branch, w_out=w_out, norm_ffn_g=norm_ffn_g, ffn_w1=ffn_w1, ffn_w3=ffn_w3,
             ffn_w2=ffn_w2, moe_router=moe_router, moe_w1=moe_w1, moe_w3=moe_w3, moe_w2=moe_w2)
    return (_trunk(x_prompt, p), _trunk(x_sample, p))
```

```python
import functools
import math

import jax
import jax.numpy as jnp
from jax import lax
from jax.experimental import pallas as pl
from jax.experimental.pallas import tpu as pltpu

F32 = jnp.float32
BF16 = jnp.bfloat16

LANES = 128
SUBLANES = 8
VMEM_LIMIT_BYTES = 56 * 1024 * 1024

D_MODEL = 1024
DEPTH = 2
BRANCH_W = 512
HG_HEADS = 4
HG_DK = 128
HG_CHUNK = 64
HG_HEADS_PER_STEP = 2
HG_LEVELS = 6
LB_EPS = 1e-20
RG_WIDTH = 512
RG_BLOCKS = 8
RG_BLOCK_W = RG_WIDTH // RG_BLOCKS
RG_CONV_W = 4
RG_C = 8.0
RG_TILE = 1024
RG_HALO = 8
ATT_GROUPS = ((128, 1), (512, 4), (2048, 16))
ATT_HEADS_PER_GROUP = 4
ATT_HEADS = ATT_HEADS_PER_GROUP * len(ATT_GROUPS)
ATT_DH = 128
ATT_RADIUS = 64
ATT_QBLK = 128
ATT_UNROLL = 16
ATT_PREP_UNROLL = 8
ATT_TILE = 256
FFN_DENSE = 2816
N_EXPERTS = 8
TOP_K = 2
FFN_EXPERT = 3584
NORM_EPS = 1e-6
NEG_INF = -1e30
LOG2_E = 1.4426950408889634

CB_HQ, CB_HF, CB_HB, CB_HV, CB_HG = 0, 4, 8, 12, 16
CB_RX, CB_RGATE = 20, 24
CB_AQ, CB_AK, CB_AV = 28, 40, 52
CB_GATES = 64
IN_COLS = 11264
N_CB = IN_COLS // LANES

INPROJ_TM, INPROJ_TN = 2048, 1024
MERGE_TM = 512
FFN_TM, FFN_TF = 512, 2816
MOE_ROWS, MOE_TF = 512, 1792
ROUTER_TM = 512
COMBINE_TM = 512


def _params(sem):
    return pltpu.CompilerParams(dimension_semantics=sem, vmem_limit_bytes=VMEM_LIMIT_BYTES)


def _dot(a, b):
    return jnp.dot(a, b, preferred_element_type=F32)


def _dot_nt(a, b):
    return lax.dot_general(a, b, (((1,), (1,)), ((), ())), preferred_element_type=F32)


def _sigmoid(x):
    return jax.nn.sigmoid(x)


def _lockstep(gens):
    results = [None] * len(gens)
    active = list(range(len(gens)))
    while active:
        for idx in list(active):
            try:
                next(gens[idx])
            except StopIteration as stop:
                results[idx] = stop.value
                active.remove(idx)
    return results


def _inproj_kernel(x_ref, g_ref, w_ref, o_ref, h_ref):
    @pl.when(pl.program_id(1) == 0)
    def _():
        x = x_ref[...]
        ms = jnp.mean(x * x, axis=-1, keepdims=True)
        h_ref[...] = (x * lax.rsqrt(ms + NORM_EPS) * g_ref[...]).astype(BF16)

    acc = _dot(h_ref[...], w_ref[...])
    for c in range(INPROJ_TN // LANES):
        o_ref[c] = acc[:, c * LANES:(c + 1) * LANES].astype(BF16)


def _inproj(x, g, w):
    T = x.shape[0]
    return pl.pallas_call(
        _inproj_kernel,
        grid=(T // INPROJ_TM, IN_COLS // INPROJ_TN),
        in_specs=[
            pl.BlockSpec((INPROJ_TM, D_MODEL), lambda i, j: (i, 0)),
            pl.BlockSpec((1, D_MODEL), lambda i, j: (0, 0)),
            pl.BlockSpec((D_MODEL, INPROJ_TN), lambda i, j: (0, j)),
        ],
        out_specs=pl.BlockSpec((INPROJ_TN // LANES, INPROJ_TM, LANES), lambda i, j: (j, i, 0)),
        out_shape=jax.ShapeDtypeStruct((N_CB, T, LANES), BF16),
        scratch_shapes=[pltpu.VMEM((INPROJ_TM, D_MODEL), BF16)],
        compiler_params=_params(("parallel", "arbitrary")),
        name="inproj",
    )(x, g, w)


def _hgrn_chain(q, vb, z, c0, c1, oml, code, sign_ref, states, slot, reverse):
    C = HG_CHUNK
    nv = C // SUBLANES
    e = jnp.exp(-jnp.abs(z))
    log_sig = jnp.minimum(z, 0.0) - jnp.log(1.0 + e)
    t = c1 + log_sig
    log_f = jnp.maximum(c0, t) + jnp.log(1.0 + jnp.exp(-jnp.abs(c0 - t)))
    k = oml * (jnp.where(z >= 0.0, e, 1.0) / (1.0 + e))
    sc_diag = _dot_nt(q.astype(BF16), k.astype(BF16))
    vt = vb.astype(F32).T.astype(BF16)
    yield
    x3 = log_f.reshape(nv, SUBLANES, LANES)
    sub3 = lax.broadcasted_iota(jnp.int32, (nv, SUBLANES, LANES), 1)
    d = 1
    while d < SUBLANES:
        if reverse:
            x3 = x3 + jnp.where(sub3 < SUBLANES - d, pltpu.roll(x3, SUBLANES - d, 1), 0.0)
        else:
            x3 = x3 + jnp.where(sub3 >= d, pltpu.roll(x3, d, 1), 0.0)
        d *= 2
    edge = 0 if reverse else SUBLANES - 1
    tot = jnp.broadcast_to(x3[:, edge:edge + 1, :], (nv, SUBLANES, LANES))
    groups = [None] * nv
    run = None
    for g in (range(nv - 1, -1, -1) if reverse else range(nv)):
        groups[g] = x3[g] if run is None else x3[g] + run
        run = tot[g] if run is None else run + tot[g]
    a3 = jnp.stack(groups) * LOG2_E
    a = a3.reshape(C, LANES)
    sub = lax.broadcasted_iota(jnp.int32, (C, LANES), 0) & (SUBLANES - 1)
    yield
    q_in = (q * jnp.exp2(a)).astype(BF16)
    a_last = a[0:1, :] if reverse else a[C - 1:C, :]
    k_end = (k * jnp.exp2(a_last - a)).astype(BF16)
    st = states[slot]
    o_inter = _dot_nt(q_in, st.astype(BF16))
    states[slot] = st * jnp.exp2(a_last) + _dot(vt, k_end)
    yield

    def pick(r):
        return jnp.broadcast_to(a3[:, r:r + 1, :], (nv, SUBLANES, LANES)).reshape(C, LANES)

    level_scores = []
    for b in range(HG_LEVELS):
        h = 1 << b
        m = h if reverse else h - 1
        if h == 1:
            if reverse:
                ref = jnp.where((sub & 1) == 0, pltpu.roll(a, C - 1, 0), a)
            else:
                ref = jnp.where((sub & 1) == 1, pltpu.roll(a, 1, 0), a)
        elif 2 * h < SUBLANES:
            ref = jnp.where(sub < 2 * h, pick(m), pick(m + 2 * h))
        elif 2 * h == SUBLANES:
            ref = pick(m)
        else:
            pieces = [jnp.broadcast_to(a[blk * 2 * h + m:blk * 2 * h + m + 1, :], (2 * h, LANES))
                      for blk in range(C // (2 * h))]
            ref = jnp.concatenate(pieces, axis=0) if len(pieces) > 1 else pieces[0]
        eb = jnp.exp2(((ref - a) if reverse else (a - ref)) * sign_ref[b])
        level_scores.append(_dot_nt((q * eb).astype(BF16), (k * eb).astype(BF16)))
        yield
    scores = jnp.where(code == HG_LEVELS, sc_diag, 0.0)
    for b in range(HG_LEVELS):
        scores = jnp.where(code == b, level_scores[b], scores)
    o = o_inter + _dot(scores.astype(BF16), vb)
    yield
    return o


def _hgrn_kernel(*refs, S):
    C = HG_CHUNK
    n = S // C
    nh = HG_HEADS_PER_STEP
    head_refs = [refs[5 * hh:5 * hh + 5] for hh in range(nh)]
    par_ref, o_ref, acc_ref, code_ref, sign_ref = refs[5 * nh:]

    ti = lax.broadcasted_iota(jnp.int32, (C, C), 0)
    si = lax.broadcasted_iota(jnp.int32, (C, C), 1)
    x = ti ^ si
    lvl = jnp.zeros((C, C), jnp.int32)
    for b in range(1, HG_LEVELS):
        lvl = lvl + jnp.where(x >= (1 << b), 1, 0)
    diag = jnp.where(ti == si, HG_LEVELS, -1)
    code_ref[0] = jnp.where(ti > si, lvl, diag)
    code_ref[1] = jnp.where(ti < si, lvl, diag)
    row = lax.broadcasted_iota(jnp.int32, (C, LANES), 0)
    for b in range(HG_LEVELS):
        sign_ref[b] = jnp.where((row & (1 << b)) != 0, 1.0, -1.0)

    def chain(ci, hh, d, states):
        q_ref, zf_ref, zb_ref, v_ref, _ = head_refs[hh]
        z_ref = zb_ref if d else zf_ref
        par = par_ref[hh]
        r0 = pl.multiple_of(ci * C, C)
        return _hgrn_chain(q_ref[pl.ds(r0, C), :].astype(F32), v_ref[pl.ds(r0, C), :],
                           z_ref[pl.ds(r0, C), :].astype(F32), par[3 * d:3 * d + 1], par[3 * d + 1:3 * d + 2],
                           par[3 * d + 2:3 * d + 3], code_ref[d], sign_ref, states, 2 * hh + d, d == 1)

    def make_body(accumulate):
        def body(i, carry):
            states = list(carry)
            jobs = []
            for hh in range(nh):
                jobs += [(2 * i, hh, 0), (2 * i + 1, hh, 0), (n - 1 - 2 * i, hh, 1), (n - 2 - 2 * i, hh, 1)]
            outs = _lockstep([chain(ci, hh, d, states) for ci, hh, d in jobs])
            for (ci, hh, _), o in zip(jobs, outs):
                r0 = pl.multiple_of(ci * C, C)
                if accumulate:
                    acc_ref[hh, pl.ds(r0, C), :] += o
                else:
                    acc_ref[hh, pl.ds(r0, C), :] = o
            return tuple(states)
        return body

    zero = jnp.zeros((LANES, HG_DK), F32)
    carry = lax.fori_loop(0, n // 4, make_body(False), (zero,) * (2 * nh))
    lax.fori_loop(n // 4, n // 2, make_body(True), carry)

    ft = 256

    def fin(i, _):
        r0 = pl.multiple_of(i * ft, ft)
        for hh in range(nh):
            o = acc_ref[hh, pl.ds(r0, ft), :]
            y = o * lax.rsqrt(jnp.mean(o * o, axis=-1, keepdims=True) + NORM_EPS) * par_ref[hh][6:7]
            g = head_refs[hh][4][pl.ds(r0, ft), :].astype(F32)
            o_ref[hh, pl.ds(r0, ft), :] = (y * (g * _sigmoid(g))).astype(BF16)
        return 0

    lax.fori_loop(0, S // ft, fin, 0)


def _hgrn(proj, par, B, S):
    nh = HG_HEADS_PER_STEP

    def spec(cb0, hh):
        return pl.BlockSpec((None, S, LANES), lambda b, h: (cb0 + nh * h + hh, b, 0))

    in_specs = []
    for hh in range(nh):
        in_specs += [spec(CB_HQ, hh), spec(CB_HF, hh), spec(CB_HB, hh), spec(CB_HV, hh), spec(CB_HG, hh)]
    in_specs.append(pl.BlockSpec((nh, SUBLANES, LANES), lambda b, h: (h, 0, 0)))
    return pl.pallas_call(
        functools.partial(_hgrn_kernel, S=S),
        grid=(B, HG_HEADS // nh),
        in_specs=in_specs,
        out_specs=pl.BlockSpec((nh, S, LANES), lambda b, h: (h, b, 0)),
        out_shape=jax.ShapeDtypeStruct((HG_HEADS, B * S, LANES), BF16),
        scratch_shapes=[pltpu.VMEM((nh, S, LANES), F32), pltpu.VMEM((2, HG_CHUNK, HG_CHUNK), jnp.int32),
                        pltpu.VMEM((HG_LEVELS, HG_CHUNK, LANES), F32)],
        compiler_params=_params(("parallel", "parallel")),
        name="hgrn2",
    )(*([proj] * (5 * nh)), par)


def _linear_scan_tile(a, u, carry, reverse):
    n = a.shape[0]
    nv = n // SUBLANES
    a3 = a.reshape(nv, SUBLANES, LANES)
    u3 = u.reshape(nv, SUBLANES, LANES)
    sub = lax.broadcasted_iota(jnp.int32, (nv, SUBLANES, LANES), 1)
    d = 1
    while d < SUBLANES:
        if reverse:
            keep = sub < SUBLANES - d
            shift = SUBLANES - d
        else:
            keep = sub >= d
            shift = d
        a_s = jnp.where(keep, pltpu.roll(a3, shift, 1), 1.0)
        u_s = jnp.where(keep, pltpu.roll(u3, shift, 1), 0.0)
        u3 = a3 * u_s + u3
        a3 = a3 * a_s
        d *= 2
        yield
    edge = 0 if reverse else SUBLANES - 1
    a_tot = jnp.broadcast_to(a3[:, edge:edge + 1, :], a3.shape)
    u_tot = jnp.broadcast_to(u3[:, edge:edge + 1, :], u3.shape)
    groups = [None] * nv
    for g in (range(nv - 1, -1, -1) if reverse else range(nv)):
        groups[g] = u3[g] + a3[g] * carry
        carry = u_tot[g] + a_tot[g] * carry
        if g % 4 == 0:
            yield
    return jnp.stack(groups).reshape(n, LANES), carry


def _rg_kernel(x_ref, gate_ref, par_ref, w_ref, o_ref, xpad_ref, xc_ref, h_ref, *, S):
    TS = RG_TILE
    n = S // TS
    par = par_ref[...]
    conv_b = par[4:5]

    xpad_ref[pl.ds(0, RG_HALO), :] = jnp.zeros((RG_HALO, LANES), F32)
    xpad_ref[pl.ds(S + RG_HALO, RG_HALO), :] = jnp.zeros((RG_HALO, LANES), F32)

    def copy(i, _):
        r0 = pl.multiple_of(i * TS, TS)
        xpad_ref[pl.ds(r0 + RG_HALO, TS), :] = x_ref[pl.ds(r0, TS), :].astype(F32)
        return 0

    lax.fori_loop(0, n, copy, 0)

    def conv(i, _):
        r0 = pl.multiple_of(i * TS, TS)
        xc = conv_b
        for j in range(RG_CONV_W):
            off = RG_HALO - RG_CONV_W // 2 + j
            xc = xc + par[j:j + 1] * xpad_ref[pl.ds(r0 + off, TS), :]
        xc_ref[pl.ds(r0, TS), :] = xc
        return 0

    lax.fori_loop(0, n, conv, 0)

    def tile(i, carries, d, final):
        r0 = pl.multiple_of(i * TS, TS)
        xc = xc_ref[pl.ds(r0, TS), :]
        gts = _dot(xc.astype(BF16), w_ref[:, d * 2 * LANES:(d + 1) * 2 * LANES])
        yield
        r = _sigmoid(gts[:, :LANES] + par[5 + 2 * d:6 + 2 * d])
        ig = _sigmoid(gts[:, LANES:] + par[6 + 2 * d:7 + 2 * d])
        lam = par[9 + d:10 + d]
        softplus_neg_lam = jnp.maximum(-lam, 0.0) + jnp.log1p(jnp.exp(-jnp.abs(lam)))
        log_a = (-RG_C) * r * softplus_neg_lam
        a = jnp.exp(log_a)
        y = 1.0 - a * a
        u = jnp.where(y > 0.0, y * lax.rsqrt(y), 0.0) * (ig * xc)
        yield
        h, carries[d] = yield from _linear_scan_tile(a, u, carries[d], reverse=(d == 1))
        yield
        if final:
            gate = gate_ref[pl.ds(r0, TS), :].astype(F32)
            o_ref[pl.ds(r0, TS), :] = (jax.nn.gelu(gate) * (h_ref[pl.ds(r0, TS), :] + h)).astype(BF16)
        else:
            h_ref[pl.ds(r0, TS), :] = h

    def make_body(final):
        def body(i, carry):
            carries = list(carry)
            _lockstep([tile(i, carries, 0, final), tile(n - 1 - i, carries, 1, final)])
            return tuple(carries)
        return body

    zero = jnp.zeros((SUBLANES, LANES), F32)
    carry = lax.fori_loop(0, n // 2, make_body(False), (zero, zero))
    lax.fori_loop(n // 2, n, make_body(True), carry)


def _rglru(proj, par, w4, B, S):
    ng = RG_WIDTH // LANES

    def spec(cb0):
        return pl.BlockSpec((None, S, LANES), lambda b, g: (cb0 + g, b, 0))

    return pl.pallas_call(
        functools.partial(_rg_kernel, S=S),
        grid=(B, ng),
        in_specs=[spec(CB_RX), spec(CB_RGATE),
                  pl.BlockSpec((None, 2 * SUBLANES, LANES), lambda b, g: (g, 0, 0)),
                  pl.BlockSpec((None, LANES, 4 * LANES), lambda b, g: (g, 0, 0))],
        out_specs=pl.BlockSpec((None, S, LANES), lambda b, g: (g, b, 0)),
        out_shape=jax.ShapeDtypeStruct((ng, B * S, LANES), BF16),
        scratch_shapes=[pltpu.VMEM((S + 2 * RG_HALO, LANES), F32), pltpu.VMEM((S, LANES), F32),
                        pltpu.VMEM((S, LANES), F32)],
        compiler_params=_params(("parallel", "parallel")),
        name="rglru",
    )(proj, proj, par, w4)


def _attn_kernel(q0_ref, k0_ref, v0_ref, q1_ref, k1_ref, v1_ref, q2_ref, k2_ref, v2_ref, par_ref, o_ref,
                 perm_ref, qd_ref, kd_ref, vd_ref, bias_ref, og_ref, lg_ref, *, S):
    refs = ((q0_ref, k0_ref, v0_ref), (q1_ref, k1_ref, v1_ref), (q2_ref, k2_ref, v2_ref))
    ct = ATT_TILE

    for g, (_, dil) in enumerate(ATT_GROUPS):
        q_ref, k_ref, v_ref = refs[g]
        par = par_ref[g]
        qg, kg, slope = par[0:1], par[1:2], par[2:3, 0:1]
        L = S // dil
        Q = min(ATT_QBLK, L)
        KW = min(Q + 2 * ATT_RADIUS, L)
        nq = L // Q
        nq_shift = int(math.log2(nq))

        for var in range(3):
            qi = var * ATT_RADIUS + lax.broadcasted_iota(jnp.int32, (Q, KW), 0)
            rel = jnp.abs(qi - lax.broadcasted_iota(jnp.int32, (Q, KW), 1))
            bias_ref[var, :Q, :KW] = jnp.where(rel <= ATT_RADIUS, -slope * rel.astype(F32), NEG_INF)

        per = ct // dil
        if dil > 1:
            pi = lax.broadcasted_iota(jnp.int32, (ct, ct), 0)
            pj = lax.broadcasted_iota(jnp.int32, (ct, ct), 1)
            src_tok = (pi & (per - 1)) * dil + lax.shift_right_logical(pi, int(math.log2(per)))
            perm_ref[...] = jnp.where(pj == src_tok, 1.0, 0.0).astype(BF16)

        def prep_tile(i, q_ref=q_ref, k_ref=k_ref, v_ref=v_ref, qg=qg, kg=kg, dil=dil, per=per, L=L):
            r0 = pl.multiple_of(i * ct, ct)
            xs = (q_ref[pl.ds(r0, ct), :].astype(F32), k_ref[pl.ds(r0, ct), :].astype(F32))
            sums = [jnp.sum(x * x, axis=-1, keepdims=True) for x in xs]
            yield
            tiles = [(x * lax.rsqrt(ss * (1.0 / LANES) + NORM_EPS) * gain).astype(BF16)
                     for x, ss, gain in zip(xs, sums, (qg, kg))]
            tiles.append(v_ref[pl.ds(r0, ct), :])
            if dil > 1:
                y = _dot(perm_ref[...], jnp.concatenate(tiles, axis=1))
                tiles = [y[:, c * LANES:(c + 1) * LANES] for c in range(3)]
                yield
            for dst_ref, y in zip((qd_ref, kd_ref, vd_ref), tiles):
                if dil == 1:
                    dst_ref[pl.ds(r0, ct), :] = y
                else:
                    y = y.astype(BF16)
                    for r in range(dil):
                        dst_ref[pl.ds(pl.multiple_of(r * L + i * per, 16), per), :] = y[r * per:(r + 1) * per, :]

        prep_unroll = min(ATT_PREP_UNROLL, S // ct)

        def prep(i, _, prep_tile=prep_tile, prep_unroll=prep_unroll):
            _lockstep([prep_tile(i * prep_unroll + u) for u in range(prep_unroll)])
            return 0

        lax.fori_loop(0, S // ct // prep_unroll, prep, 0)

        def rows(start, size, dil=dil):
            if dil == 1:
                return pl.ds(start, size)
            return pl.ds(start, size, stride=dil)

        unroll = min(ATT_UNROLL, dil * nq)

        def qblock(it, g=g, dil=dil, L=L, Q=Q, KW=KW, nq=nq, nq_shift=nq_shift, rows=rows):
            r = lax.shift_right_logical(it, nq_shift)
            m0 = (it & (nq - 1)) * Q
            ks = jnp.clip(m0 - ATT_RADIUS, 0, L - KW)
            var = lax.shift_right_logical(m0 - ks, int(math.log2(ATT_RADIUS)))
            base = r * L
            qn = qd_ref[pl.ds(pl.multiple_of(base + m0, 16), Q), :]
            kk = kd_ref[pl.ds(pl.multiple_of(base + ks, 16), KW), :]
            s = _dot_nt(qn, kk)
            yield
            s = s + bias_ref[var, :Q, :KW]
            m = jnp.max(s, axis=-1, keepdims=True)
            yield
            pb = jnp.exp(s - m).astype(BF16)
            vv = vd_ref[pl.ds(pl.multiple_of(base + ks, 16), KW), :]
            ol = _dot(pb, jnp.concatenate([vv, jnp.ones((KW, LANES), BF16)], axis=1))
            yield
            l = ol[:, LANES:]
            og_ref[g, rows(m0 * dil + r, Q), :] = ol[:, :LANES] / l
            lg_ref[g, rows(m0 * dil + r, Q), :] = m + jnp.log(l)

        def qblocks(i, _, unroll=unroll, qblock=qblock):
            _lockstep([qblock(i * unroll + u) for u in range(unroll)])
            return 0

        lax.fori_loop(0, (dil * nq) // unroll, qblocks, 0)

    def merge(i, _):
        r0 = pl.multiple_of(i * ct, ct)
        l0, l1, l2 = (lg_ref[g, pl.ds(r0, ct), :] for g in range(3))
        m = jnp.maximum(jnp.maximum(l0, l1), l2)
        e0, e1, e2 = jnp.exp(l0 - m), jnp.exp(l1 - m), jnp.exp(l2 - m)
        o = (e0 * og_ref[0, pl.ds(r0, ct), :] + e1 * og_ref[1, pl.ds(r0, ct), :]
             + e2 * og_ref[2, pl.ds(r0, ct), :]) / (e0 + e1 + e2)
        o_ref[pl.ds(r0, ct), :] = o.astype(BF16)
        return 0

    lax.fori_loop(0, S // ct, merge, 0)


def _attention(proj, par, B, S):
    H = ATT_HEADS_PER_GROUP
    ngroups = len(ATT_GROUPS)

    def spec(cb0, g):
        return pl.BlockSpec((None, S, LANES), lambda b, j: (cb0 + g * H + j, b, 0))

    in_specs = []
    for g in range(ngroups):
        in_specs += [spec(CB_AQ, g), spec(CB_AK, g), spec(CB_AV, g)]
    in_specs.append(pl.BlockSpec((None, ngroups, SUBLANES, LANES), lambda b, j: (j, 0, 0, 0)))
    return pl.pallas_call(
        functools.partial(_attn_kernel, S=S),
        grid=(B, H),
        in_specs=in_specs,
        out_specs=pl.BlockSpec((None, S, LANES), lambda b, j: (j, b, 0)),
        out_shape=jax.ShapeDtypeStruct((H, B * S, LANES), BF16),
        scratch_shapes=[pltpu.VMEM((ATT_TILE, ATT_TILE), BF16),
                        pltpu.VMEM((S, LANES), BF16), pltpu.VMEM((S, LANES), BF16), pltpu.VMEM((S, LANES), BF16),
                        pltpu.VMEM((3, ATT_QBLK, ATT_QBLK + 2 * ATT_RADIUS), F32),
                        pltpu.VMEM((ngroups, S, LANES), F32), pltpu.VMEM((ngroups, S, LANES), F32)],
        compiler_params=_params(("parallel", "parallel")),
        name="attention",
    )(*([proj] * (3 * ngroups)), par)


def _cat_heads(ref):
    return jnp.concatenate([ref[c] for c in range(ref.shape[0])], axis=-1)


def _merge_kernel(x_ref, hg_ref, rg_ref, att_ref, ga_ref, gb_ref, gc_ref, wb_ref, wo_ref, ng_ref,
                  x1_ref, h2_ref):
    merged = _sigmoid(_cat_heads(ga_ref).astype(F32)) * _dot(_cat_heads(hg_ref), wb_ref[0])
    merged += _sigmoid(_cat_heads(gb_ref).astype(F32)) * _dot(_cat_heads(rg_ref), wb_ref[1])
    merged += _sigmoid(_cat_heads(gc_ref).astype(F32)) * _dot(_cat_heads(att_ref), wb_ref[2])
    x1 = x_ref[...] + _dot(merged.astype(BF16), wo_ref[...])
    x1_ref[...] = x1
    ms = jnp.mean(x1 * x1, axis=-1, keepdims=True)
    h2_ref[...] = (x1 * lax.rsqrt(ms + NORM_EPS) * ng_ref[...]).astype(BF16)


def _merge(x, proj, hg, rg, att, wb, wo, ng):
    T = x.shape[0]
    tm = MERGE_TM
    nb = BRANCH_W // LANES
    ngate = D_MODEL // LANES

    branch = pl.BlockSpec((nb, tm, LANES), lambda i: (0, i, 0))

    def gate(n):
        return pl.BlockSpec((ngate, tm, LANES), lambda i: (CB_GATES // ngate + n, i, 0))

    row = pl.BlockSpec((tm, D_MODEL), lambda i: (i, 0))
    return pl.pallas_call(
        _merge_kernel,
        grid=(T // tm,),
        in_specs=[row, branch, branch, branch, gate(0), gate(1), gate(2),
                  pl.BlockSpec((3, BRANCH_W, D_MODEL), lambda i: (0, 0, 0)),
                  pl.BlockSpec((D_MODEL, D_MODEL), lambda i: (0, 0)),
                  pl.BlockSpec((1, D_MODEL), lambda i: (0, 0))],
        out_specs=[row, row],
        out_shape=[jax.ShapeDtypeStruct((T, D_MODEL), F32), jax.ShapeDtypeStruct((T, D_MODEL), BF16)],
        compiler_params=_params(("parallel",)),
        name="merge",
    )(x, hg, rg, att, proj, proj, proj, wb, wo, ng)


def _ffn_kernel(h_ref, x_ref, w1_ref, w3_ref, w2_ref, o_ref, acc_ref):
    f = pl.program_id(1)
    h = h_ref[...]
    h1 = _dot(h, w1_ref[...])
    y = _dot((h1 * _sigmoid(h1) * _dot(h, w3_ref[...])).astype(BF16), w2_ref[...])

    @pl.when(f == 0)
    def _():
        acc_ref[...] = x_ref[...] + y

    @pl.when(f > 0)
    def _():
        acc_ref[...] += y

    @pl.when(f == pl.num_programs(1) - 1)
    def _():
        o_ref[...] = acc_ref[...]


def _ffn_dense(h2, x1, w1, w3, w2):
    T = h2.shape[0]
    tm, tf = FFN_TM, FFN_TF
    return pl.pallas_call(
        _ffn_kernel,
        grid=(T // tm, FFN_DENSE // tf),
        in_specs=[pl.BlockSpec((tm, D_MODEL), lambda i, f: (i, 0)),
                  pl.BlockSpec((tm, D_MODEL), lambda i, f: (i, 0)),
                  pl.BlockSpec((D_MODEL, tf), lambda i, f: (0, f), pipeline_mode=pl.Buffered(1)),
                  pl.BlockSpec((D_MODEL, tf), lambda i, f: (0, f), pipeline_mode=pl.Buffered(1)),
                  pl.BlockSpec((tf, D_MODEL), lambda i, f: (f, 0), pipeline_mode=pl.Buffered(1))],
        out_specs=pl.BlockSpec((tm, D_MODEL), lambda i, f: (i, 0)),
        out_shape=jax.ShapeDtypeStruct((T, D_MODEL), F32),
        scratch_shapes=[pltpu.VMEM((tm, D_MODEL), F32)],
        compiler_params=_params(("parallel", "arbitrary")),
        name="ffn_dense",
    )(h2, x1, w1, w3, w2)


def _router_kernel(h_ref, wr_ref, o_ref, cnt_ref, base_ref):
    tm = ROUTER_TM

    @pl.when(pl.program_id(0) == 0)
    def _():
        base_ref[...] = jnp.zeros_like(base_ref)

    lane = lax.broadcasted_iota(jnp.int32, (tm, LANES), 1)
    logits = jnp.where(lane < N_EXPERTS, _dot(h_ref[...], wr_ref[...]), -jnp.inf)
    m1 = jnp.max(logits, axis=-1, keepdims=True)
    i1 = jnp.min(jnp.where(logits == m1, lane, LANES), axis=-1, keepdims=True)
    rest = jnp.where(lane == i1, -jnp.inf, logits)
    m2 = jnp.max(rest, axis=-1, keepdims=True)
    i2 = jnp.min(jnp.where(rest == m2, lane, LANES), axis=-1, keepdims=True)
    e21 = jnp.exp(m2 - m1)
    g1 = 1.0 / (1.0 + e21)
    g2 = e21 / (1.0 + e21)

    onehot = jnp.where(lane == i1, 1.0, jnp.where(lane == i2, 1.0, 0.0))
    ti = lax.broadcasted_iota(jnp.int32, (tm, tm), 0)
    si = lax.broadcasted_iota(jnp.int32, (tm, tm), 1)
    before = jnp.where(si < ti, 1.0, 0.0).astype(BF16)
    pos = _dot(before, onehot.astype(BF16)) + base_ref[0:1, :]
    r1 = jnp.sum(jnp.where(lane == i1, pos, 0.0), axis=-1, keepdims=True)
    r2 = jnp.sum(jnp.where(lane == i2, pos, 0.0), axis=-1, keepdims=True)
    total = base_ref[0:1, :] + jnp.sum(onehot, axis=0, keepdims=True)
    base_ref[...] = jnp.broadcast_to(total, base_ref.shape)
    cnt_ref[...] = jnp.broadcast_to(total, cnt_ref.shape)

    out = jnp.where(lane == 0, i1.astype(F32), 0.0)
    out = jnp.where(lane == 1, i2.astype(F32), out)
    out = jnp.where(lane == 2, g1, out)
    out = jnp.where(lane == 3, g2, out)
    out = jnp.where(lane == 4, r1, out)
    out = jnp.where(lane == 5, r2, out)
    o_ref[...] = out


def _router(h2, wr):
    T = h2.shape[0]
    tm = ROUTER_TM
    return pl.pallas_call(
        _router_kernel,
        grid=(T // tm,),
        in_specs=[pl.BlockSpec((tm, D_MODEL), lambda i: (i, 0)),
                  pl.BlockSpec((D_MODEL, LANES), lambda i: (0, 0))],
        out_specs=[pl.BlockSpec((tm, LANES), lambda i: (i, 0)),
                   pl.BlockSpec((SUBLANES, LANES), lambda i: (0, 0))],
        out_shape=[jax.ShapeDtypeStruct((T, LANES), F32), jax.ShapeDtypeStruct((SUBLANES, LANES), F32)],
        scratch_shapes=[pltpu.VMEM((SUBLANES, LANES), F32)],
        compiler_params=_params(("arbitrary",)),
        name="moe_router",
    )(h2, wr)


def _expert_kernel(be_ref, nu_ref, x_ref, w1_ref, w3_ref, w2_ref, o_ref, acc_ref):
    i = pl.program_id(0)
    f = pl.program_id(1)

    @pl.when(f == 0)
    def _():
        acc_ref[...] = jnp.zeros_like(acc_ref)

    @pl.when(i < nu_ref[0])
    def _():
        x = x_ref[...]
        h1 = _dot(x, w1_ref[...])
        acc_ref[...] += _dot((h1 * _sigmoid(h1) * _dot(x, w3_ref[...])).astype(BF16), w2_ref[...])

    @pl.when(f == pl.num_programs(1) - 1)
    def _():
        o_ref[...] = acc_ref[...].astype(BF16)


def _experts(rows, block_expert, n_used, w1, w3, w2):
    n_rows = rows.shape[0]
    R, tf = MOE_ROWS, MOE_TF
    grid_spec = pltpu.PrefetchScalarGridSpec(
        num_scalar_prefetch=2,
        grid=(n_rows // R, FFN_EXPERT // tf),
        in_specs=[pl.BlockSpec((R, D_MODEL), lambda i, f, be, nu: (i, 0)),
                  pl.BlockSpec((None, D_MODEL, tf), lambda i, f, be, nu: (be[i], 0, f)),
                  pl.BlockSpec((None, D_MODEL, tf), lambda i, f, be, nu: (be[i], 0, f)),
                  pl.BlockSpec((None, tf, D_MODEL), lambda i, f, be, nu: (be[i], f, 0))],
        out_specs=pl.BlockSpec((R, D_MODEL), lambda i, f, be, nu: (i, 0)),
        scratch_shapes=[pltpu.VMEM((R, D_MODEL), F32)],
    )
    return pl.pallas_call(
        _expert_kernel,
        grid_spec=grid_spec,
        out_shape=jax.ShapeDtypeStruct((n_rows, D_MODEL), BF16),
        compiler_params=_params(("parallel", "arbitrary")),
        name="moe_experts",
    )(block_expert, n_used, rows, w1, w3, w2)


def _combine_kernel(x_ref, y1_ref, y2_ref, info_ref, o_ref):
    info = info_ref[...]
    o_ref[...] = (x_ref[...] + info[:, 2:3] * y1_ref[...].astype(F32)
                  + info[:, 3:4] * y2_ref[...].astype(F32))


def _combine(x1, y1, y2, info):
    T = x1.shape[0]
    tm = COMBINE_TM
    row = pl.BlockSpec((tm, D_MODEL), lambda i: (i, 0))
    return pl.pallas_call(
        _combine_kernel,
        grid=(T // tm,),
        in_specs=[row, row, row, pl.BlockSpec((tm, LANES), lambda i: (i, 0))],
        out_specs=row,
        out_shape=jax.ShapeDtypeStruct((T, D_MODEL), F32),
        compiler_params=_params(("parallel",)),
        name="moe_combine",
    )(x1, y1, y2, info)


def _moe(h2, x1, wr, w1, w3, w2):
    T = h2.shape[0]
    R = MOE_ROWS
    n_blocks = -(-(T * TOP_K) // R) + N_EXPERTS
    info, cnt = _router(h2, wr)
    expert = info[:, 0:2].astype(jnp.int32)
    rank = info[:, 4:6].astype(jnp.int32)
    counts = cnt[0, :N_EXPERTS].astype(jnp.int32)
    blocks_per = (counts + R - 1) // R
    bend = jnp.cumsum(blocks_per)
    pstart = (bend - blocks_per) * R
    lanes = jnp.arange(N_EXPERTS, dtype=jnp.int32)
    dest = jnp.sum(jnp.where(expert[..., None] == lanes, pstart, 0), axis=-1) + rank
    n_used = bend[-1:]
    block_expert = jnp.minimum(
        jnp.searchsorted(bend, jnp.arange(n_blocks, dtype=jnp.int32), side='right'),
        N_EXPERTS - 1).astype(jnp.int32)
    n_pad = n_blocks * R - T * TOP_K
    pad_end = jnp.cumsum(blocks_per * R - counts)
    pad_expert = jnp.sum(jnp.arange(n_pad, dtype=jnp.int32)[:, None] >= pad_end[None, :], axis=1).astype(jnp.int32)
    keys = jnp.concatenate([expert.reshape(-1) * 2, pad_expert * 2 + 1])
    token = jnp.broadcast_to(jnp.arange(T, dtype=jnp.int32)[:, None], (T, TOP_K)).reshape(-1)
    vals = jnp.concatenate([token, jnp.zeros((n_pad,), jnp.int32)])
    _, tok_of_row = lax.sort((keys, vals), num_keys=1, is_stable=True)
    rows = h2.at[tok_of_row].get(mode='promise_in_bounds')
    y = _experts(rows, block_expert, n_used, w1, w3, w2)
    y1 = y.at[dest[:, 0]].get(mode='promise_in_bounds')
    y2 = y.at[dest[:, 1]].get(mode='promise_in_bounds')
    return _combine(x1, y1, y2, info)


def _pad_rows(rows, n):
    rows = jnp.stack(rows, axis=-2)
    pad = [(0, 0)] * (rows.ndim - 2) + [(0, n - rows.shape[-2]), (0, 0)]
    return jnp.pad(rows, pad)


def _hgrn_params(lb, norm_g):
    lb = lb.reshape(2, HG_HEADS, HG_DK)
    rows = []
    for d in range(2):
        rows += [jnp.log(jnp.maximum(lb[d], LB_EPS)), jnp.log1p(-lb[d]), 1.0 - lb[d]]
    rows.append(norm_g.reshape(HG_HEADS, HG_DK))
    return _pad_rows(rows, SUBLANES)


def _rg_params(conv_w, conv_b, b_a, b_x, lam):
    ng = RG_WIDTH // LANES
    g = lambda t: t.reshape(ng, LANES)
    rows = [g(conv_w[j]) for j in range(RG_CONV_W)] + [g(conv_b)]
    rows += [g(b_a[0]), g(b_x[0]), g(b_a[1]), g(b_x[1]), g(lam[0]), g(lam[1])]
    return _pad_rows(rows, 2 * SUBLANES)


def _rg_gate_weights(w_a, w_x):
    ng = RG_WIDTH // LANES
    per = RG_BLOCKS // ng

    def dense(w):
        w = w.reshape(ng, per, RG_BLOCK_W, RG_BLOCK_W)
        eye = jnp.eye(per, dtype=w.dtype)
        return jnp.einsum('gpcd,pq->gpcqd', w, eye).reshape(ng, LANES, LANES)

    return jnp.concatenate([dense(w_a[0]), dense(w_x[0]), dense(w_a[1]), dense(w_x[1])], axis=-1).astype(BF16)


def _attn_params(q_g, k_g):
    H = ATT_HEADS_PER_GROUP
    groups = []
    for gi, (_, dil) in enumerate(ATT_GROUPS):
        heads = jnp.arange(gi * H + 1, (gi + 1) * H + 1, dtype=F32)
        slopes = 2.0 ** (-8.0 * heads / ATT_HEADS) * dil
        rows = [jnp.broadcast_to(q_g * (ATT_DH ** -0.5), (H, LANES)), jnp.broadcast_to(k_g, (H, LANES)),
                jnp.broadcast_to(slopes[:, None], (H, LANES))]
        groups.append(_pad_rows(rows, SUBLANES))
    return jnp.stack(groups, axis=1)


def _trunk(x3, p):
    B, S, _ = x3.shape
    x = x3.reshape(B * S, D_MODEL)
    sm = jax.nn.softmax(p['hg_lb_logits'].astype(F32), axis=1)
    lower_bounds = jnp.cumsum(sm, axis=1) - sm[:, :1]
    for l in range(DEPTH):
        proj = _inproj(x, p['norm_mix_g'][l][None], p['w_in'][l].astype(BF16))
        hg = _hgrn(proj, _hgrn_params(lower_bounds[:, l], p['hg_norm_g'][l]), B, S)
        rg = _rglru(proj,
                    _rg_params(p['rg_conv_w'][l], p['rg_conv_b'][l], p['rg_b_a'][l], p['rg_b_x'][l],
                               p['rg_lambda'][l]),
                    _rg_gate_weights(p['rg_w_a'][l], p['rg_w_x'][l]), B, S)
        att = _attention(proj, _attn_params(p['attn_q_g'][l], p['attn_k_g'][l]), B, S)
        x1, h2 = _merge(x, proj, hg, rg, att, p['w_branch'][l].astype(BF16), p['w_out'][l].astype(BF16),
                        p['norm_ffn_g'][l][None])
        j = l // 2
        if l % 2 == 0:
            x = _ffn_dense(h2, x1, p['ffn_w1'][j].astype(BF16), p['ffn_w3'][j].astype(BF16),
                           p['ffn_w2'][j].astype(BF16))
        else:
            wr = jnp.pad(p['moe_router'][j], ((0, 0), (0, LANES - N_EXPERTS))).astype(BF16)
            x = _moe(h2, x1, wr, p['moe_w1'][j].astype(BF16), p['moe_w3'][j].astype(BF16),
                     p['moe_w2'][j].astype(BF16))
    return x.reshape(B, S, D_MODEL)


def kernel(x_prompt, x_sample, norm_mix_g, w_in, hg_lb_logits, hg_norm_g, rg_conv_w, rg_conv_b, rg_w_a, rg_b_a,
           rg_w_x, rg_b_x, rg_lambda, attn_q_g, attn_k_g, w_branch, w_out, norm_ffn_g, ffn_w1, ffn_w3, ffn_w2,
           moe_router, moe_w1, moe_w3, moe_w2):
    p = dict(norm_mix_g=norm_mix_g, w_in=w_in, hg_lb_logits=hg_lb_logits, hg_norm_g=hg_norm_g,
             rg_conv_w=rg_conv_w, rg_conv_b=rg_conv_b, rg_w_a=rg_w_a, rg_b_a=rg_b_a, rg_w_x=rg_w_x,
             rg_b_x=rg_b_x, rg_lambda=rg_lambda, attn_q_g=attn_q_g, attn_k_g=attn_k_g,
             w_branch=w_branch, w_out=w_out, norm_ffn_g=norm_ffn_g, ffn_w1=ffn_w1, ffn_w3=ffn_w3,
             ffn_w2=ffn_w2, moe_router=moe_router, moe_w1=moe_w1, moe_w3=moe_w3, moe_w2=moe_w2)
    return (_trunk(x_prompt, p), _trunk(x_sample, p))
```

```python
import functools
import math

import jax
import jax.numpy as jnp
from jax import lax
from jax.experimental import pallas as pl
from jax.experimental.pallas import tpu as pltpu

F32 = jnp.float32
BF16 = jnp.bfloat16

LANES = 128
SUBLANES = 8
VMEM_LIMIT_BYTES = 56 * 1024 * 1024

D_MODEL = 1024
DEPTH = 2
BRANCH_W = 512
HG_HEADS = 4
HG_DK = 128
HG_CHUNK = 64
HG_HEADS_PER_STEP = 2
HG_LEVELS = 6
LB_EPS = 1e-20
RG_WIDTH = 512
RG_BLOCKS = 8
RG_BLOCK_W = RG_WIDTH // RG_BLOCKS
RG_CONV_W = 4
RG_C = 8.0
RG_TILE = 1024
RG_HALO = 8
ATT_GROUPS = ((128, 1), (512, 4), (2048, 16))
ATT_HEADS_PER_GROUP = 4
ATT_HEADS = ATT_HEADS_PER_GROUP * len(ATT_GROUPS)
ATT_DH = 128
ATT_RADIUS = 64
ATT_QBLK = 128
ATT_UNROLL = 32
ATT_PREP_UNROLL = 8
ATT_TILE = 256
FFN_DENSE = 2816
N_EXPERTS = 8
TOP_K = 2
FFN_EXPERT = 3584
NORM_EPS = 1e-6
NEG_INF = -1e30
LOG2_E = 1.4426950408889634

CB_HQ, CB_HF, CB_HB, CB_HV, CB_HG = 0, 4, 8, 12, 16
CB_RX, CB_RGATE = 20, 24
CB_AQ, CB_AK, CB_AV = 28, 40, 52
CB_GATES = 64
IN_COLS = 11264
N_CB = IN_COLS // LANES

INPROJ_TM, INPROJ_TN = 2048, 1024
MERGE_TM = 512
FFN_TM, FFN_TF = 512, 2816
MOE_ROWS, MOE_TF = 512, 1792
ROUTER_TM = 512
COMBINE_TM = 512


def _params(sem):
    return pltpu.CompilerParams(dimension_semantics=sem, vmem_limit_bytes=VMEM_LIMIT_BYTES)


def _dot(a, b):
    return jnp.dot(a, b, preferred_element_type=F32)


def _dot_nt(a, b):
    return lax.dot_general(a, b, (((1,), (1,)), ((), ())), preferred_element_type=F32)


def _sigmoid(x):
    return jax.nn.sigmoid(x)


def _lockstep(gens):
    results = [None] * len(gens)
    active = list(range(len(gens)))
    while active:
        for idx in list(active):
            try:
                next(gens[idx])
            except StopIteration as stop:
                results[idx] = stop.value
                active.remove(idx)
    return results


def _inproj_kernel(x_ref, g_ref, w_ref, o_ref, h_ref):
    @pl.when(pl.program_id(1) == 0)
    def _():
        x = x_ref[...]
        ms = jnp.mean(x * x, axis=-1, keepdims=True)
        h_ref[...] = (x * lax.rsqrt(ms + NORM_EPS) * g_ref[...]).astype(BF16)

    acc = _dot(h_ref[...], w_ref[...])
    for c in range(INPROJ_TN // LANES):
        o_ref[c] = acc[:, c * LANES:(c + 1) * LANES].astype(BF16)


def _inproj(x, g, w):
    T = x.shape[0]
    return pl.pallas_call(
        _inproj_kernel,
        grid=(T // INPROJ_TM, IN_COLS // INPROJ_TN),
        in_specs=[
            pl.BlockSpec((INPROJ_TM, D_MODEL), lambda i, j: (i, 0)),
            pl.BlockSpec((1, D_MODEL), lambda i, j: (0, 0)),
            pl.BlockSpec((D_MODEL, INPROJ_TN), lambda i, j: (0, j)),
        ],
        out_specs=pl.BlockSpec((INPROJ_TN // LANES, INPROJ_TM, LANES), lambda i, j: (j, i, 0)),
        out_shape=jax.ShapeDtypeStruct((N_CB, T, LANES), BF16),
        scratch_shapes=[pltpu.VMEM((INPROJ_TM, D_MODEL), BF16)],
        compiler_params=_params(("parallel", "arbitrary")),
        name="inproj",
    )(x, g, w)


def _hgrn_chain(q, vb, z, c0, c1, oml, code, sign_ref, states, slot, reverse):
    C = HG_CHUNK
    nv = C // SUBLANES
    e = jnp.exp(-jnp.abs(z))
    log_sig = jnp.minimum(z, 0.0) - jnp.log(1.0 + e)
    t = c1 + log_sig
    log_f = jnp.maximum(c0, t) + jnp.log(1.0 + jnp.exp(-jnp.abs(c0 - t)))
    k = oml * (jnp.where(z >= 0.0, e, 1.0) / (1.0 + e))
    sc_diag = _dot_nt(q.astype(BF16), k.astype(BF16))
    vt = vb.astype(F32).T.astype(BF16)
    yield
    x3 = log_f.reshape(nv, SUBLANES, LANES)
    sub3 = lax.broadcasted_iota(jnp.int32, (nv, SUBLANES, LANES), 1)
    d = 1
    while d < SUBLANES:
        if reverse:
            x3 = x3 + jnp.where(sub3 < SUBLANES - d, pltpu.roll(x3, SUBLANES - d, 1), 0.0)
        else:
            x3 = x3 + jnp.where(sub3 >= d, pltpu.roll(x3, d, 1), 0.0)
        d *= 2
    edge = 0 if reverse else SUBLANES - 1
    tot = jnp.broadcast_to(x3[:, edge:edge + 1, :], (nv, SUBLANES, LANES))
    groups = [None] * nv
    run = None
    for g in (range(nv - 1, -1, -1) if reverse else range(nv)):
        groups[g] = x3[g] if run is None else x3[g] + run
        run = tot[g] if run is None else run + tot[g]
    a3 = jnp.stack(groups) * LOG2_E
    a = a3.reshape(C, LANES)
    sub = lax.broadcasted_iota(jnp.int32, (C, LANES), 0) & (SUBLANES - 1)
    yield
    q_in = (q * jnp.exp2(a)).astype(BF16)
    a_last = a[0:1, :] if reverse else a[C - 1:C, :]
    k_end = (k * jnp.exp2(a_last - a)).astype(BF16)
    st = states[slot]
    o_inter = _dot_nt(q_in, st.astype(BF16))
    states[slot] = st * jnp.exp2(a_last) + _dot(vt, k_end)
    yield

    def pick(r):
        return jnp.broadcast_to(a3[:, r:r + 1, :], (nv, SUBLANES, LANES)).reshape(C, LANES)

    level_scores = []
    for b in range(HG_LEVELS):
        h = 1 << b
        m = h if reverse else h - 1
        if h == 1:
            if reverse:
                ref = jnp.where((sub & 1) == 0, pltpu.roll(a, C - 1, 0), a)
            else:
                ref = jnp.where((sub & 1) == 1, pltpu.roll(a, 1, 0), a)
        elif 2 * h < SUBLANES:
            ref = jnp.where(sub < 2 * h, pick(m), pick(m + 2 * h))
        elif 2 * h == SUBLANES:
            ref = pick(m)
        else:
            pieces = [jnp.broadcast_to(a[blk * 2 * h + m:blk * 2 * h + m + 1, :], (2 * h, LANES))
                      for blk in range(C // (2 * h))]
            ref = jnp.concatenate(pieces, axis=0) if len(pieces) > 1 else pieces[0]
        eb = jnp.exp2(((ref - a) if reverse else (a - ref)) * sign_ref[b])
        level_scores.append(_dot_nt((q * eb).astype(BF16), (k * eb).astype(BF16)))
        yield
    scores = jnp.where(code == HG_LEVELS, sc_diag, 0.0)
    for b in range(HG_LEVELS):
        scores = jnp.where(code == b, level_scores[b], scores)
    o = o_inter + _dot(scores.astype(BF16), vb)
    yield
    return o


def _hgrn_kernel(*refs, S):
    C = HG_CHUNK
    n = S // C
    nh = HG_HEADS_PER_STEP
    head_refs = [refs[5 * hh:5 * hh + 5] for hh in range(nh)]
    par_ref, o_ref, acc_ref, code_ref, sign_ref = refs[5 * nh:]

    ti = lax.broadcasted_iota(jnp.int32, (C, C), 0)
    si = lax.broadcasted_iota(jnp.int32, (C, C), 1)
    x = ti ^ si
    lvl = jnp.zeros((C, C), jnp.int32)
    for b in range(1, HG_LEVELS):
        lvl = lvl + jnp.where(x >= (1 << b), 1, 0)
    diag = jnp.where(ti == si, HG_LEVELS, -1)
    code_ref[0] = jnp.where(ti > si, lvl, diag)
    code_ref[1] = jnp.where(ti < si, lvl, diag)
    row = lax.broadcasted_iota(jnp.int32, (C, LANES), 0)
    for b in range(HG_LEVELS):
        sign_ref[b] = jnp.where((row & (1 << b)) != 0, 1.0, -1.0)

    def chain(ci, hh, d, states):
        q_ref, zf_ref, zb_ref, v_ref, _ = head_refs[hh]
        z_ref = zb_ref if d else zf_ref
        par = par_ref[hh]
        r0 = pl.multiple_of(ci * C, C)
        return _hgrn_chain(q_ref[pl.ds(r0, C), :].astype(F32), v_ref[pl.ds(r0, C), :],
                           z_ref[pl.ds(r0, C), :].astype(F32), par[3 * d:3 * d + 1], par[3 * d + 1:3 * d + 2],
                           par[3 * d + 2:3 * d + 3], code_ref[d], sign_ref, states, 2 * hh + d, d == 1)

    def make_body(accumulate):
        def body(i, carry):
            states = list(carry)
            jobs = []
            for hh in range(nh):
                jobs += [(2 * i, hh, 0), (2 * i + 1, hh, 0), (n - 1 - 2 * i, hh, 1), (n - 2 - 2 * i, hh, 1)]
            outs = _lockstep([chain(ci, hh, d, states) for ci, hh, d in jobs])
            for (ci, hh, _), o in zip(jobs, outs):
                r0 = pl.multiple_of(ci * C, C)
                if accumulate:
                    acc_ref[hh, pl.ds(r0, C), :] += o
                else:
                    acc_ref[hh, pl.ds(r0, C), :] = o
            return tuple(states)
        return body

    zero = jnp.zeros((LANES, HG_DK), F32)
    carry = lax.fori_loop(0, n // 4, make_body(False), (zero,) * (2 * nh))
    lax.fori_loop(n // 4, n // 2, make_body(True), carry)

    ft = 256

    def fin(i, _):
        r0 = pl.multiple_of(i * ft, ft)
        for hh in range(nh):
            o = acc_ref[hh, pl.ds(r0, ft), :]
            y = o * lax.rsqrt(jnp.mean(o * o, axis=-1, keepdims=True) + NORM_EPS) * par_ref[hh][6:7]
            g = head_refs[hh][4][pl.ds(r0, ft), :].astype(F32)
            o_ref[hh, pl.ds(r0, ft), :] = (y * (g * _sigmoid(g))).astype(BF16)
        return 0

    lax.fori_loop(0, S // ft, fin, 0)


def _hgrn(proj, par, B, S):
    nh = HG_HEADS_PER_STEP

    def spec(cb0, hh):
        return pl.BlockSpec((None, S, LANES), lambda b, h: (cb0 + nh * h + hh, b, 0))

    in_specs = []
    for hh in range(nh):
        in_specs += [spec(CB_HQ, hh), spec(CB_HF, hh), spec(CB_HB, hh), spec(CB_HV, hh), spec(CB_HG, hh)]
    in_specs.append(pl.BlockSpec((nh, SUBLANES, LANES), lambda b, h: (h, 0, 0)))
    return pl.pallas_call(
        functools.partial(_hgrn_kernel, S=S),
        grid=(B, HG_HEADS // nh),
        in_specs=in_specs,
        out_specs=pl.BlockSpec((nh, S, LANES), lambda b, h: (h, b, 0)),
        out_shape=jax.ShapeDtypeStruct((HG_HEADS, B * S, LANES), BF16),
        scratch_shapes=[pltpu.VMEM((nh, S, LANES), F32), pltpu.VMEM((2, HG_CHUNK, HG_CHUNK), jnp.int32),
                        pltpu.VMEM((HG_LEVELS, HG_CHUNK, LANES), F32)],
        compiler_params=_params(("parallel", "parallel")),
        name="hgrn2",
    )(*([proj] * (5 * nh)), par)


def _linear_scan_tile(a, u, carry, reverse):
    n = a.shape[0]
    nv = n // SUBLANES
    a3 = a.reshape(nv, SUBLANES, LANES)
    u3 = u.reshape(nv, SUBLANES, LANES)
    sub = lax.broadcasted_iota(jnp.int32, (nv, SUBLANES, LANES), 1)
    d = 1
    while d < SUBLANES:
        if reverse:
            keep = sub < SUBLANES - d
            shift = SUBLANES - d
        else:
            keep = sub >= d
            shift = d
        a_s = jnp.where(keep, pltpu.roll(a3, shift, 1), 1.0)
        u_s = jnp.where(keep, pltpu.roll(u3, shift, 1), 0.0)
        u3 = a3 * u_s + u3
        a3 = a3 * a_s
        d *= 2
        yield
    edge = 0 if reverse else SUBLANES - 1
    a_tot = jnp.broadcast_to(a3[:, edge:edge + 1, :], a3.shape)
    u_tot = jnp.broadcast_to(u3[:, edge:edge + 1, :], u3.shape)
    groups = [None] * nv
    for g in (range(nv - 1, -1, -1) if reverse else range(nv)):
        groups[g] = u3[g] + a3[g] * carry
        carry = u_tot[g] + a_tot[g] * carry
        if g % 4 == 0:
            yield
    return jnp.stack(groups).reshape(n, LANES), carry


def _rg_kernel(x_ref, gate_ref, par_ref, w_ref, o_ref, xpad_ref, xc_ref, h_ref, *, S):
    TS = RG_TILE
    n = S // TS
    par = par_ref[...]
    conv_b = par[4:5]

    xpad_ref[pl.ds(0, RG_HALO), :] = jnp.zeros((RG_HALO, LANES), F32)
    xpad_ref[pl.ds(S + RG_HALO, RG_HALO), :] = jnp.zeros((RG_HALO, LANES), F32)

    def copy(i, _):
        r0 = pl.multiple_of(i * TS, TS)
        xpad_ref[pl.ds(r0 + RG_HALO, TS), :] = x_ref[pl.ds(r0, TS), :].astype(F32)
        return 0

    lax.fori_loop(0, n, copy, 0)

    def conv(i, _):
        r0 = pl.multiple_of(i * TS, TS)
        xc = conv_b
        for j in range(RG_CONV_W):
            off = RG_HALO - RG_CONV_W // 2 + j
            xc = xc + par[j:j + 1] * xpad_ref[pl.ds(r0 + off, TS), :]
        xc_ref[pl.ds(r0, TS), :] = xc
        return 0

    lax.fori_loop(0, n, conv, 0)

    def tile(i, carries, d, final):
        r0 = pl.multiple_of(i * TS, TS)
        xc = xc_ref[pl.ds(r0, TS), :]
        gts = _dot(xc.astype(BF16), w_ref[:, d * 2 * LANES:(d + 1) * 2 * LANES])
        yield
        r = _sigmoid(gts[:, :LANES] + par[5 + 2 * d:6 + 2 * d])
        ig = _sigmoid(gts[:, LANES:] + par[6 + 2 * d:7 + 2 * d])
        lam = par[9 + d:10 + d]
        softplus_neg_lam = jnp.maximum(-lam, 0.0) + jnp.log1p(jnp.exp(-jnp.abs(lam)))
        log_a = (-RG_C) * r * softplus_neg_lam
        a = jnp.exp(log_a)
        y = 1.0 - a * a
        u = jnp.where(y > 0.0, y * lax.rsqrt(y), 0.0) * (ig * xc)
        yield
        h, carries[d] = yield from _linear_scan_tile(a, u, carries[d], reverse=(d == 1))
        yield
        if final:
            gate = gate_ref[pl.ds(r0, TS), :].astype(F32)
            o_ref[pl.ds(r0, TS), :] = (jax.nn.gelu(gate) * (h_ref[pl.ds(r0, TS), :] + h)).astype(BF16)
        else:
            h_ref[pl.ds(r0, TS), :] = h

    def make_body(final):
        def body(i, carry):
            carries = list(carry)
            _lockstep([tile(i, carries, 0, final), tile(n - 1 - i, carries, 1, final)])
            return tuple(carries)
        return body

    zero = jnp.zeros((SUBLANES, LANES), F32)
    carry = lax.fori_loop(0, n // 2, make_body(False), (zero, zero))
    lax.fori_loop(n // 2, n, make_body(True), carry)


def _rglru(proj, par, w4, B, S):
    ng = RG_WIDTH // LANES

    def spec(cb0):
        return pl.BlockSpec((None, S, LANES), lambda b, g: (cb0 + g, b, 0))

    return pl.pallas_call(
        functools.partial(_rg_kernel, S=S),
        grid=(B, ng),
        in_specs=[spec(CB_RX), spec(CB_RGATE),
                  pl.BlockSpec((None, 2 * SUBLANES, LANES), lambda b, g: (g, 0, 0)),
                  pl.BlockSpec((None, LANES, 4 * LANES), lambda b, g: (g, 0, 0))],
        out_specs=pl.BlockSpec((None, S, LANES), lambda b, g: (g, b, 0)),
        out_shape=jax.ShapeDtypeStruct((ng, B * S, LANES), BF16),
        scratch_shapes=[pltpu.VMEM((S + 2 * RG_HALO, LANES), F32), pltpu.VMEM((S, LANES), F32),
                        pltpu.VMEM((S, LANES), F32)],
        compiler_params=_params(("parallel", "parallel")),
        name="rglru",
    )(proj, proj, par, w4)


def _attn_kernel(q0_ref, k0_ref, v0_ref, q1_ref, k1_ref, v1_ref, q2_ref, k2_ref, v2_ref, par_ref, o_ref,
                 perm_ref, qd_ref, kd_ref, vd_ref, bias_ref, og_ref, lg_ref, *, S):
    refs = ((q0_ref, k0_ref, v0_ref), (q1_ref, k1_ref, v1_ref), (q2_ref, k2_ref, v2_ref))
    ct = ATT_TILE

    for g, (_, dil) in enumerate(ATT_GROUPS):
        q_ref, k_ref, v_ref = refs[g]
        par = par_ref[g]
        qg, kg, slope = par[0:1], par[1:2], par[2:3, 0:1]
        L = S // dil
        Q = min(ATT_QBLK, L)
        KW = min(Q + 2 * ATT_RADIUS, L)
        nq = L // Q
        nq_shift = int(math.log2(nq))

        for var in range(3):
            qi = var * ATT_RADIUS + lax.broadcasted_iota(jnp.int32, (Q, KW), 0)
            rel = jnp.abs(qi - lax.broadcasted_iota(jnp.int32, (Q, KW), 1))
            bias_ref[var, :Q, :KW] = jnp.where(rel <= ATT_RADIUS, -slope * rel.astype(F32), NEG_INF)

        per = ct // dil
        if dil > 1:
            pi = lax.broadcasted_iota(jnp.int32, (ct, ct), 0)
            pj = lax.broadcasted_iota(jnp.int32, (ct, ct), 1)
            src_tok = (pi & (per - 1)) * dil + lax.shift_right_logical(pi, int(math.log2(per)))
            perm_ref[...] = jnp.where(pj == src_tok, 1.0, 0.0).astype(BF16)

        def prep_tile(i, q_ref=q_ref, k_ref=k_ref, v_ref=v_ref, qg=qg, kg=kg, dil=dil, per=per, L=L):
            r0 = pl.multiple_of(i * ct, ct)
            xs = (q_ref[pl.ds(r0, ct), :].astype(F32), k_ref[pl.ds(r0, ct), :].astype(F32))
            sums = [jnp.sum(x * x, axis=-1, keepdims=True) for x in xs]
            yield
            tiles = [(x * lax.rsqrt(ss * (1.0 / LANES) + NORM_EPS) * gain).astype(BF16)
                     for x, ss, gain in zip(xs, sums, (qg, kg))]
            tiles.append(v_ref[pl.ds(r0, ct), :])
            if dil > 1:
                y = _dot(perm_ref[...], jnp.concatenate(tiles, axis=1))
                tiles = [y[:, c * LANES:(c + 1) * LANES] for c in range(3)]
                yield
            for dst_ref, y in zip((qd_ref, kd_ref, vd_ref), tiles):
                if dil == 1:
                    dst_ref[pl.ds(r0, ct), :] = y
                else:
                    y = y.astype(BF16)
                    for r in range(dil):
                        dst_ref[pl.ds(pl.multiple_of(r * L + i * per, 16), per), :] = y[r * per:(r + 1) * per, :]

        prep_unroll = min(ATT_PREP_UNROLL, S // ct)

        def prep(i, _, prep_tile=prep_tile, prep_unroll=prep_unroll):
            _lockstep([prep_tile(i * prep_unroll + u) for u in range(prep_unroll)])
            return 0

        lax.fori_loop(0, S // ct // prep_unroll, prep, 0)

        def rows(start, size, dil=dil):
            if dil == 1:
                return pl.ds(start, size)
            return pl.ds(start, size, stride=dil)

        unroll = min(ATT_UNROLL, dil * nq)

        def qblock(it, g=g, dil=dil, L=L, Q=Q, KW=KW, nq=nq, nq_shift=nq_shift, rows=rows):
            r = lax.shift_right_logical(it, nq_shift)
            m0 = (it & (nq - 1)) * Q
            ks = jnp.clip(m0 - ATT_RADIUS, 0, L - KW)
            var = lax.shift_right_logical(m0 - ks, int(math.log2(ATT_RADIUS)))
            base = r * L
            qn = qd_ref[pl.ds(pl.multiple_of(base + m0, 16), Q), :]
            kk = kd_ref[pl.ds(pl.multiple_of(base + ks, 16), KW), :]
            s = _dot_nt(qn, kk)
            yield
            s = s + bias_ref[var, :Q, :KW]
            m = jnp.max(s, axis=-1, keepdims=True)
            yield
            pb = jnp.exp(s - m).astype(BF16)
            vv = vd_ref[pl.ds(pl.multiple_of(base + ks, 16), KW), :]
            ol = _dot(pb, jnp.concatenate([vv, jnp.ones((KW, LANES), BF16)], axis=1))
            yield
            l = ol[:, LANES:]
            og_ref[g, rows(m0 * dil + r, Q), :] = ol[:, :LANES] / l
            lg_ref[g, rows(m0 * dil + r, Q), :] = m + jnp.log(l)

        def qblocks(i, _, unroll=unroll, qblock=qblock):
            _lockstep([qblock(i * unroll + u) for u in range(unroll)])
            return 0

        lax.fori_loop(0, (dil * nq) // unroll, qblocks, 0)

    def merge(i, _):
        r0 = pl.multiple_of(i * ct, ct)
        l0, l1, l2 = (lg_ref[g, pl.ds(r0, ct), :] for g in range(3))
        m = jnp.maximum(jnp.maximum(l0, l1), l2)
        e0, e1, e2 = jnp.exp(l0 - m), jnp.exp(l1 - m), jnp.exp(l2 - m)
        o = (e0 * og_ref[0, pl.ds(r0, ct), :] + e1 * og_ref[1, pl.ds(r0, ct), :]
             + e2 * og_ref[2, pl.ds(r0, ct), :]) / (e0 + e1 + e2)
        o_ref[pl.ds(r0, ct), :] = o.astype(BF16)
        return 0

    lax.fori_loop(0, S // ct, merge, 0)


def _attention(proj, par, B, S):
    H = ATT_HEADS_PER_GROUP
    ngroups = len(ATT_GROUPS)

    def spec(cb0, g):
        return pl.BlockSpec((None, S, LANES), lambda b, j: (cb0 + g * H + j, b, 0))

    in_specs = []
    for g in range(ngroups):
        in_specs += [spec(CB_AQ, g), spec(CB_AK, g), spec(CB_AV, g)]
    in_specs.append(pl.BlockSpec((None, ngroups, SUBLANES, LANES), lambda b, j: (j, 0, 0, 0)))
    return pl.pallas_call(
        functools.partial(_attn_kernel, S=S),
        grid=(B, H),
        in_specs=in_specs,
        out_specs=pl.BlockSpec((None, S, LANES), lambda b, j: (j, b, 0)),
        out_shape=jax.ShapeDtypeStruct((H, B * S, LANES), BF16),
        scratch_shapes=[pltpu.VMEM((ATT_TILE, ATT_TILE), BF16),
                        pltpu.VMEM((S, LANES), BF16), pltpu.VMEM((S, LANES), BF16), pltpu.VMEM((S, LANES), BF16),
                        pltpu.VMEM((3, ATT_QBLK, ATT_QBLK + 2 * ATT_RADIUS), F32),
                        pltpu.VMEM((ngroups, S, LANES), F32), pltpu.VMEM((ngroups, S, LANES), F32)],
        compiler_params=_params(("parallel", "parallel")),
        name="attention",
    )(*([proj] * (3 * ngroups)), par)


def _cat_heads(ref):
    return jnp.concatenate([ref[c] for c in range(ref.shape[0])], axis=-1)


def _merge_kernel(x_ref, hg_ref, rg_ref, att_ref, ga_ref, gb_ref, gc_ref, wb_ref, wo_ref, ng_ref,
                  x1_ref, h2_ref):
    merged = _sigmoid(_cat_heads(ga_ref).astype(F32)) * _dot(_cat_heads(hg_ref), wb_ref[0])
    merged += _sigmoid(_cat_heads(gb_ref).astype(F32)) * _dot(_cat_heads(rg_ref), wb_ref[1])
    merged += _sigmoid(_cat_heads(gc_ref).astype(F32)) * _dot(_cat_heads(att_ref), wb_ref[2])
    x1 = x_ref[...] + _dot(merged.astype(BF16), wo_ref[...])
    x1_ref[...] = x1
    ms = jnp.mean(x1 * x1, axis=-1, keepdims=True)
    h2_ref[...] = (x1 * lax.rsqrt(ms + NORM_EPS) * ng_ref[...]).astype(BF16)


def _merge(x, proj, hg, rg, att, wb, wo, ng):
    T = x.shape[0]
    tm = MERGE_TM
    nb = BRANCH_W // LANES
    ngate = D_MODEL // LANES

    branch = pl.BlockSpec((nb, tm, LANES), lambda i: (0, i, 0))

    def gate(n):
        return pl.BlockSpec((ngate, tm, LANES), lambda i: (CB_GATES // ngate + n, i, 0))

    row = pl.BlockSpec((tm, D_MODEL), lambda i: (i, 0))
    return pl.pallas_call(
        _merge_kernel,
        grid=(T // tm,),
        in_specs=[row, branch, branch, branch, gate(0), gate(1), gate(2),
                  pl.BlockSpec((3, BRANCH_W, D_MODEL), lambda i: (0, 0, 0)),
                  pl.BlockSpec((D_MODEL, D_MODEL), lambda i: (0, 0)),
                  pl.BlockSpec((1, D_MODEL), lambda i: (0, 0))],
        out_specs=[row, row],
        out_shape=[jax.ShapeDtypeStruct((T, D_MODEL), F32), jax.ShapeDtypeStruct((T, D_MODEL), BF16)],
        compiler_params=_params(("parallel",)),
        name="merge",
    )(x, hg, rg, att, proj, proj, proj, wb, wo, ng)


def _ffn_kernel(h_ref, x_ref, w1_ref, w3_ref, w2_ref, o_ref, acc_ref):
    f = pl.program_id(1)
    h = h_ref[...]
    h1 = _dot(h, w1_ref[...])
    y = _dot((h1 * _sigmoid(h1) * _dot(h, w3_ref[...])).astype(BF16), w2_ref[...])

    @pl.when(f == 0)
    def _():
        acc_ref[...] = x_ref[...] + y

    @pl.when(f > 0)
    def _():
        acc_ref[...] += y

    @pl.when(f == pl.num_programs(1) - 1)
    def _():
        o_ref[...] = acc_ref[...]


def _ffn_dense(h2, x1, w1, w3, w2):
    T = h2.shape[0]
    tm, tf = FFN_TM, FFN_TF
    return pl.pallas_call(
        _ffn_kernel,
        grid=(T // tm, FFN_DENSE // tf),
        in_specs=[pl.BlockSpec((tm, D_MODEL), lambda i, f: (i, 0)),
                  pl.BlockSpec((tm, D_MODEL), lambda i, f: (i, 0)),
                  pl.BlockSpec((D_MODEL, tf), lambda i, f: (0, f), pipeline_mode=pl.Buffered(1)),
                  pl.BlockSpec((D_MODEL, tf), lambda i, f: (0, f), pipeline_mode=pl.Buffered(1)),
                  pl.BlockSpec((tf, D_MODEL), lambda i, f: (f, 0), pipeline_mode=pl.Buffered(1))],
        out_specs=pl.BlockSpec((tm, D_MODEL), lambda i, f: (i, 0)),
        out_shape=jax.ShapeDtypeStruct((T, D_MODEL), F32),
        scratch_shapes=[pltpu.VMEM((tm, D_MODEL), F32)],
        compiler_params=_params(("parallel", "arbitrary")),
        name="ffn_dense",
    )(h2, x1, w1, w3, w2)


def _router_kernel(h_ref, wr_ref, o_ref, cnt_ref, base_ref):
    tm = ROUTER_TM

    @pl.when(pl.program_id(0) == 0)
    def _():
        base_ref[...] = jnp.zeros_like(base_ref)

    lane = lax.broadcasted_iota(jnp.int32, (tm, LANES), 1)
    logits = jnp.where(lane < N_EXPERTS, _dot(h_ref[...], wr_ref[...]), -jnp.inf)
    m1 = jnp.max(logits, axis=-1, keepdims=True)
    i1 = jnp.min(jnp.where(logits == m1, lane, LANES), axis=-1, keepdims=True)
    rest = jnp.where(lane == i1, -jnp.inf, logits)
    m2 = jnp.max(rest, axis=-1, keepdims=True)
    i2 = jnp.min(jnp.where(rest == m2, lane, LANES), axis=-1, keepdims=True)
    e21 = jnp.exp(m2 - m1)
    g1 = 1.0 / (1.0 + e21)
    g2 = e21 / (1.0 + e21)

    onehot = jnp.where(lane == i1, 1.0, jnp.where(lane == i2, 1.0, 0.0))
    ti = lax.broadcasted_iota(jnp.int32, (tm, tm), 0)
    si = lax.broadcasted_iota(jnp.int32, (tm, tm), 1)
    before = jnp.where(si < ti, 1.0, 0.0).astype(BF16)
    pos = _dot(before, onehot.astype(BF16)) + base_ref[0:1, :]
    r1 = jnp.sum(jnp.where(lane == i1, pos, 0.0), axis=-1, keepdims=True)
    r2 = jnp.sum(jnp.where(lane == i2, pos, 0.0), axis=-1, keepdims=True)
    total = base_ref[0:1, :] + jnp.sum(onehot, axis=0, keepdims=True)
    base_ref[...] = jnp.broadcast_to(total, base_ref.shape)
    cnt_ref[...] = jnp.broadcast_to(total, cnt_ref.shape)

    out = jnp.where(lane == 0, i1.astype(F32), 0.0)
    out = jnp.where(lane == 1, i2.astype(F32), out)
    out = jnp.where(lane == 2, g1, out)
    out = jnp.where(lane == 3, g2, out)
    out = jnp.where(lane == 4, r1, out)
    out = jnp.where(lane == 5, r2, out)
    o_ref[...] = out


def _router(h2, wr):
    T = h2.shape[0]
    tm = ROUTER_TM
    return pl.pallas_call(
        _router_kernel,
        grid=(T // tm,),
        in_specs=[pl.BlockSpec((tm, D_MODEL), lambda i: (i, 0)),
                  pl.BlockSpec((D_MODEL, LANES), lambda i: (0, 0))],
        out_specs=[pl.BlockSpec((tm, LANES), lambda i: (i, 0)),
                   pl.BlockSpec((SUBLANES, LANES), lambda i: (0, 0))],
        out_shape=[jax.ShapeDtypeStruct((T, LANES), F32), jax.ShapeDtypeStruct((SUBLANES, LANES), F32)],
        scratch_shapes=[pltpu.VMEM((SUBLANES, LANES), F32)],
        compiler_params=_params(("arbitrary",)),
        name="moe_router",
    )(h2, wr)


def _expert_kernel(be_ref, nu_ref, x_ref, w1_ref, w3_ref, w2_ref, o_ref, acc_ref):
    i = pl.program_id(0)
    f = pl.program_id(1)

    @pl.when(f == 0)
    def _():
        acc_ref[...] = jnp.zeros_like(acc_ref)

    @pl.when(i < nu_ref[0])
    def _():
        x = x_ref[...]
        h1 = _dot(x, w1_ref[...])
        acc_ref[...] += _dot((h1 * _sigmoid(h1) * _dot(x, w3_ref[...])).astype(BF16), w2_ref[...])

    @pl.when(f == pl.num_programs(1) - 1)
    def _():
        o_ref[...] = acc_ref[...].astype(BF16)


def _experts(rows, block_expert, n_used, w1, w3, w2):
    n_rows = rows.shape[0]
    R, tf = MOE_ROWS, MOE_TF
    grid_spec = pltpu.PrefetchScalarGridSpec(
        num_scalar_prefetch=2,
        grid=(n_rows // R, FFN_EXPERT // tf),
        in_specs=[pl.BlockSpec((R, D_MODEL), lambda i, f, be, nu: (i, 0)),
                  pl.BlockSpec((None, D_MODEL, tf), lambda i, f, be, nu: (be[i], 0, f)),
                  pl.BlockSpec((None, D_MODEL, tf), lambda i, f, be, nu: (be[i], 0, f)),
                  pl.BlockSpec((None, tf, D_MODEL), lambda i, f, be, nu: (be[i], f, 0))],
        out_specs=pl.BlockSpec((R, D_MODEL), lambda i, f, be, nu: (i, 0)),
        scratch_shapes=[pltpu.VMEM((R, D_MODEL), F32)],
    )
    return pl.pallas_call(
        _expert_kernel,
        grid_spec=grid_spec,
        out_shape=jax.ShapeDtypeStruct((n_rows, D_MODEL), BF16),
        compiler_params=_params(("parallel", "arbitrary")),
        name="moe_experts",
    )(block_expert, n_used, rows, w1, w3, w2)


def _combine_kernel(x_ref, y1_ref, y2_ref, info_ref, o_ref):
    info = info_ref[...]
    o_ref[...] = (x_ref[...] + info[:, 2:3] * y1_ref[...].astype(F32)
                  + info[:, 3:4] * y2_ref[...].astype(F32))


def _combine(x1, y1, y2, info):
    T = x1.shape[0]
    tm = COMBINE_TM
    row = pl.BlockSpec((tm, D_MODEL), lambda i: (i, 0))
    return pl.pallas_call(
        _combine_kernel,
        grid=(T // tm,),
        in_specs=[row, row, row, pl.BlockSpec((tm, LANES), lambda i: (i, 0))],
        out_specs=row,
        out_shape=jax.ShapeDtypeStruct((T, D_MODEL), F32),
        compiler_params=_params(("parallel",)),
        name="moe_combine",
    )(x1, y1, y2, info)


def _moe(h2, x1, wr, w1, w3, w2):
    T = h2.shape[0]
    R = MOE_ROWS
    n_blocks = -(-(T * TOP_K) // R) + N_EXPERTS
    info, cnt = _router(h2, wr)
    expert = info[:, 0:2].astype(jnp.int32)
    rank = info[:, 4:6].astype(jnp.int32)
    counts = cnt[0, :N_EXPERTS].astype(jnp.int32)
    blocks_per = (counts + R - 1) // R
    bend = jnp.cumsum(blocks_per)
    pstart = (bend - blocks_per) * R
    lanes = jnp.arange(N_EXPERTS, dtype=jnp.int32)
    dest = jnp.sum(jnp.where(expert[..., None] == lanes, pstart, 0), axis=-1) + rank
    n_used = bend[-1:]
    block_expert = jnp.minimum(
        jnp.searchsorted(bend, jnp.arange(n_blocks, dtype=jnp.int32), side='right'),
        N_EXPERTS - 1).astype(jnp.int32)
    n_pad = n_blocks * R - T * TOP_K
    pad_end = jnp.cumsum(blocks_per * R - counts)
    pad_expert = jnp.sum(jnp.arange(n_pad, dtype=jnp.int32)[:, None] >= pad_end[None, :], axis=1).astype(jnp.int32)
    keys = jnp.concatenate([expert.reshape(-1) * 2, pad_expert * 2 + 1])
    token = jnp.broadcast_to(jnp.arange(T, dtype=jnp.int32)[:, None], (T, TOP_K)).reshape(-1)
    vals = jnp.concatenate([token, jnp.zeros((n_pad,), jnp.int32)])
    _, tok_of_row = lax.sort((keys, vals), num_keys=1, is_stable=True)
    rows = h2.at[tok_of_row].get(mode='promise_in_bounds')
    y = _experts(rows, block_expert, n_used, w1, w3, w2)
    y1 = y.at[dest[:, 0]].get(mode='promise_in_bounds')
    y2 = y.at[dest[:, 1]].get(mode='promise_in_bounds')
    return _combine(x1, y1, y2, info)


def _pad_rows(rows, n):
    rows = jnp.stack(rows, axis=-2)
    pad = [(0, 0)] * (rows.ndim - 2) + [(0, n - rows.shape[-2]), (0, 0)]
    return jnp.pad(rows, pad)


def _hgrn_params(lb, norm_g):
    lb = lb.reshape(2, HG_HEADS, HG_DK)
    rows = []
    for d in range(2):
        rows += [jnp.log(jnp.maximum(lb[d], LB_EPS)), jnp.log1p(-lb[d]), 1.0 - lb[d]]
    rows.append(norm_g.reshape(HG_HEADS, HG_DK))
    return _pad_rows(rows, SUBLANES)


def _rg_params(conv_w, conv_b, b_a, b_x, lam):
    ng = RG_WIDTH // LANES
    g = lambda t: t.reshape(ng, LANES)
    rows = [g(conv_w[j]) for j in range(RG_CONV_W)] + [g(conv_b)]
    rows += [g(b_a[0]), g(b_x[0]), g(b_a[1]), g(b_x[1]), g(lam[0]), g(lam[1])]
    return _pad_rows(rows, 2 * SUBLANES)


def _rg_gate_weights(w_a, w_x):
    ng = RG_WIDTH // LANES
    per = RG_BLOCKS // ng

    def dense(w):
        w = w.reshape(ng, per, RG_BLOCK_W, RG_BLOCK_W)
        eye = jnp.eye(per, dtype=w.dtype)
        return jnp.einsum('gpcd,pq->gpcqd', w, eye).reshape(ng, LANES, LANES)

    return jnp.concatenate([dense(w_a[0]), dense(w_x[0]), dense(w_a[1]), dense(w_x[1])], axis=-1).astype(BF16)


def _attn_params(q_g, k_g):
    H = ATT_HEADS_PER_GROUP
    groups = []
    for gi, (_, dil) in enumerate(ATT_GROUPS):
        heads = jnp.arange(gi * H + 1, (gi + 1) * H + 1, dtype=F32)
        slopes = 2.0 ** (-8.0 * heads / ATT_HEADS) * dil
        rows = [jnp.broadcast_to(q_g * (ATT_DH ** -0.5), (H, LANES)), jnp.broadcast_to(k_g, (H, LANES)),
                jnp.broadcast_to(slopes[:, None], (H, LANES))]
        groups.append(_pad_rows(rows, SUBLANES))
    return jnp.stack(groups, axis=1)


def _trunk(x3, p):
    B, S, _ = x3.shape
    x = x3.reshape(B * S, D_MODEL)
    sm = jax.nn.softmax(p['hg_lb_logits'].astype(F32), axis=1)
    lower_bounds = jnp.cumsum(sm, axis=1) - sm[:, :1]
    for l in range(DEPTH):
        proj = _inproj(x, p['norm_mix_g'][l][None], p['w_in'][l].astype(BF16))
        hg = _hgrn(proj, _hgrn_params(lower_bounds[:, l], p['hg_norm_g'][l]), B, S)
        rg = _rglru(proj,
                    _rg_params(p['rg_conv_w'][l], p['rg_conv_b'][l], p['rg_b_a'][l], p['rg_b_x'][l],
                               p['rg_lambda'][l]),
                    _rg_gate_weights(p['rg_w_a'][l], p['rg_w_x'][l]), B, S)
        att = _attention(proj, _attn_params(p['attn_q_g'][l], p['attn_k_g'][l]), B, S)
        x1, h2 = _merge(x, proj, hg, rg, att, p['w_branch'][l].astype(BF16), p['w_out'][l].astype(BF16),
                        p['norm_ffn_g'][l][None])
        j = l // 2
        if l % 2 == 0:
            x = _ffn_dense(h2, x1, p['ffn_w1'][j].astype(BF16), p['ffn_w3'][j].astype(BF16),
                           p['ffn_w2'][j].astype(BF16))
        else:
            wr = jnp.pad(p['moe_router'][j], ((0, 0), (0, LANES - N_EXPERTS))).astype(BF16)
            x = _moe(h2, x1, wr, p['moe_w1'][j].astype(BF16), p['moe_w3'][j].astype(BF16),
                     p['moe_w2'][j].astype(BF16))
    return x.reshape(B, S, D_MODEL)


def kernel(x_prompt, x_sample, norm_mix_g, w_in, hg_lb_logits, hg_norm_g, rg_conv_w, rg_conv_b, rg_w_a, rg_b_a,
           rg_w_x, rg_b_x, rg_lambda, attn_q_g, attn_k_g, w_branch, w_out, norm_ffn_g, ffn_w1, ffn_w3, ffn_w2,
           moe_router, moe_w1, moe_w3, moe_w2):
    p = dict(norm_mix_g=norm_mix_g, w_in=w_in, hg_lb_logits=hg_lb_logits, hg_norm_g=hg_norm_g,
             rg_conv_w=rg_conv_w, rg_conv_b=rg_conv_b, rg_w_a=rg_w_a, rg_b_a=rg_b_a, rg_w_x=rg_w_x,
             rg_b_x=rg_b_x, rg_lambda=rg_lambda, attn_q_g=attn_q_g, attn_k_g=attn_k_g,
             w_branch=w_branch, w_out=w_out, norm_ffn_g=norm_ffn_g, ffn_w1=ffn_w1, ffn_w3=ffn_w3,
             ffn_w2=ffn_w2, moe_router=moe_router, moe_w1=moe_w1, moe_w3=moe_w3, moe_w2=moe_w2)
    return (_trunk(x_prompt, p), _trunk(x_sample, p))
```
